```python
import jax, jax.numpy as jnp
from jax import lax
import numpy as np

D_MODEL = 1024
BATCH = 8
SEQ = 4096
DEPTH = 1
DEC_BATCH = 128
DEC_SEQ = 8
PAST_LEN = 16384
PAGE_SIZE = 128

SW_HEADS = 8
SW_KV_HEADS = 2
SW_GROUP = SW_HEADS // SW_KV_HEADS
SW_HEAD_DIM = 64
WINDOW = 128
SW_BLOCK = 128
HG_HEADS = 4
HG_KEY_DIM = 128
HG_VAL_DIM = 128
HG_CHUNK = 64
MEM_LEN = 256
MX_HEADS = 4
MX_HEAD_DIM = 128

SW_Q = SW_HEADS * SW_HEAD_DIM
SW_KV = SW_KV_HEADS * SW_HEAD_DIM
HG_K = HG_HEADS * HG_KEY_DIM
HG_V = HG_HEADS * HG_VAL_DIM
MX_W = MX_HEADS * MX_HEAD_DIM
N_BRANCH = 3
IN_SIZES = (SW_Q, SW_KV, SW_KV, HG_K, HG_K, HG_V, HG_V, MX_W, N_BRANCH * D_MODEL)
IN_WIDTH = sum(IN_SIZES)
IN_OFFSETS = tuple(int(o) for o in np.cumsum(IN_SIZES)[:-1])
D_FF = ((8 * D_MODEL + 3 * 256 - 1) // (3 * 256)) * 256
ALPHA = (2.0 * DEPTH) ** 0.25
BETA = (8.0 * DEPTH) ** -0.25
LN_EPS = 1e-5
RMS_EPS = 1e-6

kernel_name = 'hybrid_swa_sink_hgrn2_memory_decode_step'


def layer_norm(x, w, b):
    xf = x.astype(jnp.float32)
    mu = jnp.mean(xf, axis=-1, keepdims=True)
    var = jnp.mean(jnp.square(xf - mu), axis=-1, keepdims=True)
    return ((xf - mu) * lax.rsqrt(var + LN_EPS) * w.astype(jnp.float32) + b.astype(jnp.float32)).astype(x.dtype)


def alibi_slopes(n):
    return 2.0 ** (-8.0 * jnp.arange(1, n + 1, dtype=jnp.float32) / n)


def sink_attention(q, k, v, dist, valid, sinks):
    s = jnp.einsum('bnqkgd,bnskd->bnkgqs', q, k, preferred_element_type=jnp.float32) * (SW_HEAD_DIM ** -0.5)
    slopes = alibi_slopes(SW_HEADS).reshape(SW_KV_HEADS, SW_GROUP)[:, :, None, None]
    s = s - slopes * dist[:, None, None].astype(jnp.float32)
    s = jnp.where(valid[:, None, None], s, -jnp.inf)
    sink = sinks.astype(jnp.float32).reshape(SW_KV_HEADS, SW_GROUP)[:, :, None, None]
    m = jnp.maximum(jnp.max(s, axis=-1, keepdims=True), sink)
    p = jnp.exp(s - m)
    denom = jnp.sum(p, axis=-1, keepdims=True) + jnp.exp(sink - m)
    return jnp.einsum('bnkgqs,bnskd->bnqkgd', (p / denom).astype(v.dtype), v)


def window_attention_prompt(q, k, v, sinks):
    B, T = q.shape[:2]
    nb = T // SW_BLOCK
    qb = q.reshape(B, nb, SW_BLOCK, SW_KV_HEADS, SW_GROUP, SW_HEAD_DIM)

    def with_prev(a):
        a = a.reshape(B, nb, SW_BLOCK, SW_KV_HEADS, SW_HEAD_DIM)
        prev = jnp.concatenate([jnp.zeros_like(a[:, :1]), a[:, :-1]], axis=1)
        return jnp.concatenate([prev, a], axis=2)

    kk, vv = with_prev(k), with_prev(v)
    blk = jnp.arange(nb)[:, None] * SW_BLOCK
    qpos = blk + jnp.arange(SW_BLOCK)[None]
    kpos = blk - SW_BLOCK + jnp.arange(2 * SW_BLOCK)[None]
    dist = qpos[:, :, None] - kpos[:, None, :]
    valid = (dist >= 0) & (dist < WINDOW) & (kpos[:, None, :] >= 0)
    o = sink_attention(qb, kk, vv, dist, valid, sinks)
    return o.reshape(B, T, SW_Q)


def window_attention_sample(q, k, v, k_buf, v_buf, sinks):
    B, T = q.shape[:2]
    W = k_buf.shape[1]
    kk = jnp.concatenate([k_buf, k], axis=1)
    vv = jnp.concatenate([v_buf, v], axis=1)
    qpos = PAST_LEN + jnp.arange(T)
    kpos = jnp.concatenate([PAST_LEN - W + jnp.arange(W), PAST_LEN + jnp.arange(T)])
    dist = (qpos[:, None] - kpos[None, :])[None]
    valid = (dist >= 0) & (dist < WINDOW)
    o = sink_attention(q.reshape(B, 1, T, SW_KV_HEADS, SW_GROUP, SW_HEAD_DIM),
                       kk[:, None], vv[:, None], dist, valid, sinks)
    return o.reshape(B, T, SW_Q), kk[:, -W:], vv[:, -W:]


def hgrn2_chunkwise(q, k, v, logf, s0):
    B, T = q.shape[:2]
    C = min(HG_CHUNK, T)
    nc = -(-T // C)
    pad = nc * C - T

    def blocks(a):
        a = jnp.pad(a, ((0, 0), (0, pad), (0, 0), (0, 0)))
        return a.reshape(B, nc, C, a.shape[2], a.shape[3]).transpose(1, 0, 3, 2, 4)

    causal = jnp.tril(jnp.ones((C, C), dtype=bool))[:, :, None]

    def step(S, inp):
        qc, kc, vc, gc = inp
        b = jnp.cumsum(gc, axis=2)
        o_inter = jnp.einsum('bhtd,bhde->bhte', qc * jnp.exp(b), S)
        diff = b[:, :, :, None, :] - b[:, :, None, :, :]
        decay = jnp.exp(jnp.where(causal, diff, -jnp.inf))
        A = jnp.einsum('bhtd,bhsd,bhtsd->bhts', qc, kc, decay)
        o = o_inter + jnp.einsum('bhts,bhse->bhte', A, vc)
        b_last = b[:, :, -1:, :]
        S_new = jnp.exp(b_last[:, :, 0, :])[..., None] * S + jnp.einsum('bhsd,bhse->bhde', kc * jnp.exp(b_last - b), vc)
        return S_new, o

    S_fin, o = lax.scan(step, s0, (blocks(q), blocks(k), blocks(v), blocks(logf)))
    o = o.transpose(1, 0, 3, 2, 4).reshape(B, nc * C, q.shape[2], v.shape[3])[:, :T]
    return o, S_fin


def memory_attention(q, mem_k, mem_v):
    B, T = q.shape[:2]
    s = jnp.einsum('bthd,bshd->bhts', q, mem_k, preferred_element_type=jnp.float32) * (MX_HEAD_DIM ** -0.5)
    p = jax.nn.softmax(s, axis=-1)
    return jnp.einsum('bhts,bshd->bthd', p.astype(mem_v.dtype), mem_v).reshape(B, T, MX_W)


def decoder_layer(x, k_buf, v_buf, hg_s0, mem_k, mem_v, lw):
    B, T, _ = x.shape
    f32 = jnp.float32
    proj = x @ lw['w_in']
    sq, sk, sv, hq, hf, hi, hg, mq, gl = jnp.split(proj, IN_OFFSETS, axis=-1)
    sq = sq.reshape(B, T, SW_HEADS, SW_HEAD_DIM)
    sk = sk.reshape(B, T, SW_KV_HEADS, SW_HEAD_DIM)
    sv = sv.reshape(B, T, SW_KV_HEADS, SW_HEAD_DIM)
    if k_buf is None:
        o_sw = window_attention_prompt(sq, sk, sv, lw['sinks'])
        new_k, new_v = sk[:, T - WINDOW:], sv[:, T - WINDOW:]
    else:
        o_sw, new_k, new_v = window_attention_sample(sq, sk, sv, k_buf, v_buf, lw['sinks'])
    lb = lw['hg_lb']
    f = lb + (1.0 - lb) * jax.nn.sigmoid(hf.astype(f32))
    hshape = (B, T, HG_HEADS, HG_KEY_DIM)
    q_h = jax.nn.silu(hq.astype(f32)).reshape(hshape)
    k_h = (1.0 - f).reshape(hshape)
    logf = jnp.log(f).reshape(hshape)
    v_h = hi.astype(f32).reshape(B, T, HG_HEADS, HG_VAL_DIM)
    s0 = jnp.zeros((B, HG_HEADS, HG_KEY_DIM, HG_VAL_DIM), f32) if hg_s0 is None else hg_s0.astype(f32)
    o_h, s_fin = hgrn2_chunkwise(q_h, k_h, v_h, logf, s0)
    o_h = o_h * lax.rsqrt(jnp.mean(jnp.square(o_h), axis=-1, keepdims=True) + RMS_EPS) * lw['hg_norm_w'].astype(f32)
    o_h = (o_h.reshape(B, T, HG_V) * jax.nn.silu(hg.astype(f32))).astype(x.dtype)
    o_m = memory_attention(mq.reshape(B, T, MX_HEADS, MX_HEAD_DIM), mem_k, mem_v)
    g = jax.nn.sigmoid(gl.astype(f32)).reshape(B, T, N_BRANCH, D_MODEL).astype(x.dtype)
    mix = (g[:, :, 0] * (o_sw @ lw['w_up_sw'])
           + g[:, :, 1] * (o_h @ lw['w_up_hg'])
           + g[:, :, 2] * (o_m @ lw['w_up_mx']))
    h = layer_norm(ALPHA * x + mix @ lw['w_o'], lw['ln1_w'], lw['ln1_b'])
    ff_gate, ff_up = jnp.split(h @ lw['w_ffn_in'], 2, axis=-1)
    y = layer_norm(ALPHA * h + (jax.nn.silu(ff_gate) * ff_up) @ lw['w_ffn_out'], lw['ln2_w'], lw['ln2_b'])
    return y, new_k, new_v, s_fin


def setup_inputs(seed: int = 0) -> dict:
    key = jax.random.key(seed)
    ks = jax.random.split(key, 26)

    def nrm(k, shape, scale=1.0):
        return jax.random.normal(k, shape, jnp.float32) * scale

    win_buf = min(WINDOW, PAST_LEN)
    return {
        'x_prompt': nrm(ks[0], (BATCH, SEQ, D_MODEL)),
        'x_sample': nrm(ks[1], (DEC_BATCH, DEC_SEQ, D_MODEL)),
        'cache_k_win': nrm(ks[2], (DEPTH, DEC_BATCH, win_buf, SW_KV_HEADS, SW_HEAD_DIM)),
        'cache_v_win': nrm(ks[3], (DEPTH, DEC_BATCH, win_buf, SW_KV_HEADS, SW_HEAD_DIM)),
        'state_hgrn': nrm(ks[4], (DEPTH, DEC_BATCH, HG_HEADS, HG_KEY_DIM, HG_VAL_DIM), 0.5),
        'cache_mem_k': nrm(ks[5], (DEPTH, DEC_BATCH, MEM_LEN, MX_HEADS, MX_HEAD_DIM)),
        'cache_mem_v': nrm(ks[6], (DEPTH, DEC_BATCH, MEM_LEN, MX_HEADS, MX_HEAD_DIM)),
        'mem_prompt': nrm(ks[7], (BATCH, MEM_LEN, D_MODEL)),
        'ln0_w': 1.0 + nrm(ks[8], (D_MODEL,), 0.02),
        'ln0_b': nrm(ks[9], (D_MODEL,), 0.02),
        'w_in': nrm(ks[10], (DEPTH, D_MODEL, IN_WIDTH), D_MODEL ** -0.5),
        'w_up_sw': nrm(ks[11], (DEPTH, SW_Q, D_MODEL), SW_Q ** -0.5),
        'w_up_hg': nrm(ks[12], (DEPTH, HG_V, D_MODEL), HG_V ** -0.5),
        'w_up_mx': nrm(ks[13], (DEPTH, MX_W, D_MODEL), MX_W ** -0.5),
        'sw_sinks': nrm(ks[14], (DEPTH, SW_HEADS), 0.5),
        'hg_lower_bound': nrm(ks[15], (DEPTH + 1, HG_K), 0.1),
        'hg_norm_w': 1.0 + nrm(ks[16], (DEPTH, HG_VAL_DIM), 0.02),
        'w_mem_kv': nrm(ks[17], (DEPTH, D_MODEL, 2 * MX_W), D_MODEL ** -0.5),
        'w_o': nrm(ks[18], (DEPTH, D_MODEL, D_MODEL), BETA * D_MODEL ** -0.5),
        'ln1_w': 1.0 + nrm(ks[19], (DEPTH, D_MODEL), 0.02),
        'ln1_b': nrm(ks[20], (DEPTH, D_MODEL), 0.02),
        'w_ffn_in': nrm(ks[21], (DEPTH, D_MODEL, 2 * D_FF), D_MODEL ** -0.5),
        'w_ffn_out': nrm(ks[22], (DEPTH, D_FF, D_MODEL), BETA * D_FF ** -0.5),
        'ln2_w': 1.0 + nrm(ks[23], (DEPTH, D_MODEL), 0.02),
        'ln2_b': nrm(ks[24], (DEPTH, D_MODEL), 0.02),
    }


def reference(x_prompt, x_sample, cache_k_win, cache_v_win, state_hgrn, cache_mem_k, cache_mem_v, mem_prompt,
              ln0_w, ln0_b, w_in, w_up_sw, w_up_hg, w_up_mx, sw_sinks, hg_lower_bound, hg_norm_w, w_mem_kv,
              w_o, ln1_w, ln1_b, w_ffn_in, w_ffn_out, ln2_w, ln2_b):
    lower = jnp.cumsum(jax.nn.softmax(hg_lower_bound.astype(jnp.float32), axis=0), axis=0)
    xp = layer_norm(x_prompt, ln0_w, ln0_b)
    xs = layer_norm(x_sample, ln0_w, ln0_b)
    bp = mem_prompt.shape[0]
    kp_l, vp_l, sp_l, mkp_l, mvp_l, ks_l, vs_l, ss_l = [], [], [], [], [], [], [], []
    for l in range(DEPTH):
        lw = {'w_in': w_in[l], 'w_up_sw': w_up_sw[l], 'w_up_hg': w_up_hg[l], 'w_up_mx': w_up_mx[l],
              'sinks': sw_sinks[l], 'hg_lb': lower[l], 'hg_norm_w': hg_norm_w[l], 'w_o': w_o[l],
              'ln1_w': ln1_w[l], 'ln1_b': ln1_b[l], 'w_ffn_in': w_ffn_in[l], 'w_ffn_out': w_ffn_out[l],
              'ln2_w': ln2_w[l], 'ln2_b': ln2_b[l]}
        mkv = (mem_prompt @ w_mem_kv[l]).reshape(bp, MEM_LEN, 2, MX_HEADS, MX_HEAD_DIM)
        mk, mv = mkv[:, :, 0], mkv[:, :, 1]
        xp, kp, vp, sp = decoder_layer(xp, None, None, None, mk, mv, lw)
        xs, ks_, vs_, ss = decoder_layer(xs, cache_k_win[l], cache_v_win[l], state_hgrn[l],
                                         cache_mem_k[l], cache_mem_v[l], lw)
        kp_l.append(kp); vp_l.append(vp); sp_l.append(sp); mkp_l.append(mk); mvp_l.append(mv)
        ks_l.append(ks_); vs_l.append(vs_); ss_l.append(ss)
    return (xp, xs,
            jnp.stack(kp_l), jnp.stack(vp_l), jnp.stack(sp_l), jnp.stack(mkp_l), jnp.stack(mvp_l),
            jnp.stack(ks_l), jnp.stack(vs_l), jnp.stack(ss_l))
```

```python
import functools

import numpy as np
import jax
import jax.numpy as jnp
from jax import lax
from jax.experimental import pallas as pl
from jax.experimental.pallas import tpu as pltpu

D_MODEL = 1024
DEPTH = 1
PAST_LEN = 16384
SW_HEADS, SW_KV_HEADS, SW_HEAD_DIM = 8, 2, 64
SW_GROUP = SW_HEADS // SW_KV_HEADS
WINDOW = 128
HG_HEADS, HG_DIM = 4, 128
MEM_LEN, MX_HEADS, MX_HEAD_DIM = 256, 4, 128
SW_Q = SW_HEADS * SW_HEAD_DIM
SW_KV = SW_KV_HEADS * SW_HEAD_DIM
HG_W = HG_HEADS * HG_DIM
MX_W = MX_HEADS * MX_HEAD_DIM
D_FF = 2816
ALPHA = (2.0 * DEPTH) ** 0.25
LN_EPS = 1e-5
RMS_EPS = 1e-6
OFF_SW = 0
OFF_HG = SW_Q + 2 * SW_KV
OFF_MX = OFF_HG + 4 * HG_W
OFF_GATE = OFF_MX + MX_W
IN_WIDTH = OFF_GATE + 3 * D_MODEL

LANES = 128
BF16_SUBLANES = 16
VMEM_LIMIT_BYTES = 60 * 1024 * 1024

NEG = -1e30
HG_CHUNK = 64
HG_SUB = 16
PROMPT_TILE = 256
SAMPLE_BATCH_TILE = 8
F32 = jnp.float32
BF16 = jnp.bfloat16


def _dot(a, b):
    return jnp.dot(a, b, preferred_element_type=F32)


def _dot_nt(a, b):
    return lax.dot_general(a, b, (((1,), (1,)), ((), ())), preferred_element_type=F32)


def _dot_tn(a, b):
    return lax.dot_general(a, b, (((0,), (0,)), ((), ())), preferred_element_type=F32)


def _exact_dot(m_bf16, x):
    hi = x.astype(BF16)
    r1 = x - hi.astype(F32)
    mid = r1.astype(BF16)
    lo = (r1 - mid.astype(F32)).astype(BF16)
    return _dot(m_bf16, hi) + _dot(m_bf16, mid) + _dot(m_bf16, lo)


def _layer_norm(x, w, b):
    mu = jnp.mean(x, axis=-1, keepdims=True)
    xc = x - mu
    var = jnp.mean(xc * xc, axis=-1, keepdims=True)
    return xc * lax.rsqrt(var + LN_EPS) * w + b


def _sigmoid(x):
    return 1.0 / (1.0 + jnp.exp(-x))


def _silu(x):
    return x * _sigmoid(x)


def _lower_bound(hlb):
    m = jnp.max(hlb, axis=0, keepdims=True)
    e = jnp.exp(hlb - m)
    return e[0:1] / jnp.sum(e, axis=0, keepdims=True)


def _lane_lo(shape):
    return lax.broadcasted_iota(jnp.int32, shape, len(shape) - 1) < SW_HEAD_DIM


def _dup_heads(kv):
    rolled = pltpu.roll(kv, SW_HEAD_DIM, axis=1)
    lo = _lane_lo(kv.shape)
    return jnp.where(lo, kv, rolled), jnp.where(lo, rolled, kv)


def _sink_softmax_pv(s, vv, sinks_ref, g, rows):
    ps, dens = [], []
    for hh in range(SW_GROUP):
        sh = s[hh * rows:(hh + 1) * rows]
        sink = sinks_ref[g * SW_GROUP + hh]
        m = jnp.maximum(jnp.max(sh, axis=-1, keepdims=True), sink)
        p = jnp.exp(sh - m)
        dens.append(jnp.sum(p, axis=-1, keepdims=True) + jnp.exp(sink - m))
        ps.append(p)
    o = _dot(jnp.concatenate(ps, axis=0).astype(BF16), vv)
    o = o / jnp.concatenate(dens, axis=0)
    lo = _lane_lo((rows, LANES))
    return (jnp.where(lo, o[0:rows], o[rows:2 * rows]),
            jnp.where(lo, o[2 * rows:3 * rows], o[3 * rows:4 * rows]))


def _stack_group_queries(q, g):
    qa = q[:, 2 * g * LANES:(2 * g + 1) * LANES]
    qb = q[:, (2 * g + 1) * LANES:(2 * g + 2) * LANES]
    lo = _lane_lo(qa.shape)
    z = jnp.zeros_like(qa)
    return jnp.concatenate([jnp.where(lo, qa, z), jnp.where(lo, z, qa),
                            jnp.where(lo, qb, z), jnp.where(lo, z, qb)], axis=0).astype(BF16)


def _hgrn_diag(q, k, v, b, n):
    row = lax.broadcasted_iota(jnp.int32, (n, HG_DIM), 0)
    o = jnp.zeros((n, HG_DIM), F32)
    for s in range(n):
        t0 = (s // 8) * 8
        rr = row[t0:]
        e = jnp.exp(jnp.where(rr >= s, b[t0:] - b[s:s + 1], -jnp.inf))
        col = jnp.sum(q[t0:] * e * k[s:s + 1], axis=-1, keepdims=True)
        upd = col * v[s:s + 1]
        o = o + (upd if t0 == 0 else jnp.concatenate([jnp.zeros((t0, HG_DIM), F32), upd], axis=0))
    return o


def _col_bcast(rowvec):
    return jnp.transpose(jnp.broadcast_to(rowvec, (HG_DIM, HG_DIM)))


def _hgrn_chunk(q, k, v, lf, s0, tri):
    c = HG_CHUNK
    b = _exact_dot(tri, lf)
    s0b = s0.astype(BF16)
    o = _dot((q * jnp.exp(b)).astype(BF16), s0b)
    b_last = b[c - 1:c]
    kd = (k * jnp.exp(b_last - b)).astype(BF16)
    vb = v.astype(BF16)
    s_new = _col_bcast(jnp.exp(b_last)) * s0 + _dot_tn(kd, vb)
    row = lax.broadcasted_iota(jnp.int32, (c, HG_DIM), 0)
    parts = []
    for i in range(c // HG_SUB):
        r0 = i * HG_SUB
        od = _hgrn_diag(q[r0:r0 + HG_SUB], k[r0:r0 + HG_SUB], v[r0:r0 + HG_SUB], b[r0:r0 + HG_SUB], HG_SUB)
        if i > 0:
            g_i = b[r0 - 1:r0]
            qi = (q[r0:r0 + HG_SUB] * jnp.exp(b[r0:r0 + HG_SUB] - g_i)).astype(BF16)
            ki = (k[0:r0] * jnp.exp(g_i - b[0:r0])).astype(BF16)
            a = _dot_nt(qi, ki)
            od = od + _dot(a.astype(BF16), vb[0:r0])
        parts.append(od)
    del row
    return o + jnp.concatenate(parts, axis=0), s_new


def _rms_gate(o, gate, nw):
    outs = []
    for h in range(HG_HEADS):
        oh = o[:, h * HG_DIM:(h + 1) * HG_DIM]
        ms = jnp.mean(oh * oh, axis=-1, keepdims=True)
        outs.append(oh * lax.rsqrt(ms + RMS_EPS) * nw * gate[:, h * HG_DIM:(h + 1) * HG_DIM])
    return jnp.concatenate(outs, axis=-1)


def _mem_attention(mq, mk, mv):
    outs = []
    for h in range(MX_HEADS):
        sl = slice(h * MX_HEAD_DIM, (h + 1) * MX_HEAD_DIM)
        s = _dot_nt(mq[:, sl].astype(BF16), mk[:, sl]) * (MX_HEAD_DIM ** -0.5)
        p = jnp.exp(s - jnp.max(s, axis=-1, keepdims=True))
        den = jnp.sum(p, axis=-1, keepdims=True)
        outs.append(_dot(p.astype(BF16), mv[:, sl]) / den)
    return jnp.concatenate(outs, axis=-1)


def _merge_ffn(xn, xb, o_sw, o_h, o_m, w_in_ref, w_up_ref, w_o_ref, w_f1_ref, w_f2_ref, lnp_ref):
    mix = None
    for j, br in enumerate((o_sw, o_h, o_m)):
        c0 = OFF_GATE + j * D_MODEL
        gate = _sigmoid(_dot(xb, w_in_ref[:, c0:c0 + D_MODEL]))
        term = gate * _dot(br.astype(BF16), w_up_ref[j])
        mix = term if mix is None else mix + term
    h = _layer_norm(ALPHA * xn + _dot(mix.astype(BF16), w_o_ref[...]), lnp_ref[2:3], lnp_ref[3:4])
    hb = h.astype(BF16)
    half = D_FF // 2
    acc = None
    for c in range(2):
        gt = _dot(hb, w_f1_ref[:, c * half:(c + 1) * half])
        up = _dot(hb, w_f1_ref[:, D_FF + c * half:D_FF + (c + 1) * half])
        part = _dot((_silu(gt) * up).astype(BF16), w_f2_ref[c * half:(c + 1) * half, :])
        acc = part if acc is None else acc + part
    return _layer_norm(ALPHA * h + acc, lnp_ref[4:5], lnp_ref[5:6])


def _hgrn_inputs(hp, lb):
    f = lb + (1.0 - lb) * _sigmoid(hp[:, HG_W:2 * HG_W])
    return (_silu(hp[:, 0:HG_W]), 1.0 - f, hp[:, 2 * HG_W:3 * HG_W], jnp.log(f), _silu(hp[:, 3 * HG_W:4 * HG_W]))


def _prompt_bias():
    i = np.arange(WINDOW)[:, None]
    j = np.arange(2 * WINDOW)[None, :]
    dist = i + WINDOW - j
    valid = (dist >= 0) & (dist < WINDOW)
    out = np.zeros((SW_KV_HEADS, SW_GROUP, WINDOW, 2 * WINDOW), np.float32)
    for h in range(SW_HEADS):
        slope = np.float32(2.0) ** np.float32(-8.0 * (h + 1) / SW_HEADS)
        out[h // SW_GROUP, h % SW_GROUP] = np.where(valid, -slope * dist.astype(np.float32), np.float32(NEG))
    return out.reshape(SW_KV_HEADS, SW_GROUP * WINDOW, 2 * WINDOW)


def _prompt_kernel(sinks_ref, x_ref, mk_ref, mv_ref, bias_ref, lnp_ref, hlb_ref, hnw_ref,
                   w_in_ref, w_up_ref, w_o_ref, w_f1_ref, w_f2_ref,
                   y_ref, nk_ref, nv_ref, s_ref,
                   kd_ref, vd_ref, hq_ref, hk_ref, hv_ref, hl_ref, oh_ref, *, tm, nt):
    t = pl.program_id(1)

    @pl.when(t == 0)
    def _():
        kd_ref[:, 0:WINDOW, :] = jnp.zeros((SW_KV_HEADS, WINDOW, LANES), BF16)
        vd_ref[:, 0:WINDOW, :] = jnp.zeros((SW_KV_HEADS, WINDOW, LANES), BF16)
        s_ref[...] = jnp.zeros_like(s_ref)

    xn = _layer_norm(x_ref[0], lnp_ref[0:1], lnp_ref[1:2])
    xb = xn.astype(BF16)

    qkv = _dot(xb, w_in_ref[:, OFF_SW:OFF_HG])
    q = qkv[:, 0:SW_Q] * (SW_HEAD_DIM ** -0.5)
    k = qkv[:, SW_Q:SW_Q + SW_KV]
    v = qkv[:, SW_Q + SW_KV:SW_Q + 2 * SW_KV]

    @pl.when(t == nt - 1)
    def _():
        nk_ref[0] = k[tm - WINDOW:tm]
        nv_ref[0] = v[tm - WINDOW:tm]

    for g, dup in enumerate(_dup_heads(k)):
        kd_ref[g, WINDOW:WINDOW + tm, :] = dup.astype(BF16)
    for g, dup in enumerate(_dup_heads(v)):
        vd_ref[g, WINDOW:WINDOW + tm, :] = dup.astype(BF16)

    col = lax.broadcasted_iota(jnp.int32, (1, 2 * WINDOW), 1)
    first = jnp.where(jnp.logical_and(t == 0, col < WINDOW), NEG, 0.0).astype(F32)
    o_rows = []
    for j in range(tm // WINDOW):
        qj = q[j * WINDOW:(j + 1) * WINDOW]
        blocks = []
        for g in range(SW_KV_HEADS):
            kk = kd_ref[g, j * WINDOW:(j + 2) * WINDOW, :]
            vv = vd_ref[g, j * WINDOW:(j + 2) * WINDOW, :]
            s = _dot_nt(_stack_group_queries(qj, g), kk) + bias_ref[g]
            if j == 0:
                s = s + first
            blocks.extend(_sink_softmax_pv(s, vv, sinks_ref, g, WINDOW))
        o_rows.append(jnp.concatenate(blocks, axis=-1))
    o_sw = jnp.concatenate(o_rows, axis=0)
    kd_ref[:, 0:WINDOW, :] = kd_ref[:, tm:tm + WINDOW, :]
    vd_ref[:, 0:WINDOW, :] = vd_ref[:, tm:tm + WINDOW, :]

    hq, hk, hv, hl, hgate = _hgrn_inputs(_dot(xb, w_in_ref[:, OFF_HG:OFF_MX]), _lower_bound(hlb_ref[...]))
    for h in range(HG_HEADS):
        sl = slice(h * HG_DIM, (h + 1) * HG_DIM)
        hq_ref[h] = hq[:, sl]
        hk_ref[h] = hk[:, sl]
        hv_ref[h] = hv[:, sl]
        hl_ref[h] = hl[:, sl]
    ri = lax.broadcasted_iota(jnp.int32, (HG_CHUNK, HG_CHUNK), 0)
    ci = lax.broadcasted_iota(jnp.int32, (HG_CHUNK, HG_CHUNK), 1)
    tri = jnp.where(ri >= ci, 1.0, 0.0).astype(BF16)

    def chunk_body(c, carry):
        rows = pl.ds(pl.multiple_of(c * HG_CHUNK, HG_CHUNK), HG_CHUNK)
        for h in range(HG_HEADS):
            o, s_new = _hgrn_chunk(hq_ref[h, rows, :], hk_ref[h, rows, :], hv_ref[h, rows, :], hl_ref[h, rows, :],
                                   s_ref[0, h], tri)
            s_ref[0, h] = s_new
            oh_ref[rows, h * HG_DIM:(h + 1) * HG_DIM] = o
        return carry

    lax.fori_loop(0, tm // HG_CHUNK, chunk_body, 0)
    o_h = _rms_gate(oh_ref[...], hgate, hnw_ref[...])

    o_m = _mem_attention(_dot(xb, w_in_ref[:, OFF_MX:OFF_GATE]), mk_ref[0], mv_ref[0])

    y_ref[0] = _merge_ffn(xn, xb, o_sw, o_h, o_m, w_in_ref, w_up_ref, w_o_ref, w_f1_ref, w_f2_ref, lnp_ref)


def _full(shape):
    nd = len(shape)
    return pl.BlockSpec(shape, lambda *_: (0,) * nd, pipeline_mode=pl.Buffered(1))


def _prompt_layer(x, mk_b, mv_b, sinks, lnp, hlb, hnw, w_in, w_up, w_o, w_f1, w_f2, *, tm):
    bsz, seq, _ = x.shape
    nt = seq // tm
    assert seq % tm == 0 and tm % WINDOW == 0 and tm % HG_CHUNK == 0
    bias = jnp.asarray(_prompt_bias())
    kern = functools.partial(_prompt_kernel, tm=tm, nt=nt)
    return pl.pallas_call(
        kern,
        grid=(bsz, nt),
        in_specs=[
            pl.BlockSpec(memory_space=pltpu.SMEM),
            pl.BlockSpec((1, tm, D_MODEL), lambda b, t: (b, t, 0)),
            pl.BlockSpec((1, MEM_LEN, MX_W), lambda b, t: (b, 0, 0)),
            pl.BlockSpec((1, MEM_LEN, MX_W), lambda b, t: (b, 0, 0)),
            _full(bias.shape), _full(lnp.shape), _full(hlb.shape), _full(hnw.shape),
            _full(w_in.shape), _full(w_up.shape), _full(w_o.shape), _full(w_f1.shape), _full(w_f2.shape),
        ],
        out_specs=[
            pl.BlockSpec((1, tm, D_MODEL), lambda b, t: (b, t, 0)),
            pl.BlockSpec((1, WINDOW, SW_KV), lambda b, t: (b, 0, 0)),
            pl.BlockSpec((1, WINDOW, SW_KV), lambda b, t: (b, 0, 0)),
            pl.BlockSpec((1, HG_HEADS, HG_DIM, HG_DIM), lambda b, t: (b, 0, 0, 0)),
        ],
        out_shape=[
            jax.ShapeDtypeStruct((bsz, seq, D_MODEL), F32),
            jax.ShapeDtypeStruct((bsz, WINDOW, SW_KV), F32),
            jax.ShapeDtypeStruct((bsz, WINDOW, SW_KV), F32),
            jax.ShapeDtypeStruct((bsz, HG_HEADS, HG_DIM, HG_DIM), F32),
        ],
        scratch_shapes=[
            pltpu.VMEM((SW_KV_HEADS, WINDOW + tm, LANES), BF16),
            pltpu.VMEM((SW_KV_HEADS, WINDOW + tm, LANES), BF16),
            pltpu.VMEM((HG_HEADS, tm, HG_DIM), F32),
            pltpu.VMEM((HG_HEADS, tm, HG_DIM), F32),
            pltpu.VMEM((HG_HEADS, tm, HG_DIM), F32),
            pltpu.VMEM((HG_HEADS, tm, HG_DIM), F32),
            pltpu.VMEM((tm, HG_W), F32),
        ],
        compiler_params=pltpu.CompilerParams(
            dimension_semantics=("arbitrary", "arbitrary"), vmem_limit_bytes=VMEM_LIMIT_BYTES),
        name="prompt_layer",
    )(sinks, x, mk_b, mv_b, bias, lnp, hlb, hnw, w_in, w_up, w_o, w_f1, w_f2)


def _mem_kv_kernel(m_ref, w_ref, mk_ref, mv_ref, mkb_ref, mvb_ref):
    kv = _dot(m_ref[0].astype(BF16), w_ref[...])
    mk_ref[0] = kv[:, 0:MX_W]
    mv_ref[0] = kv[:, MX_W:2 * MX_W]
    mkb_ref[0] = kv[:, 0:MX_W].astype(BF16)
    mvb_ref[0] = kv[:, MX_W:2 * MX_W].astype(BF16)


def _mem_kv(mem, w_kv):
    bsz = mem.shape[0]
    blk = pl.BlockSpec((1, MEM_LEN, MX_W), lambda b: (b, 0, 0))
    return pl.pallas_call(
        _mem_kv_kernel,
        grid=(bsz,),
        in_specs=[pl.BlockSpec((1, MEM_LEN, D_MODEL), lambda b: (b, 0, 0)), _full(w_kv.shape)],
        out_specs=[blk, blk, blk, blk],
        out_shape=[jax.ShapeDtypeStruct((bsz, MEM_LEN, MX_W), F32)] * 2
        + [jax.ShapeDtypeStruct((bsz, MEM_LEN, MX_W), BF16)] * 2,
        compiler_params=pltpu.CompilerParams(dimension_semantics=("arbitrary",)),
        name="mem_kv",
    )(mem, w_kv)


def _prep_params(p):
    assert p['w_in'].shape[0] == DEPTH == 1
    zeros = jnp.zeros((D_MODEL,), F32)
    return dict(
        lnp=jnp.stack([p['ln0_w'], p['ln0_b'], p['ln1_w'][0], p['ln1_b'][0], p['ln2_w'][0], p['ln2_b'][0],
                       zeros, zeros]).astype(F32),
        sinks=p['sw_sinks'][0].astype(F32),
        hlb=p['hg_lower_bound'].astype(F32),
        hnw=p['hg_norm_w'].astype(F32),
        w_in=p['w_in'][0].astype(BF16),
        w_up=jnp.stack([p['w_up_sw'][0], p['w_up_hg'][0], p['w_up_mx'][0]]).astype(BF16),
        w_o=p['w_o'][0].astype(BF16),
        w_f1=p['w_ffn_in'][0].astype(BF16),
        w_f2=p['w_ffn_out'][0].astype(BF16),
        w_kv=p['w_mem_kv'][0].astype(BF16),
    )


def _prompt_group(x, mem, p, *, tm, pp=None):
    pp = _prep_params(p) if pp is None else pp
    mk, mv, mk_b, mv_b = _mem_kv(mem, pp['w_kv'])
    y, nk, nv, s = _prompt_layer(x, mk_b, mv_b, pp['sinks'], pp['lnp'], pp['hlb'], pp['hnw'],
                                 pp['w_in'], pp['w_up'], pp['w_o'], pp['w_f1'], pp['w_f2'], tm=tm)
    return y, nk, nv, s, mk, mv


def _sample_proj_kernel(x_ref, lnp_ref, w_in_ref, xn_ref, proj_ref):
    xn = _layer_norm(x_ref[...], lnp_ref[0:1], lnp_ref[1:2])
    xn_ref[...] = xn
    proj_ref[...] = _dot(xn.astype(BF16), w_in_ref[:, 0:OFF_GATE])


def _sample_proj(x2, lnp, w_in, *, tr):
    rows = x2.shape[0]
    return pl.pallas_call(
        _sample_proj_kernel,
        grid=(rows // tr,),
        in_specs=[pl.BlockSpec((tr, D_MODEL), lambda i: (i, 0)), _full(lnp.shape), _full(w_in.shape)],
        out_specs=[pl.BlockSpec((tr, D_MODEL), lambda i: (i, 0)), pl.BlockSpec((tr, OFF_GATE), lambda i: (i, 0))],
        out_shape=[jax.ShapeDtypeStruct((rows, D_MODEL), F32), jax.ShapeDtypeStruct((rows, OFF_GATE), F32)],
        compiler_params=pltpu.CompilerParams(
            dimension_semantics=("arbitrary",), vmem_limit_bytes=VMEM_LIMIT_BYTES),
        name="sample_proj",
    )(x2, lnp, w_in)


def _sample_bias(t_new, n_keys):
    i = np.arange(t_new)[:, None]
    j = np.arange(n_keys)[None, :]
    kpos = np.where(j < WINDOW, PAST_LEN - WINDOW + j, PAST_LEN + j - WINDOW)
    dist = PAST_LEN + i - kpos
    valid = (dist >= 0) & (dist < WINDOW) & (j < WINDOW + t_new)
    out = np.zeros((SW_KV_HEADS, SW_GROUP, t_new, n_keys), np.float32)
    for h in range(SW_HEADS):
        slope = np.float32(2.0) ** np.float32(-8.0 * (h + 1) / SW_HEADS)
        out[h // SW_GROUP, h % SW_GROUP] = np.where(valid, -slope * dist.astype(np.float32), np.float32(NEG))
    return out.reshape(SW_KV_HEADS, SW_GROUP * t_new, n_keys)


def _sample_mix_kernel(sinks_ref, proj_ref, kbuf_ref, vbuf_ref, st_ref, mk_ref, mv_ref, bias_ref, hlb_ref, hnw_ref,
                       osw_ref, oh_ref, om_ref, nk_ref, nv_ref, ns_ref,
                       kk_ref, vv_ref, hq_ref, hk_ref, hb_ref, hg_ref, *, bt, t_new):
    nkeys = 2 * WINDOW
    r = bt * t_new
    hq, hk, _, hl, hgate = _hgrn_inputs(proj_ref[:, OFF_HG:OFF_MX], _lower_bound(hlb_ref[...]))
    ri = lax.broadcasted_iota(jnp.int32, (r, r), 0)
    ci = lax.broadcasted_iota(jnp.int32, (r, r), 1)
    same = (ri // t_new) == (ci // t_new)
    tri = jnp.where(jnp.logical_and(same, ri >= ci), 1.0, 0.0).astype(BF16)
    hb_ref[...] = _exact_dot(tri, hl)
    hq_ref[...] = hq
    hk_ref[...] = hk
    hg_ref[...] = hgate
    kk_ref[WINDOW + t_new:nkeys, :] = jnp.zeros((nkeys - WINDOW - t_new, LANES), F32)
    vv_ref[WINDOW + t_new:nkeys, :] = jnp.zeros((nkeys - WINDOW - t_new, LANES), F32)
    pad_rows = BF16_SUBLANES - t_new

    def pad(a):
        return jnp.concatenate([a, jnp.zeros((pad_rows, a.shape[1]), F32)], axis=0)

    def batch_body(i, carry):
        rows = pl.ds(pl.multiple_of(i * t_new, t_new), t_new)
        kk_ref[0:WINDOW, :] = kbuf_ref[i]
        vv_ref[0:WINDOW, :] = vbuf_ref[i]
        kk_ref[WINDOW:WINDOW + t_new, :] = proj_ref[rows, SW_Q:SW_Q + SW_KV]
        vv_ref[WINDOW:WINDOW + t_new, :] = proj_ref[rows, SW_Q + SW_KV:SW_Q + 2 * SW_KV]
        nk_ref[i] = kk_ref[t_new:t_new + WINDOW, :]
        nv_ref[i] = vv_ref[t_new:t_new + WINDOW, :]
        kds = _dup_heads(kk_ref[...])
        vds = _dup_heads(vv_ref[...])
        q = proj_ref[rows, 0:SW_Q] * (SW_HEAD_DIM ** -0.5)
        blocks = []
        for g in range(SW_KV_HEADS):
            s = _dot_nt(_stack_group_queries(q, g), kds[g].astype(BF16)) + bias_ref[g]
            blocks.extend(_sink_softmax_pv(s, vds[g].astype(BF16), sinks_ref, g, t_new))
        osw_ref[rows, :] = jnp.concatenate(blocks, axis=-1)
        mq = pad(proj_ref[rows, OFF_MX:OFF_GATE])
        om_ref[rows, :] = _mem_attention(mq, mk_ref[i].astype(BF16), mv_ref[i].astype(BF16))[0:t_new]
        for h in range(HG_HEADS):
            sl = slice(h * HG_DIM, (h + 1) * HG_DIM)
            qh = hq_ref[rows, sl]
            kh = hk_ref[rows, sl]
            vh = proj_ref[rows, OFF_HG + 2 * HG_W + h * HG_DIM:OFF_HG + 2 * HG_W + (h + 1) * HG_DIM]
            b = hb_ref[rows, sl]
            s0 = st_ref[i, h]
            o = _dot(pad(qh * jnp.exp(b)).astype(BF16), s0.astype(BF16))[0:t_new]
            b_last = b[t_new - 1:t_new]
            kd = pad(kh * jnp.exp(b_last - b)).astype(BF16)
            ns_ref[i, h] = _col_bcast(jnp.exp(b_last)) * s0 + _dot_tn(kd, pad(vh).astype(BF16))
            oh_ref[rows, sl] = o + _hgrn_diag(qh, kh, vh, b, t_new)
        return carry

    lax.fori_loop(0, bt, batch_body, 0)
    oh_ref[...] = _rms_gate(oh_ref[...], hg_ref[...], hnw_ref[...])


def _sample_mix(proj, kbuf, vbuf, st, mk, mv, sinks, hlb, hnw, *, bt, t_new):
    nb = kbuf.shape[0]
    r = bt * t_new
    assert nb % bt == 0 and t_new == 8 and r % BF16_SUBLANES == 0
    bias = jnp.asarray(_sample_bias(t_new, 2 * WINDOW))
    kern = functools.partial(_sample_mix_kernel, bt=bt, t_new=t_new)
    row_blk = pl.BlockSpec((r, HG_W), lambda i: (i, 0))
    kv_blk = pl.BlockSpec((bt, WINDOW, SW_KV), lambda i: (i, 0, 0))
    st_blk = pl.BlockSpec((bt, HG_HEADS, HG_DIM, HG_DIM), lambda i: (i, 0, 0, 0))
    mem_blk = pl.BlockSpec((bt, MEM_LEN, MX_W), lambda i: (i, 0, 0))
    return pl.pallas_call(
        kern,
        grid=(nb // bt,),
        in_specs=[pl.BlockSpec(memory_space=pltpu.SMEM),
                  pl.BlockSpec((r, OFF_GATE), lambda i: (i, 0)),
                  kv_blk, kv_blk, st_blk, mem_blk, mem_blk,
                  _full(bias.shape), _full(hlb.shape), _full(hnw.shape)],
        out_specs=[row_blk, row_blk, row_blk, kv_blk, kv_blk, st_blk],
        out_shape=[jax.ShapeDtypeStruct((nb * t_new, HG_W), F32)] * 3
        + [jax.ShapeDtypeStruct((nb, WINDOW, SW_KV), F32)] * 2
        + [jax.ShapeDtypeStruct((nb, HG_HEADS, HG_DIM, HG_DIM), F32)],
        scratch_shapes=[pltpu.VMEM((2 * WINDOW, LANES), F32), pltpu.VMEM((2 * WINDOW, LANES), F32),
                        pltpu.VMEM((r, HG_W), F32), pltpu.VMEM((r, HG_W), F32),
                        pltpu.VMEM((r, HG_W), F32), pltpu.VMEM((r, HG_W), F32)],
        compiler_params=pltpu.CompilerParams(
            dimension_semantics=("arbitrary",), vmem_limit_bytes=VMEM_LIMIT_BYTES),
        name="sample_mix",
    )(sinks, proj, kbuf, vbuf, st, mk, mv, bias, hlb, hnw)


def _sample_out_kernel(xn_ref, osw_ref, oh_ref, om_ref, lnp_ref, w_in_ref, w_up_ref, w_o_ref, w_f1_ref, w_f2_ref,
                       y_ref):
    xn = xn_ref[...]
    y_ref[...] = _merge_ffn(xn, xn.astype(BF16), osw_ref[...], oh_ref[...], om_ref[...],
                            w_in_ref, w_up_ref, w_o_ref, w_f1_ref, w_f2_ref, lnp_ref)


def _sample_out(xn, osw, oh, om, lnp, w_in, w_up, w_o, w_f1, w_f2, *, tr):
    rows = xn.shape[0]
    wide = pl.BlockSpec((tr, D_MODEL), lambda i: (i, 0))
    half = pl.BlockSpec((tr, HG_W), lambda i: (i, 0))
    return pl.pallas_call(
        _sample_out_kernel,
        grid=(rows // tr,),
        in_specs=[wide, half, half, half, _full(lnp.shape), _full(w_in.shape), _full(w_up.shape),
                  _full(w_o.shape), _full(w_f1.shape), _full(w_f2.shape)],
        out_specs=wide,
        out_shape=jax.ShapeDtypeStruct((rows, D_MODEL), F32),
        compiler_params=pltpu.CompilerParams(
            dimension_semantics=("arbitrary",), vmem_limit_bytes=VMEM_LIMIT_BYTES),
        name="sample_out",
    )(xn, osw, oh, om, lnp, w_in, w_up, w_o, w_f1, w_f2)


def _sample_group(x, p, *, bt, pp=None):
    pp = _prep_params(p) if pp is None else pp
    nb, t_new, _ = x.shape
    rows = nb * t_new
    tr = min(rows, 256)
    xn, proj = _sample_proj(x.reshape(rows, D_MODEL), pp['lnp'], pp['w_in'], tr=tr)
    osw, oh, om, nk, nv, ns = _sample_mix(
        proj,
        p['cache_k_win'][0].reshape(nb, WINDOW, SW_KV), p['cache_v_win'][0].reshape(nb, WINDOW, SW_KV),
        p['state_hgrn'][0],
        p['cache_mem_k'][0].reshape(nb, MEM_LEN, MX_W), p['cache_mem_v'][0].reshape(nb, MEM_LEN, MX_W),
        pp['sinks'], pp['hlb'], pp['hnw'], bt=bt, t_new=t_new)
    y = _sample_out(xn, osw, oh, om, pp['lnp'], pp['w_in'], pp['w_up'], pp['w_o'], pp['w_f1'], pp['w_f2'], tr=tr)
    return y.reshape(nb, t_new, D_MODEL), nk, nv, ns


def kernel(x_prompt, x_sample, cache_k_win, cache_v_win, state_hgrn, cache_mem_k, cache_mem_v, mem_prompt,
           ln0_w, ln0_b, w_in, w_up_sw, w_up_hg, w_up_mx, sw_sinks, hg_lower_bound, hg_norm_w, w_mem_kv,
           w_o, ln1_w, ln1_b, w_ffn_in, w_ffn_out, ln2_w, ln2_b):
    p = dict(ln0_w=ln0_w, ln0_b=ln0_b, w_in=w_in, w_up_sw=w_up_sw, w_up_hg=w_up_hg, w_up_mx=w_up_mx,
             sw_sinks=sw_sinks, hg_lower_bound=hg_lower_bound, hg_norm_w=hg_norm_w, w_mem_kv=w_mem_kv,
             w_o=w_o, ln1_w=ln1_w, ln1_b=ln1_b, w_ffn_in=w_ffn_in, w_ffn_out=w_ffn_out, ln2_w=ln2_w, ln2_b=ln2_b)
    p.update(cache_k_win=cache_k_win, cache_v_win=cache_v_win, state_hgrn=state_hgrn,
             cache_mem_k=cache_mem_k, cache_mem_v=cache_mem_v)
    pp = _prep_params(p)
    bp, bs = x_prompt.shape[0], x_sample.shape[0]
    y, nk, nv, s, mk, mv = _prompt_group(x_prompt, mem_prompt, p, tm=PROMPT_TILE, pp=pp)
    ys, nks, nvs, ss = _sample_group(x_sample, p, bt=SAMPLE_BATCH_TILE, pp=pp)
    win = (WINDOW, SW_KV_HEADS, SW_HEAD_DIM)
    mem = (MEM_LEN, MX_HEADS, MX_HEAD_DIM)
    return (y, ys,
            nk.reshape(1, bp, *win), nv.reshape(1, bp, *win), s[None],
            mk.reshape(1, bp, *mem), mv.reshape(1, bp, *mem),
            nks.reshape(1, bs, *win), nvs.reshape(1, bs, *win), ss[None])
```

```python
import functools

import numpy as np
import jax
import jax.numpy as jnp
from jax import lax
from jax.experimental import pallas as pl
from jax.experimental.pallas import tpu as pltpu

D_MODEL = 1024
DEPTH = 1
PAST_LEN = 16384
SW_HEADS, SW_KV_HEADS, SW_HEAD_DIM = 8, 2, 64
SW_GROUP = SW_HEADS // SW_KV_HEADS
WINDOW = 128
HG_HEADS, HG_DIM = 4, 128
MEM_LEN, MX_HEADS, MX_HEAD_DIM = 256, 4, 128
SW_Q = SW_HEADS * SW_HEAD_DIM
SW_KV = SW_KV_HEADS * SW_HEAD_DIM
HG_W = HG_HEADS * HG_DIM
MX_W = MX_HEADS * MX_HEAD_DIM
D_FF = 2816
ALPHA = (2.0 * DEPTH) ** 0.25
LN_EPS = 1e-5
RMS_EPS = 1e-6
OFF_SW = 0
OFF_HG = SW_Q + 2 * SW_KV
OFF_MX = OFF_HG + 4 * HG_W
OFF_GATE = OFF_MX + MX_W
IN_WIDTH = OFF_GATE + 3 * D_MODEL

LANES = 128
BF16_SUBLANES = 16
VMEM_LIMIT_BYTES = 60 * 1024 * 1024

NEG = -1e30
HG_CHUNK = 64
HG_SUB = 16
HG_SAFE_RANGE = 80.0
PROMPT_TILE = 256
SAMPLE_BATCH_TILE = 8
F32 = jnp.float32
BF16 = jnp.bfloat16


def _dot(a, b):
    return jnp.dot(a, b, preferred_element_type=F32)


def _dot_nt(a, b):
    return lax.dot_general(a, b, (((1,), (1,)), ((), ())), preferred_element_type=F32)


def _dot_tn(a, b):
    return lax.dot_general(a, b, (((0,), (0,)), ((), ())), preferred_element_type=F32)


def _exact_dot(m_bf16, x):
    hi = x.astype(BF16)
    r1 = x - hi.astype(F32)
    mid = r1.astype(BF16)
    lo = (r1 - mid.astype(F32)).astype(BF16)
    return _dot(m_bf16, hi) + _dot(m_bf16, mid) + _dot(m_bf16, lo)


def _layer_norm(x, w, b):
    mu = jnp.mean(x, axis=-1, keepdims=True)
    xc = x - mu
    var = jnp.mean(xc * xc, axis=-1, keepdims=True)
    return xc * lax.rsqrt(var + LN_EPS) * w + b


def _sigmoid(x):
    return 1.0 / (1.0 + jnp.exp(-x))


def _silu(x):
    return x * _sigmoid(x)


def _lower_bound(hlb):
    m = jnp.max(hlb, axis=0, keepdims=True)
    e = jnp.exp(hlb - m)
    return e[0:1] / jnp.sum(e, axis=0, keepdims=True)


def _lane_lo(shape):
    return lax.broadcasted_iota(jnp.int32, shape, len(shape) - 1) < SW_HEAD_DIM


def _dup_heads(kv):
    rolled = pltpu.roll(kv, SW_HEAD_DIM, axis=1)
    lo = _lane_lo(kv.shape)
    return jnp.where(lo, kv, rolled), jnp.where(lo, rolled, kv)


def _sink_softmax_pv(s, vv, sinks_ref, g, rows):
    ps, dens = [], []
    for hh in range(SW_GROUP):
        sh = s[hh * rows:(hh + 1) * rows]
        sink = sinks_ref[g * SW_GROUP + hh]
        m = jnp.maximum(jnp.max(sh, axis=-1, keepdims=True), sink)
        p = jnp.exp(sh - m)
        dens.append(jnp.sum(p, axis=-1, keepdims=True) + jnp.exp(sink - m))
        ps.append(p)
    o = _dot(jnp.concatenate(ps, axis=0).astype(BF16), vv)
    o = o / jnp.concatenate(dens, axis=0)
    lo = _lane_lo((rows, LANES))
    return (jnp.where(lo, o[0:rows], o[rows:2 * rows]),
            jnp.where(lo, o[2 * rows:3 * rows], o[3 * rows:4 * rows]))


def _stack_group_queries(q, g):
    qa = q[:, 2 * g * LANES:(2 * g + 1) * LANES]
    qb = q[:, (2 * g + 1) * LANES:(2 * g + 2) * LANES]
    lo = _lane_lo(qa.shape)
    z = jnp.zeros_like(qa)
    return jnp.concatenate([jnp.where(lo, qa, z), jnp.where(lo, z, qa),
                            jnp.where(lo, qb, z), jnp.where(lo, z, qb)], axis=0).astype(BF16)


def _hgrn_diag(q, k, v, b, n):
    row = lax.broadcasted_iota(jnp.int32, (n, HG_DIM), 0)
    o = jnp.zeros((n, HG_DIM), F32)
    for s in range(n):
        t0 = (s // 8) * 8
        rr = row[t0:]
        e = jnp.exp(jnp.where(rr >= s, b[t0:] - b[s:s + 1], -jnp.inf))
        col = jnp.sum(q[t0:] * e * k[s:s + 1], axis=-1, keepdims=True)
        upd = col * v[s:s + 1]
        o = o + (upd if t0 == 0 else jnp.concatenate([jnp.zeros((t0, HG_DIM), F32), upd], axis=0))
    return o


def _col_bcast(rowvec):
    return jnp.transpose(jnp.broadcast_to(rowvec, (HG_DIM, HG_DIM)))


def _hgrn_chunk_fast(qe, ke, vb, kd, dec, st, causal):
    a = jnp.where(causal, _dot_nt(qe, ke), 0.0).astype(BF16)
    o = _dot_nt(qe, st.astype(BF16)) + _dot(a, vb)
    return o, st * dec + _dot_tn(vb, kd)


def _hgrn_chunk_exact(q, k, v, b, st):
    c = HG_CHUNK
    stb = st.astype(BF16)
    o = _dot_nt((q * jnp.exp(b)).astype(BF16), stb)
    b_last = b[c - 1:c]
    kd = (k * jnp.exp(b_last - b)).astype(BF16)
    vb = v.astype(BF16)
    s_new = st * jnp.exp(b_last) + _dot_tn(vb, kd)
    parts = []
    for i in range(c // HG_SUB):
        r0 = i * HG_SUB
        od = _hgrn_diag(q[r0:r0 + HG_SUB], k[r0:r0 + HG_SUB], v[r0:r0 + HG_SUB], b[r0:r0 + HG_SUB], HG_SUB)
        if i > 0:
            g_i = b[r0 - 1:r0]
            qi = (q[r0:r0 + HG_SUB] * jnp.exp(b[r0:r0 + HG_SUB] - g_i)).astype(BF16)
            ki = (k[0:r0] * jnp.exp(g_i - b[0:r0])).astype(BF16)
            a = _dot_nt(qi, ki)
            od = od + _dot(a.astype(BF16), vb[0:r0])
        parts.append(od)
    return o + jnp.concatenate(parts, axis=0), s_new


def _rms_gate(o, gate, nw):
    outs = []
    for h in range(HG_HEADS):
        oh = o[:, h * HG_DIM:(h + 1) * HG_DIM]
        ms = jnp.mean(oh * oh, axis=-1, keepdims=True)
        outs.append(oh * lax.rsqrt(ms + RMS_EPS) * nw * gate[:, h * HG_DIM:(h + 1) * HG_DIM])
    return jnp.concatenate(outs, axis=-1)


def _mem_attention(mq, mk, mv):
    outs = []
    for h in range(MX_HEADS):
        sl = slice(h * MX_HEAD_DIM, (h + 1) * MX_HEAD_DIM)
        s = _dot_nt(mq[:, sl].astype(BF16), mk[:, sl]) * (MX_HEAD_DIM ** -0.5)
        p = jnp.exp(s - jnp.max(s, axis=-1, keepdims=True))
        den = jnp.sum(p, axis=-1, keepdims=True)
        outs.append(_dot(p.astype(BF16), mv[:, sl]) / den)
    return jnp.concatenate(outs, axis=-1)


def _branch_gate(xb, j, w_in_ref):
    c0 = OFF_GATE + j * D_MODEL
    return _sigmoid(_dot(xb, w_in_ref[:, c0:c0 + D_MODEL]))


def _out_ffn(xn, mix, w_o_ref, w_f1_ref, w_f2_ref, lnp_ref):
    h = _layer_norm(ALPHA * xn + _dot(mix.astype(BF16), w_o_ref[...]), lnp_ref[2:3], lnp_ref[3:4])
    hb = h.astype(BF16)
    half = D_FF // 2
    acc = None
    for c in range(2):
        gt = _dot(hb, w_f1_ref[:, c * half:(c + 1) * half])
        up = _dot(hb, w_f1_ref[:, D_FF + c * half:D_FF + (c + 1) * half])
        part = _dot((_silu(gt) * up).astype(BF16), w_f2_ref[c * half:(c + 1) * half, :])
        acc = part if acc is None else acc + part
    return _layer_norm(ALPHA * h + acc, lnp_ref[4:5], lnp_ref[5:6])


def _hgrn_inputs(hp, lb):
    f = lb + (1.0 - lb) * _sigmoid(hp[:, HG_W:2 * HG_W])
    return (_silu(hp[:, 0:HG_W]), 1.0 - f, hp[:, 2 * HG_W:3 * HG_W], jnp.log(f), _silu(hp[:, 3 * HG_W:4 * HG_W]))


def _prompt_bias():
    i = np.arange(WINDOW)[:, None]
    j = np.arange(2 * WINDOW)[None, :]
    dist = i + WINDOW - j
    valid = (dist >= 0) & (dist < WINDOW)
    out = np.zeros((SW_KV_HEADS, SW_GROUP, WINDOW, 2 * WINDOW), np.float32)
    for h in range(SW_HEADS):
        slope = np.float32(2.0) ** np.float32(-8.0 * (h + 1) / SW_HEADS)
        out[h // SW_GROUP, h % SW_GROUP] = np.where(valid, -slope * dist.astype(np.float32), np.float32(NEG))
    return out.reshape(SW_KV_HEADS, SW_GROUP * WINDOW, 2 * WINDOW)


def _prompt_kernel(sinks_ref, x_ref, mk_ref, mv_ref, bias_ref, tri_ref, lnp_ref, hlb_ref, hnw_ref,
                   w_in_ref, w_up_ref, w_o_ref, w_f1_ref, w_f2_ref,
                   y_ref, nk_ref, nv_ref, s_ref,
                   kd_ref, vd_ref, st_ref, sto_ref, hq_ref, hk_ref, hv_ref, hb_ref, oh_ref, *, tm, nt):
    t = pl.program_id(1)

    @pl.when(t == 0)
    def _():
        kd_ref[:, 0:WINDOW, :] = jnp.zeros((SW_KV_HEADS, WINDOW, LANES), BF16)
        vd_ref[:, 0:WINDOW, :] = jnp.zeros((SW_KV_HEADS, WINDOW, LANES), BF16)
        st_ref[...] = jnp.zeros_like(st_ref)

    xn = _layer_norm(x_ref[0], lnp_ref[0:1], lnp_ref[1:2])
    xb = xn.astype(BF16)

    qkv = _dot(xb, w_in_ref[:, OFF_SW:OFF_HG])
    q = qkv[:, 0:SW_Q] * (SW_HEAD_DIM ** -0.5)
    k = qkv[:, SW_Q:SW_Q + SW_KV]
    v = qkv[:, SW_Q + SW_KV:SW_Q + 2 * SW_KV]

    nk_ref[0] = k[tm - WINDOW:tm]
    nv_ref[0] = v[tm - WINDOW:tm]

    for g, dup in enumerate(_dup_heads(k)):
        kd_ref[g, WINDOW:WINDOW + tm, :] = dup.astype(BF16)
    for g, dup in enumerate(_dup_heads(v)):
        vd_ref[g, WINDOW:WINDOW + tm, :] = dup.astype(BF16)

    col = lax.broadcasted_iota(jnp.int32, (1, 2 * WINDOW), 1)
    first = jnp.where(jnp.logical_and(t == 0, col < WINDOW), NEG, 0.0).astype(F32)
    o_rows = []
    for j in range(tm // WINDOW):
        qj = q[j * WINDOW:(j + 1) * WINDOW]
        blocks = []
        for g in range(SW_KV_HEADS):
            kk = kd_ref[g, j * WINDOW:(j + 2) * WINDOW, :]
            vv = vd_ref[g, j * WINDOW:(j + 2) * WINDOW, :]
            s = _dot_nt(_stack_group_queries(qj, g), kk) + bias_ref[g]
            if j == 0:
                s = s + first
            blocks.extend(_sink_softmax_pv(s, vv, sinks_ref, g, WINDOW))
        o_rows.append(jnp.concatenate(blocks, axis=-1))
    o_sw = jnp.concatenate(o_rows, axis=0)
    kd_ref[:, 0:WINDOW, :] = kd_ref[:, tm:tm + WINDOW, :]
    vd_ref[:, 0:WINDOW, :] = vd_ref[:, tm:tm + WINDOW, :]

    hq, hk, hv, hl, hgate = _hgrn_inputs(_dot(xb, w_in_ref[:, OFF_HG:OFF_MX]), _lower_bound(hlb_ref[...]))
    hb = _exact_dot(tri_ref[...], hl)
    sto_ref[...] = st_ref[...]
    qe = (hq * jnp.exp(hb)).astype(BF16)
    ke = (hk * jnp.exp(-hb)).astype(BF16)
    vb = hv.astype(BF16)
    ri = lax.broadcasted_iota(jnp.int32, (HG_CHUNK, HG_CHUNK), 0)
    ci = lax.broadcasted_iota(jnp.int32, (HG_CHUNK, HG_CHUNK), 1)
    causal = ri >= ci
    st = [st_ref[h] for h in range(HG_HEADS)]
    for c in range(tm // HG_CHUNK):
        rows = slice(c * HG_CHUNK, (c + 1) * HG_CHUNK)
        b_last = hb[(c + 1) * HG_CHUNK - 1:(c + 1) * HG_CHUNK]
        kd = (hk[rows] * jnp.exp(b_last - hb[rows])).astype(BF16)
        dec = jnp.exp(b_last)
        for h in range(HG_HEADS):
            sl = slice(h * HG_DIM, (h + 1) * HG_DIM)
            o, st[h] = _hgrn_chunk_fast(qe[rows, sl], ke[rows, sl], vb[rows, sl], kd[:, sl], dec[:, sl],
                                        st[h], causal)
            oh_ref[rows, sl] = o
    for h in range(HG_HEADS):
        st_ref[h] = st[h]

    o_m = _mem_attention(_dot(xb, w_in_ref[:, OFF_MX:OFF_GATE]), mk_ref[0], mv_ref[0])
    mix = (_branch_gate(xb, 0, w_in_ref) * _dot(o_sw.astype(BF16), w_up_ref[0])
           + _branch_gate(xb, 2, w_in_ref) * _dot(o_m.astype(BF16), w_up_ref[2]))
    gate_h = _branch_gate(xb, 1, w_in_ref)

    @pl.when(jnp.min(hb) < -HG_SAFE_RANGE)
    def _():
        hq_ref[...] = hq
        hk_ref[...] = hk
        hv_ref[...] = hv
        hb_ref[...] = hb

        def chunk_body(c, carry):
            rows = pl.ds(pl.multiple_of(c * HG_CHUNK, HG_CHUNK), HG_CHUNK)
            for h in range(HG_HEADS):
                sl = slice(h * HG_DIM, (h + 1) * HG_DIM)
                o, st_new = _hgrn_chunk_exact(hq_ref[rows, sl], hk_ref[rows, sl], hv_ref[rows, sl], hb_ref[rows, sl],
                                              sto_ref[h])
                sto_ref[h] = st_new
                oh_ref[rows, sl] = o
            return carry

        lax.fori_loop(0, tm // HG_CHUNK, chunk_body, 0)
        st_ref[...] = sto_ref[...]

    mix = mix + gate_h * _dot(_rms_gate(oh_ref[...], hgate, hnw_ref[...]).astype(BF16), w_up_ref[1])
    y_ref[0] = _out_ffn(xn, mix, w_o_ref, w_f1_ref, w_f2_ref, lnp_ref)

    @pl.when(t == nt - 1)
    def _():
        for h in range(HG_HEADS):
            s_ref[0, h] = jnp.transpose(st_ref[h])


def _full(shape):
    nd = len(shape)
    return pl.BlockSpec(shape, lambda *_: (0,) * nd, pipeline_mode=pl.Buffered(1))


def _prompt_layer(x, mk_b, mv_b, sinks, lnp, hlb, hnw, w_in, w_up, w_o, w_f1, w_f2, *, tm):
    bsz, seq, _ = x.shape
    nt = seq // tm
    assert seq % tm == 0 and tm % WINDOW == 0 and tm % HG_CHUNK == 0
    bias = jnp.asarray(_prompt_bias())
    r = np.arange(tm)
    tri = jnp.asarray((r[:, None] // HG_CHUNK == r[None, :] // HG_CHUNK) & (r[:, None] >= r[None, :]), BF16)
    kern = functools.partial(_prompt_kernel, tm=tm, nt=nt)
    return pl.pallas_call(
        kern,
        grid=(bsz, nt),
        in_specs=[
            pl.BlockSpec(memory_space=pltpu.SMEM),
            pl.BlockSpec((1, tm, D_MODEL), lambda b, t: (b, t, 0)),
            pl.BlockSpec((1, MEM_LEN, MX_W), lambda b, t: (b, 0, 0)),
            pl.BlockSpec((1, MEM_LEN, MX_W), lambda b, t: (b, 0, 0)),
            _full(bias.shape), _full(tri.shape), _full(lnp.shape), _full(hlb.shape), _full(hnw.shape),
            _full(w_in.shape), _full(w_up.shape), _full(w_o.shape), _full(w_f1.shape), _full(w_f2.shape),
        ],
        out_specs=[
            pl.BlockSpec((1, tm, D_MODEL), lambda b, t: (b, t, 0)),
            pl.BlockSpec((1, WINDOW, SW_KV), lambda b, t: (b, 0, 0)),
            pl.BlockSpec((1, WINDOW, SW_KV), lambda b, t: (b, 0, 0)),
            pl.BlockSpec((1, HG_HEADS, HG_DIM, HG_DIM), lambda b, t: (b, 0, 0, 0)),
        ],
        out_shape=[
            jax.ShapeDtypeStruct((bsz, seq, D_MODEL), F32),
            jax.ShapeDtypeStruct((bsz, WINDOW, SW_KV), F32),
            jax.ShapeDtypeStruct((bsz, WINDOW, SW_KV), F32),
            jax.ShapeDtypeStruct((bsz, HG_HEADS, HG_DIM, HG_DIM), F32),
        ],
        scratch_shapes=[
            pltpu.VMEM((SW_KV_HEADS, WINDOW + tm, LANES), BF16),
            pltpu.VMEM((SW_KV_HEADS, WINDOW + tm, LANES), BF16),
            pltpu.VMEM((HG_HEADS, HG_DIM, HG_DIM), F32),
            pltpu.VMEM((HG_HEADS, HG_DIM, HG_DIM), F32),
            pltpu.VMEM((tm, HG_W), F32),
            pltpu.VMEM((tm, HG_W), F32),
            pltpu.VMEM((tm, HG_W), F32),
            pltpu.VMEM((tm, HG_W), F32),
            pltpu.VMEM((tm, HG_W), F32),
        ],
        compiler_params=pltpu.CompilerParams(
            dimension_semantics=("arbitrary", "arbitrary"), vmem_limit_bytes=VMEM_LIMIT_BYTES),
        name="prompt_layer",
    )(sinks, x, mk_b, mv_b, bias, tri, lnp, hlb, hnw, w_in, w_up, w_o, w_f1, w_f2)


def _mem_kv_kernel(m_ref, w_ref, mk_ref, mv_ref, mkb_ref, mvb_ref):
    kv = _dot(m_ref[0].astype(BF16), w_ref[...])
    mk_ref[0] = kv[:, 0:MX_W]
    mv_ref[0] = kv[:, MX_W:2 * MX_W]
    mkb_ref[0] = kv[:, 0:MX_W].astype(BF16)
    mvb_ref[0] = kv[:, MX_W:2 * MX_W].astype(BF16)


def _mem_kv(mem, w_kv):
    bsz = mem.shape[0]
    blk = pl.BlockSpec((1, MEM_LEN, MX_W), lambda b: (b, 0, 0))
    return pl.pallas_call(
        _mem_kv_kernel,
        grid=(bsz,),
        in_specs=[pl.BlockSpec((1, MEM_LEN, D_MODEL), lambda b: (b, 0, 0)), _full(w_kv.shape)],
        out_specs=[blk, blk, blk, blk],
        out_shape=[jax.ShapeDtypeStruct((bsz, MEM_LEN, MX_W), F32)] * 2
        + [jax.ShapeDtypeStruct((bsz, MEM_LEN, MX_W), BF16)] * 2,
        compiler_params=pltpu.CompilerParams(dimension_semantics=("arbitrary",)),
        name="mem_kv",
    )(mem, w_kv)


def _prep_params(p):
    assert p['w_in'].shape[0] == DEPTH == 1
    zeros = jnp.zeros((D_MODEL,), F32)
    return dict(
        lnp=jnp.stack([p['ln0_w'], p['ln0_b'], p['ln1_w'][0], p['ln1_b'][0], p['ln2_w'][0], p['ln2_b'][0],
                       zeros, zeros]).astype(F32),
        sinks=p['sw_sinks'][0].astype(F32),
        hlb=p['hg_lower_bound'].astype(F32),
        hnw=p['hg_norm_w'].astype(F32),
        w_in=p['w_in'][0].astype(BF16),
        w_up=jnp.stack([p['w_up_sw'][0], p['w_up_hg'][0], p['w_up_mx'][0]]).astype(BF16),
        w_o=p['w_o'][0].astype(BF16),
        w_f1=p['w_ffn_in'][0].astype(BF16),
        w_f2=p['w_ffn_out'][0].astype(BF16),
        w_kv=p['w_mem_kv'][0].astype(BF16),
    )


def _prompt_group(x, mem, p, *, tm, pp=None):
    pp = _prep_params(p) if pp is None else pp
    mk, mv, mk_b, mv_b = _mem_kv(mem, pp['w_kv'])
    y, nk, nv, s = _prompt_layer(x, mk_b, mv_b, pp['sinks'], pp['lnp'], pp['hlb'], pp['hnw'],
                                 pp['w_in'], pp['w_up'], pp['w_o'], pp['w_f1'], pp['w_f2'], tm=tm)
    return y, nk, nv, s, mk, mv


def _sample_proj_kernel(x_ref, lnp_ref, w_in_ref, xn_ref, proj_ref):
    xn = _layer_norm(x_ref[...], lnp_ref[0:1], lnp_ref[1:2])
    xn_ref[...] = xn
    proj_ref[...] = _dot(xn.astype(BF16), w_in_ref[:, 0:OFF_GATE])


def _sample_proj(x2, lnp, w_in, *, tr):
    rows = x2.shape[0]
    return pl.pallas_call(
        _sample_proj_kernel,
        grid=(rows // tr,),
        in_specs=[pl.BlockSpec((tr, D_MODEL), lambda i: (i, 0)), _full(lnp.shape), _full(w_in.shape)],
        out_specs=[pl.BlockSpec((tr, D_MODEL), lambda i: (i, 0)), pl.BlockSpec((tr, OFF_GATE), lambda i: (i, 0))],
        out_shape=[jax.ShapeDtypeStruct((rows, D_MODEL), F32), jax.ShapeDtypeStruct((rows, OFF_GATE), F32)],
        compiler_params=pltpu.CompilerParams(
            dimension_semantics=("arbitrary",), vmem_limit_bytes=VMEM_LIMIT_BYTES),
        name="sample_proj",
    )(x2, lnp, w_in)


def _sample_bias(t_new, n_keys):
    i = np.arange(t_new)[:, None]
    j = np.arange(n_keys)[None, :]
    kpos = np.where(j < WINDOW, PAST_LEN - WINDOW + j, PAST_LEN + j - WINDOW)
    dist = PAST_LEN + i - kpos
    valid = (dist >= 0) & (dist < WINDOW) & (j < WINDOW + t_new)
    out = np.zeros((SW_KV_HEADS, SW_GROUP, t_new, n_keys), np.float32)
    for h in range(SW_HEADS):
        slope = np.float32(2.0) ** np.float32(-8.0 * (h + 1) / SW_HEADS)
        out[h // SW_GROUP, h % SW_GROUP] = np.where(valid, -slope * dist.astype(np.float32), np.float32(NEG))
    return out.reshape(SW_KV_HEADS, SW_GROUP * t_new, n_keys)


def _sample_mix_kernel(sinks_ref, proj_ref, kbuf_ref, vbuf_ref, st_ref, mk_ref, mv_ref, bias_ref, hlb_ref, hnw_ref,
                       osw_ref, oh_ref, om_ref, nk_ref, nv_ref, ns_ref,
                       kk_ref, vv_ref, hq_ref, hk_ref, hb_ref, hg_ref, *, bt, t_new):
    nkeys = 2 * WINDOW
    r = bt * t_new
    hq, hk, _, hl, hgate = _hgrn_inputs(proj_ref[:, OFF_HG:OFF_MX], _lower_bound(hlb_ref[...]))
    ri = lax.broadcasted_iota(jnp.int32, (r, r), 0)
    ci = lax.broadcasted_iota(jnp.int32, (r, r), 1)
    same = (ri // t_new) == (ci // t_new)
    tri = jnp.where(jnp.logical_and(same, ri >= ci), 1.0, 0.0).astype(BF16)
    hb_ref[...] = _exact_dot(tri, hl)
    hq_ref[...] = hq
    hk_ref[...] = hk
    hg_ref[...] = hgate
    kk_ref[WINDOW + t_new:nkeys, :] = jnp.zeros((nkeys - WINDOW - t_new, LANES), F32)
    vv_ref[WINDOW + t_new:nkeys, :] = jnp.zeros((nkeys - WINDOW - t_new, LANES), F32)
    pad_rows = BF16_SUBLANES - t_new

    def pad(a):
        return jnp.concatenate([a, jnp.zeros((pad_rows, a.shape[1]), F32)], axis=0)

    def batch_body(i, carry):
        rows = pl.ds(pl.multiple_of(i * t_new, t_new), t_new)
        kk_ref[0:WINDOW, :] = kbuf_ref[i]
        vv_ref[0:WINDOW, :] = vbuf_ref[i]
        kk_ref[WINDOW:WINDOW + t_new, :] = proj_ref[rows, SW_Q:SW_Q + SW_KV]
        vv_ref[WINDOW:WINDOW + t_new, :] = proj_ref[rows, SW_Q + SW_KV:SW_Q + 2 * SW_KV]
        nk_ref[i] = kk_ref[t_new:t_new + WINDOW, :]
        nv_ref[i] = vv_ref[t_new:t_new + WINDOW, :]
        kds = _dup_heads(kk_ref[...])
        vds = _dup_heads(vv_ref[...])
        q = proj_ref[rows, 0:SW_Q] * (SW_HEAD_DIM ** -0.5)
        blocks = []
        for g in range(SW_KV_HEADS):
            s = _dot_nt(_stack_group_queries(q, g), kds[g].astype(BF16)) + bias_ref[g]
            blocks.extend(_sink_softmax_pv(s, vds[g].astype(BF16), sinks_ref, g, t_new))
        osw_ref[rows, :] = jnp.concatenate(blocks, axis=-1)
        mq = pad(proj_ref[rows, OFF_MX:OFF_GATE])
        om_ref[rows, :] = _mem_attention(mq, mk_ref[i].astype(BF16), mv_ref[i].astype(BF16))[0:t_new]
        for h in range(HG_HEADS):
            sl = slice(h * HG_DIM, (h + 1) * HG_DIM)
            qh = hq_ref[rows, sl]
            kh = hk_ref[rows, sl]
            vh = proj_ref[rows, OFF_HG + 2 * HG_W + h * HG_DIM:OFF_HG + 2 * HG_W + (h + 1) * HG_DIM]
            b = hb_ref[rows, sl]
            s0 = st_ref[i, h]
            o = _dot(pad(qh * jnp.exp(b)).astype(BF16), s0.astype(BF16))[0:t_new]
            b_last = b[t_new - 1:t_new]
            kd = pad(kh * jnp.exp(b_last - b)).astype(BF16)
            ns_ref[i, h] = _col_bcast(jnp.exp(b_last)) * s0 + _dot_tn(kd, pad(vh).astype(BF16))
            oh_ref[rows, sl] = o + _hgrn_diag(qh, kh, vh, b, t_new)
        return carry

    lax.fori_loop(0, bt, batch_body, 0)
    oh_ref[...] = _rms_gate(oh_ref[...], hg_ref[...], hnw_ref[...])


def _sample_mix(proj, kbuf, vbuf, st, mk, mv, sinks, hlb, hnw, *, bt, t_new):
    nb = kbuf.shape[0]
    r = bt * t_new
    assert nb % bt == 0 and t_new == 8 and r % BF16_SUBLANES == 0
    bias = jnp.asarray(_sample_bias(t_new, 2 * WINDOW))
    kern = functools.partial(_sample_mix_kernel, bt=bt, t_new=t_new)
    row_blk = pl.BlockSpec((r, HG_W), lambda i: (i, 0))
    kv_blk = pl.BlockSpec((bt, WINDOW, SW_KV), lambda i: (i, 0, 0))
    st_blk = pl.BlockSpec((bt, HG_HEADS, HG_DIM, HG_DIM), lambda i: (i, 0, 0, 0))
    mem_blk = pl.BlockSpec((bt, MEM_LEN, MX_W), lambda i: (i, 0, 0))
    return pl.pallas_call(
        kern,
        grid=(nb // bt,),
        in_specs=[pl.BlockSpec(memory_space=pltpu.SMEM),
                  pl.BlockSpec((r, OFF_GATE), lambda i: (i, 0)),
                  kv_blk, kv_blk, st_blk, mem_blk, mem_blk,
                  _full(bias.shape), _full(hlb.shape), _full(hnw.shape)],
        out_specs=[row_blk, row_blk, row_blk, kv_blk, kv_blk, st_blk],
        out_shape=[jax.ShapeDtypeStruct((nb * t_new, HG_W), F32)] * 3
        + [jax.ShapeDtypeStruct((nb, WINDOW, SW_KV), F32)] * 2
        + [jax.ShapeDtypeStruct((nb, HG_HEADS, HG_DIM, HG_DIM), F32)],
        scratch_shapes=[pltpu.VMEM((2 * WINDOW, LANES), F32), pltpu.VMEM((2 * WINDOW, LANES), F32),
                        pltpu.VMEM((r, HG_W), F32), pltpu.VMEM((r, HG_W), F32),
                        pltpu.VMEM((r, HG_W), F32), pltpu.VMEM((r, HG_W), F32)],
        compiler_params=pltpu.CompilerParams(
            dimension_semantics=("arbitrary",), vmem_limit_bytes=VMEM_LIMIT_BYTES),
        name="sample_mix",
    )(sinks, proj, kbuf, vbuf, st, mk, mv, bias, hlb, hnw)


def _sample_out_kernel(xn_ref, osw_ref, oh_ref, om_ref, lnp_ref, w_in_ref, w_up_ref, w_o_ref, w_f1_ref, w_f2_ref,
                       y_ref):
    xn = xn_ref[...]
    xb = xn.astype(BF16)
    mix = None
    for j, br_ref in enumerate((osw_ref, oh_ref, om_ref)):
        term = _branch_gate(xb, j, w_in_ref) * _dot(br_ref[...].astype(BF16), w_up_ref[j])
        mix = term if mix is None else mix + term
    y_ref[...] = _out_ffn(xn, mix, w_o_ref, w_f1_ref, w_f2_ref, lnp_ref)


def _sample_out(xn, osw, oh, om, lnp, w_in, w_up, w_o, w_f1, w_f2, *, tr):
    rows = xn.shape[0]
    wide = pl.BlockSpec((tr, D_MODEL), lambda i: (i, 0))
    half = pl.BlockSpec((tr, HG_W), lambda i: (i, 0))
    return pl.pallas_call(
        _sample_out_kernel,
        grid=(rows // tr,),
        in_specs=[wide, half, half, half, _full(lnp.shape), _full(w_in.shape), _full(w_up.shape),
                  _full(w_o.shape), _full(w_f1.shape), _full(w_f2.shape)],
        out_specs=wide,
        out_shape=jax.ShapeDtypeStruct((rows, D_MODEL), F32),
        compiler_params=pltpu.CompilerParams(
            dimension_semantics=("arbitrary",), vmem_limit_bytes=VMEM_LIMIT_BYTES),
        name="sample_out",
    )(xn, osw, oh, om, lnp, w_in, w_up, w_o, w_f1, w_f2)


def _sample_group(x, p, *, bt, pp=None):
    pp = _prep_params(p) if pp is None else pp
    nb, t_new, _ = x.shape
    rows = nb * t_new
    tr = min(rows, 256)
    xn, proj = _sample_proj(x.reshape(rows, D_MODEL), pp['lnp'], pp['w_in'], tr=tr)
    osw, oh, om, nk, nv, ns = _sample_mix(
        proj,
        p['cache_k_win'][0].reshape(nb, WINDOW, SW_KV), p['cache_v_win'][0].reshape(nb, WINDOW, SW_KV),
        p['state_hgrn'][0],
        p['cache_mem_k'][0].reshape(nb, MEM_LEN, MX_W), p['cache_mem_v'][0].reshape(nb, MEM_LEN, MX_W),
        pp['sinks'], pp['hlb'], pp['hnw'], bt=bt, t_new=t_new)
    y = _sample_out(xn, osw, oh, om, pp['lnp'], pp['w_in'], pp['w_up'], pp['w_o'], pp['w_f1'], pp['w_f2'], tr=tr)
    return y.reshape(nb, t_new, D_MODEL), nk, nv, ns


def kernel(x_prompt, x_sample, cache_k_win, cache_v_win, state_hgrn, cache_mem_k, cache_mem_v, mem_prompt,
           ln0_w, ln0_b, w_in, w_up_sw, w_up_hg, w_up_mx, sw_sinks, hg_lower_bound, hg_norm_w, w_mem_kv,
           w_o, ln1_w, ln1_b, w_ffn_in, w_ffn_out, ln2_w, ln2_b):
    p = dict(ln0_w=ln0_w, ln0_b=ln0_b, w_in=w_in, w_up_sw=w_up_sw, w_up_hg=w_up_hg, w_up_mx=w_up_mx,
             sw_sinks=sw_sinks, hg_lower_bound=hg_lower_bound, hg_norm_w=hg_norm_w, w_mem_kv=w_mem_kv,
             w_o=w_o, ln1_w=ln1_w, ln1_b=ln1_b, w_ffn_in=w_ffn_in, w_ffn_out=w_ffn_out, ln2_w=ln2_w, ln2_b=ln2_b)
    p.update(cache_k_win=cache_k_win, cache_v_win=cache_v_win, state_hgrn=state_hgrn,
             cache_mem_k=cache_mem_k, cache_mem_v=cache_mem_v)
    pp = _prep_params(p)
    bp, bs = x_prompt.shape[0], x_sample.shape[0]
    y, nk, nv, s, mk, mv = _prompt_group(x_prompt, mem_prompt, p, tm=PROMPT_TILE, pp=pp)
    ys, nks, nvs, ss = _sample_group(x_sample, p, bt=SAMPLE_BATCH_TILE, pp=pp)
    win = (WINDOW, SW_KV_HEADS, SW_HEAD_DIM)
    mem = (MEM_LEN, MX_HEADS, MX_HEAD_DIM)
    return (y, ys,
            nk.reshape(1, bp, *win), nv.reshape(1, bp, *win), s[None],
            mk.reshape(1, bp, *mem), mv.reshape(1, bp, *mem),
            nks.reshape(1, bs, *win), nvs.reshape(1, bs, *win), ss[None])
```

```python
import functools

import numpy as np
import jax
import jax.numpy as jnp
from jax import lax
from jax.experimental import pallas as pl
from jax.experimental.pallas import tpu as pltpu

D_MODEL = 1024
DEPTH = 1
PAST_LEN = 16384
SW_HEADS, SW_KV_HEADS, SW_HEAD_DIM = 8, 2, 64
SW_GROUP = SW_HEADS // SW_KV_HEADS
WINDOW = 128
HG_HEADS, HG_DIM = 4, 128
MEM_LEN, MX_HEADS, MX_HEAD_DIM = 256, 4, 128
SW_Q = SW_HEADS * SW_HEAD_DIM
SW_KV = SW_KV_HEADS * SW_HEAD_DIM
HG_W = HG_HEADS * HG_DIM
MX_W = MX_HEADS * MX_HEAD_DIM
D_FF = 2816
ALPHA = (2.0 * DEPTH) ** 0.25
LN_EPS = 1e-5
RMS_EPS = 1e-6
OFF_SW = 0
OFF_HG = SW_Q + 2 * SW_KV
OFF_MX = OFF_HG + 4 * HG_W
OFF_GATE = OFF_MX + MX_W
IN_WIDTH = OFF_GATE + 3 * D_MODEL

LANES = 128
BF16_SUBLANES = 16
VMEM_LIMIT_BYTES = 60 * 1024 * 1024

NEG = -1e30
HG_CHUNK = 64
HG_SUB = 16
HG_SAFE_RANGE = 80.0
PROMPT_TILE = 256
SAMPLE_BATCH_TILE = 4
F32 = jnp.float32
BF16 = jnp.bfloat16


def _dot(a, b):
    return jnp.dot(a, b, preferred_element_type=F32)


def _dot_nt(a, b):
    return lax.dot_general(a, b, (((1,), (1,)), ((), ())), preferred_element_type=F32)


def _dot_tn(a, b):
    return lax.dot_general(a, b, (((0,), (0,)), ((), ())), preferred_element_type=F32)


def _exact_dot(m_bf16, x):
    hi = x.astype(BF16)
    r1 = x - hi.astype(F32)
    mid = r1.astype(BF16)
    lo = (r1 - mid.astype(F32)).astype(BF16)
    return _dot(m_bf16, hi) + _dot(m_bf16, mid) + _dot(m_bf16, lo)


def _layer_norm(x, w, b):
    mu = jnp.mean(x, axis=-1, keepdims=True)
    xc = x - mu
    var = jnp.mean(xc * xc, axis=-1, keepdims=True)
    return xc * lax.rsqrt(var + LN_EPS) * w + b


def _sigmoid(x):
    return 1.0 / (1.0 + jnp.exp(-x))


def _silu(x):
    return x * _sigmoid(x)


def _lower_bound(hlb):
    m = jnp.max(hlb, axis=0, keepdims=True)
    e = jnp.exp(hlb - m)
    return e[0:1] / jnp.sum(e, axis=0, keepdims=True)


def _lane_lo(shape):
    return lax.broadcasted_iota(jnp.int32, shape, len(shape) - 1) < SW_HEAD_DIM


def _dup_heads(kv):
    rolled = pltpu.roll(kv, SW_HEAD_DIM, axis=1)
    lo = _lane_lo(kv.shape)
    return jnp.where(lo, kv, rolled), jnp.where(lo, rolled, kv)


def _sink_softmax_pv(s, vv, sinks_ref, g, rows):
    ps, dens = [], []
    for hh in range(SW_GROUP):
        sh = s[hh * rows:(hh + 1) * rows]
        sink = sinks_ref[g * SW_GROUP + hh]
        m = jnp.maximum(jnp.max(sh, axis=-1, keepdims=True), sink)
        p = jnp.exp(sh - m)
        dens.append(jnp.sum(p, axis=-1, keepdims=True) + jnp.exp(sink - m))
        ps.append(p)
    o = _dot(jnp.concatenate(ps, axis=0).astype(BF16), vv)
    o = o / jnp.concatenate(dens, axis=0)
    lo = _lane_lo((rows, LANES))
    return (jnp.where(lo, o[0:rows], o[rows:2 * rows]),
            jnp.where(lo, o[2 * rows:3 * rows], o[3 * rows:4 * rows]))


def _stack_group_queries(q, g):
    qa = q[:, 2 * g * LANES:(2 * g + 1) * LANES]
    qb = q[:, (2 * g + 1) * LANES:(2 * g + 2) * LANES]
    lo = _lane_lo(qa.shape)
    z = jnp.zeros_like(qa)
    return jnp.concatenate([jnp.where(lo, qa, z), jnp.where(lo, z, qa),
                            jnp.where(lo, qb, z), jnp.where(lo, z, qb)], axis=0).astype(BF16)


def _hgrn_diag(q, k, v, b, n):
    row = lax.broadcasted_iota(jnp.int32, (n, HG_DIM), 0)
    o = jnp.zeros((n, HG_DIM), F32)
    for s in range(n):
        t0 = (s // 8) * 8
        rr = row[t0:]
        e = jnp.exp(jnp.where(rr >= s, b[t0:] - b[s:s + 1], -jnp.inf))
        col = jnp.sum(q[t0:] * e * k[s:s + 1], axis=-1, keepdims=True)
        upd = col * v[s:s + 1]
        o = o + (upd if t0 == 0 else jnp.concatenate([jnp.zeros((t0, HG_DIM), F32), upd], axis=0))
    return o


def _col_bcast(rowvec):
    return jnp.transpose(jnp.broadcast_to(rowvec, (HG_DIM, HG_DIM)))


def _hgrn_chunk_exact(q, k, v, b, st):
    c = HG_CHUNK
    stb = st.astype(BF16)
    o = _dot_nt((q * jnp.exp(b)).astype(BF16), stb)
    b_last = b[c - 1:c]
    kd = (k * jnp.exp(b_last - b)).astype(BF16)
    vb = v.astype(BF16)
    s_new = st * jnp.exp(b_last) + _dot_tn(vb, kd)
    parts = []
    for i in range(c // HG_SUB):
        r0 = i * HG_SUB
        od = _hgrn_diag(q[r0:r0 + HG_SUB], k[r0:r0 + HG_SUB], v[r0:r0 + HG_SUB], b[r0:r0 + HG_SUB], HG_SUB)
        if i > 0:
            g_i = b[r0 - 1:r0]
            qi = (q[r0:r0 + HG_SUB] * jnp.exp(b[r0:r0 + HG_SUB] - g_i)).astype(BF16)
            ki = (k[0:r0] * jnp.exp(g_i - b[0:r0])).astype(BF16)
            a = _dot_nt(qi, ki)
            od = od + _dot(a.astype(BF16), vb[0:r0])
        parts.append(od)
    return o + jnp.concatenate(parts, axis=0), s_new


def _rms_gate(o, gate, nw):
    outs = []
    for h in range(HG_HEADS):
        oh = o[:, h * HG_DIM:(h + 1) * HG_DIM]
        ms = jnp.mean(oh * oh, axis=-1, keepdims=True)
        outs.append(oh * lax.rsqrt(ms + RMS_EPS) * nw * gate[:, h * HG_DIM:(h + 1) * HG_DIM])
    return jnp.concatenate(outs, axis=-1)


def _branch_gate(xb, j, w_in_ref):
    c0 = OFF_GATE + j * D_MODEL
    return _sigmoid(_dot(xb, w_in_ref[:, c0:c0 + D_MODEL]))


def _out_ffn(xn, mix, w_o_ref, w_f1_ref, w_f2_ref, lnp_ref):
    h = _layer_norm(ALPHA * xn + _dot(mix.astype(BF16), w_o_ref[...]), lnp_ref[2:3], lnp_ref[3:4])
    hb = h.astype(BF16)
    half = D_FF // 2
    acc = None
    for c in range(2):
        gt = _dot(hb, w_f1_ref[:, c * half:(c + 1) * half])
        up = _dot(hb, w_f1_ref[:, D_FF + c * half:D_FF + (c + 1) * half])
        part = _dot((_silu(gt) * up).astype(BF16), w_f2_ref[c * half:(c + 1) * half, :])
        acc = part if acc is None else acc + part
    return _layer_norm(ALPHA * h + acc, lnp_ref[4:5], lnp_ref[5:6])


def _hgrn_inputs(hp, lb):
    f = lb + (1.0 - lb) * _sigmoid(hp[:, HG_W:2 * HG_W])
    return (_silu(hp[:, 0:HG_W]), 1.0 - f, hp[:, 2 * HG_W:3 * HG_W], jnp.log(f), _silu(hp[:, 3 * HG_W:4 * HG_W]))


def _prompt_bias():
    i = np.arange(WINDOW)[:, None]
    j = np.arange(2 * WINDOW)[None, :]
    dist = i + WINDOW - j
    valid = (dist >= 0) & (dist < WINDOW)
    out = np.zeros((SW_KV_HEADS, SW_GROUP, WINDOW, 2 * WINDOW), np.float32)
    for h in range(SW_HEADS):
        slope = np.float32(2.0) ** np.float32(-8.0 * (h + 1) / SW_HEADS)
        out[h // SW_GROUP, h % SW_GROUP] = np.where(valid, -slope * dist.astype(np.float32), np.float32(NEG))
    return out.reshape(SW_KV_HEADS, SW_GROUP * WINDOW, 2 * WINDOW)


def _prompt_kernel(sinks_ref, x_ref, mk_ref, mv_ref, bias_ref, tri_ref, lnp_ref, hlb_ref, hnw_ref,
                   w_in_ref, w_up_ref, w_o_ref, w_f1_ref, w_f2_ref,
                   y_ref, nk_ref, nv_ref, s_ref,
                   kd_ref, vd_ref, st_ref, sto_ref, hq_ref, hk_ref, hv_ref, hb_ref, oh_ref, *, tm, nt):
    t = pl.program_id(1)

    @pl.when(t == 0)
    def _():
        kd_ref[:, 0:WINDOW, :] = jnp.zeros((SW_KV_HEADS, WINDOW, LANES), BF16)
        vd_ref[:, 0:WINDOW, :] = jnp.zeros((SW_KV_HEADS, WINDOW, LANES), BF16)
        st_ref[...] = jnp.zeros_like(st_ref)

    xn = _layer_norm(x_ref[0], lnp_ref[0:1], lnp_ref[1:2])
    xb = xn.astype(BF16)

    qkv = _dot(xb, w_in_ref[:, OFF_SW:OFF_HG])
    q = qkv[:, 0:SW_Q] * (SW_HEAD_DIM ** -0.5)
    k = qkv[:, SW_Q:SW_Q + SW_KV]
    v = qkv[:, SW_Q + SW_KV:SW_Q + 2 * SW_KV]

    nk_ref[0] = k[tm - WINDOW:tm]
    nv_ref[0] = v[tm - WINDOW:tm]

    for g, dup in enumerate(_dup_heads(k)):
        kd_ref[g, WINDOW:WINDOW + tm, :] = dup.astype(BF16)
    for g, dup in enumerate(_dup_heads(v)):
        vd_ref[g, WINDOW:WINDOW + tm, :] = dup.astype(BF16)

    col = lax.broadcasted_iota(jnp.int32, (1, 2 * WINDOW), 1)
    first = jnp.where(jnp.logical_and(t == 0, col < WINDOW), NEG, 0.0).astype(F32)
    def proj(c0, width):
        return _dot(xb, w_in_ref[:, c0:c0 + width])

    hp_parts = []
    o_rows = []
    for j in range(tm // WINDOW):
        qj = q[j * WINDOW:(j + 1) * WINDOW]
        blocks = []
        for g in range(SW_KV_HEADS):
            kk = kd_ref[g, j * WINDOW:(j + 2) * WINDOW, :]
            vv = vd_ref[g, j * WINDOW:(j + 2) * WINDOW, :]
            s = _dot_nt(_stack_group_queries(qj, g), kk) + bias_ref[g]
            if j == 0:
                s = s + first
            if len(hp_parts) < 4:
                hp_parts.append(proj(OFF_HG + len(hp_parts) * HG_W, HG_W))
            blocks.extend(_sink_softmax_pv(s, vv, sinks_ref, g, WINDOW))
        o_rows.append(jnp.concatenate(blocks, axis=-1))
    o_sw = jnp.concatenate(o_rows, axis=0)
    while len(hp_parts) < 4:
        hp_parts.append(proj(OFF_HG + len(hp_parts) * HG_W, HG_W))
    kd_ref[:, 0:WINDOW, :] = kd_ref[:, tm:tm + WINDOW, :]
    vd_ref[:, 0:WINDOW, :] = vd_ref[:, tm:tm + WINDOW, :]

    hq, hk, hv, hl, hgate = _hgrn_inputs(jnp.concatenate(hp_parts, axis=-1), _lower_bound(hlb_ref[...]))
    hb = _exact_dot(tri_ref[...], hl)
    mq = proj(OFF_MX, MX_W)
    gate_sw = _branch_gate(xb, 0, w_in_ref)
    sto_ref[...] = st_ref[...]
    qe = (hq * jnp.exp(hb)).astype(BF16)
    ke = (hk * jnp.exp(-hb)).astype(BF16)
    vb = hv.astype(BF16)
    ri = lax.broadcasted_iota(jnp.int32, (HG_CHUNK, HG_CHUNK), 0)
    ci = lax.broadcasted_iota(jnp.int32, (HG_CHUNK, HG_CHUNK), 1)
    causal = ri >= ci
    n_chunks = tm // HG_CHUNK
    heads = [slice(h * HG_DIM, (h + 1) * HG_DIM) for h in range(HG_HEADS)]
    chunks = [slice(c * HG_CHUNK, (c + 1) * HG_CHUNK) for c in range(n_chunks)]
    a_mat, upd, dec = {}, {}, []
    for c, rows in enumerate(chunks):
        b_last = hb[(c + 1) * HG_CHUNK - 1:(c + 1) * HG_CHUNK]
        kd = (hk[rows] * jnp.exp(b_last - hb[rows])).astype(BF16)
        dec.append(jnp.exp(b_last))
        for h, sl in enumerate(heads):
            a_mat[c, h] = jnp.where(causal, _dot_nt(qe[rows, sl], ke[rows, sl]), 0.0).astype(BF16)
            upd[c, h] = _dot_tn(vb[rows, sl], kd[:, sl])
    gate_mx = _branch_gate(xb, 2, w_in_ref)
    st = {(0, h): st_ref[h] for h in range(HG_HEADS)}
    for c in range(n_chunks):
        for h, sl in enumerate(heads):
            st[c + 1, h] = st[c, h] * dec[c][:, sl] + upd[c, h]
    for h in range(HG_HEADS):
        st_ref[h] = st[n_chunks, h]
    for c, rows in enumerate(chunks):
        for h, sl in enumerate(heads):
            oh_ref[rows, sl] = _dot_nt(qe[rows, sl], st[c, h].astype(BF16)) + _dot(a_mat[c, h], vb[rows, sl])
    up_sw = _dot(o_sw.astype(BF16), w_up_ref[0])

    fillers = [lambda: proj(OFF_GATE + D_MODEL, D_MODEL // 2), lambda: proj(OFF_GATE + D_MODEL + D_MODEL // 2, D_MODEL // 2)]
    filled = []
    mk = mk_ref[0]
    mv = mv_ref[0]
    outs = []
    for h in range(MX_HEADS):
        sl = slice(h * MX_HEAD_DIM, (h + 1) * MX_HEAD_DIM)
        s = _dot_nt(mq[:, sl].astype(BF16), mk[:, sl]) * (MX_HEAD_DIM ** -0.5)
        if h % 2 == 0:
            filled.append(fillers[h // 2]())
        p = jnp.exp(s - jnp.max(s, axis=-1, keepdims=True))
        den = jnp.sum(p, axis=-1, keepdims=True)
        outs.append(_dot(p.astype(BF16), mv[:, sl]) / den)
    o_m = jnp.concatenate(outs, axis=-1)
    gate_h = _sigmoid(jnp.concatenate(filled, axis=-1))
    mix = gate_sw * up_sw + gate_mx * _dot(o_m.astype(BF16), w_up_ref[2])

    @pl.when(jnp.min(hb) < -HG_SAFE_RANGE)
    def _():
        hq_ref[...] = hq
        hk_ref[...] = hk
        hv_ref[...] = hv
        hb_ref[...] = hb

        def chunk_body(c, carry):
            rows = pl.ds(pl.multiple_of(c * HG_CHUNK, HG_CHUNK), HG_CHUNK)
            for h in range(HG_HEADS):
                sl = slice(h * HG_DIM, (h + 1) * HG_DIM)
                o, st_new = _hgrn_chunk_exact(hq_ref[rows, sl], hk_ref[rows, sl], hv_ref[rows, sl], hb_ref[rows, sl],
                                              sto_ref[h])
                sto_ref[h] = st_new
                oh_ref[rows, sl] = o
            return carry

        lax.fori_loop(0, tm // HG_CHUNK, chunk_body, 0)
        st_ref[...] = sto_ref[...]

    mix = mix + gate_h * _dot(_rms_gate(oh_ref[...], hgate, hnw_ref[...]).astype(BF16), w_up_ref[1])
    y_ref[0] = _out_ffn(xn, mix, w_o_ref, w_f1_ref, w_f2_ref, lnp_ref)

    @pl.when(t == nt - 1)
    def _():
        for h in range(HG_HEADS):
            s_ref[0, h] = jnp.transpose(st_ref[h])


def _full(shape):
    nd = len(shape)
    return pl.BlockSpec(shape, lambda *_: (0,) * nd, pipeline_mode=pl.Buffered(1))


def _prompt_layer(x, mk_b, mv_b, sinks, lnp, hlb, hnw, w_in, w_up, w_o, w_f1, w_f2, *, tm):
    bsz, seq, _ = x.shape
    nt = seq // tm
    assert seq % tm == 0 and tm % WINDOW == 0 and tm % HG_CHUNK == 0
    bias = jnp.asarray(_prompt_bias())
    r = np.arange(tm)
    tri = jnp.asarray((r[:, None] // HG_CHUNK == r[None, :] // HG_CHUNK) & (r[:, None] >= r[None, :]), BF16)
    kern = functools.partial(_prompt_kernel, tm=tm, nt=nt)
    return pl.pallas_call(
        kern,
        grid=(bsz, nt),
        in_specs=[
            pl.BlockSpec(memory_space=pltpu.SMEM),
            pl.BlockSpec((1, tm, D_MODEL), lambda b, t: (b, t, 0)),
            pl.BlockSpec((1, MEM_LEN, MX_W), lambda b, t: (b, 0, 0)),
            pl.BlockSpec((1, MEM_LEN, MX_W), lambda b, t: (b, 0, 0)),
            _full(bias.shape), _full(tri.shape), _full(lnp.shape), _full(hlb.shape), _full(hnw.shape),
            _full(w_in.shape), _full(w_up.shape), _full(w_o.shape), _full(w_f1.shape), _full(w_f2.shape),
        ],
        out_specs=[
            pl.BlockSpec((1, tm, D_MODEL), lambda b, t: (b, t, 0)),
            pl.BlockSpec((1, WINDOW, SW_KV), lambda b, t: (b, 0, 0)),
            pl.BlockSpec((1, WINDOW, SW_KV), lambda b, t: (b, 0, 0)),
            pl.BlockSpec((1, HG_HEADS, HG_DIM, HG_DIM), lambda b, t: (b, 0, 0, 0)),
        ],
        out_shape=[
            jax.ShapeDtypeStruct((bsz, seq, D_MODEL), F32),
            jax.ShapeDtypeStruct((bsz, WINDOW, SW_KV), F32),
            jax.ShapeDtypeStruct((bsz, WINDOW, SW_KV), F32),
            jax.ShapeDtypeStruct((bsz, HG_HEADS, HG_DIM, HG_DIM), F32),
        ],
        scratch_shapes=[
            pltpu.VMEM((SW_KV_HEADS, WINDOW + tm, LANES), BF16),
            pltpu.VMEM((SW_KV_HEADS, WINDOW + tm, LANES), BF16),
            pltpu.VMEM((HG_HEADS, HG_DIM, HG_DIM), F32),
            pltpu.VMEM((HG_HEADS, HG_DIM, HG_DIM), F32),
            pltpu.VMEM((tm, HG_W), F32),
            pltpu.VMEM((tm, HG_W), F32),
            pltpu.VMEM((tm, HG_W), F32),
            pltpu.VMEM((tm, HG_W), F32),
            pltpu.VMEM((tm, HG_W), F32),
        ],
        compiler_params=pltpu.CompilerParams(
            dimension_semantics=("arbitrary", "arbitrary"), vmem_limit_bytes=VMEM_LIMIT_BYTES),
        name="prompt_layer",
    )(sinks, x, mk_b, mv_b, bias, tri, lnp, hlb, hnw, w_in, w_up, w_o, w_f1, w_f2)


def _mem_kv_kernel(m_ref, w_ref, mk_ref, mv_ref, mkb_ref, mvb_ref):
    kv = _dot(m_ref[0].astype(BF16), w_ref[...])
    mk_ref[0] = kv[:, 0:MX_W]
    mv_ref[0] = kv[:, MX_W:2 * MX_W]
    mkb_ref[0] = kv[:, 0:MX_W].astype(BF16)
    mvb_ref[0] = kv[:, MX_W:2 * MX_W].astype(BF16)


def _mem_kv(mem, w_kv):
    bsz = mem.shape[0]
    blk = pl.BlockSpec((1, MEM_LEN, MX_W), lambda b: (b, 0, 0))
    return pl.pallas_call(
        _mem_kv_kernel,
        grid=(bsz,),
        in_specs=[pl.BlockSpec((1, MEM_LEN, D_MODEL), lambda b: (b, 0, 0)), _full(w_kv.shape)],
        out_specs=[blk, blk, blk, blk],
        out_shape=[jax.ShapeDtypeStruct((bsz, MEM_LEN, MX_W), F32)] * 2
        + [jax.ShapeDtypeStruct((bsz, MEM_LEN, MX_W), BF16)] * 2,
        compiler_params=pltpu.CompilerParams(dimension_semantics=("arbitrary",)),
        name="mem_kv",
    )(mem, w_kv)


def _prep_params(p):
    assert p['w_in'].shape[0] == DEPTH == 1
    zeros = jnp.zeros((D_MODEL,), F32)
    return dict(
        lnp=jnp.stack([p['ln0_w'], p['ln0_b'], p['ln1_w'][0], p['ln1_b'][0], p['ln2_w'][0], p['ln2_b'][0],
                       zeros, zeros]).astype(F32),
        sinks=p['sw_sinks'][0].astype(F32),
        hlb=p['hg_lower_bound'].astype(F32),
        hnw=p['hg_norm_w'].astype(F32),
        w_in=p['w_in'][0].astype(BF16),
        w_up=jnp.stack([p['w_up_sw'][0], p['w_up_hg'][0], p['w_up_mx'][0]]).astype(BF16),
        w_o=p['w_o'][0].astype(BF16),
        w_f1=p['w_ffn_in'][0].astype(BF16),
        w_f2=p['w_ffn_out'][0].astype(BF16),
        w_kv=p['w_mem_kv'][0].astype(BF16),
    )


def _prompt_group(x, mem, p, *, tm, pp=None):
    pp = _prep_params(p) if pp is None else pp
    mk, mv, mk_b, mv_b = _mem_kv(mem, pp['w_kv'])
    y, nk, nv, s = _prompt_layer(x, mk_b, mv_b, pp['sinks'], pp['lnp'], pp['hlb'], pp['hnw'],
                                 pp['w_in'], pp['w_up'], pp['w_o'], pp['w_f1'], pp['w_f2'], tm=tm)
    return y, nk, nv, s, mk, mv


def _sample_proj_kernel(x_ref, lnp_ref, w_in_ref, xn_ref, proj_ref):
    xn = _layer_norm(x_ref[...], lnp_ref[0:1], lnp_ref[1:2])
    xn_ref[...] = xn
    proj_ref[...] = _dot(xn.astype(BF16), w_in_ref[:, 0:OFF_GATE])


def _sample_proj(x2, lnp, w_in, *, tr):
    rows = x2.shape[0]
    return pl.pallas_call(
        _sample_proj_kernel,
        grid=(rows // tr,),
        in_specs=[pl.BlockSpec((tr, D_MODEL), lambda i: (i, 0)), _full(lnp.shape), _full(w_in.shape)],
        out_specs=[pl.BlockSpec((tr, D_MODEL), lambda i: (i, 0)), pl.BlockSpec((tr, OFF_GATE), lambda i: (i, 0))],
        out_shape=[jax.ShapeDtypeStruct((rows, D_MODEL), F32), jax.ShapeDtypeStruct((rows, OFF_GATE), F32)],
        compiler_params=pltpu.CompilerParams(
            dimension_semantics=("arbitrary",), vmem_limit_bytes=VMEM_LIMIT_BYTES),
        name="sample_proj",
    )(x2, lnp, w_in)


def _sample_bias(t_new, n_keys):
    i = np.arange(t_new)[:, None]
    j = np.arange(n_keys)[None, :]
    kpos = np.where(j < WINDOW, PAST_LEN - WINDOW + j, PAST_LEN + j - WINDOW)
    dist = PAST_LEN + i - kpos
    valid = (dist >= 0) & (dist < WINDOW) & (j < WINDOW + t_new)
    out = np.zeros((SW_KV_HEADS, SW_GROUP, t_new, n_keys), np.float32)
    for h in range(SW_HEADS):
        slope = np.float32(2.0) ** np.float32(-8.0 * (h + 1) / SW_HEADS)
        out[h // SW_GROUP, h % SW_GROUP] = np.where(valid, -slope * dist.astype(np.float32), np.float32(NEG))
    return out.reshape(SW_KV_HEADS, SW_GROUP * t_new, n_keys)


def _sample_mix_kernel(sinks_ref, proj_ref, kbuf_ref, vbuf_ref, st_ref, mk_ref, mv_ref, bias_ref, hlb_ref, hnw_ref,
                       osw_ref, oh_ref, om_ref, nk_ref, nv_ref, ns_ref,
                       kk_ref, vv_ref, *, bt, t_new):
    nkeys = 2 * WINDOW
    r = bt * t_new
    elems = [slice(i * t_new, (i + 1) * t_new) for i in range(bt)]
    heads = [slice(h * HG_DIM, (h + 1) * HG_DIM) for h in range(HG_HEADS)]
    pad_rows = BF16_SUBLANES - t_new

    def pad(a):
        return jnp.concatenate([a, jnp.zeros((pad_rows, a.shape[1]), F32)], axis=0)

    hq, hk, hv, hl, hgate = _hgrn_inputs(proj_ref[:, OFF_HG:OFF_MX], _lower_bound(hlb_ref[...]))
    ri = lax.broadcasted_iota(jnp.int32, (r, r), 0)
    ci = lax.broadcasted_iota(jnp.int32, (r, r), 1)
    same = (ri // t_new) == (ci // t_new)
    tri = jnp.where(jnp.logical_and(same, ri >= ci), 1.0, 0.0).astype(BF16)
    hb = _exact_dot(tri, hl)
    qe = hq * jnp.exp(hb)
    q_all = proj_ref[:, 0:SW_Q] * (SW_HEAD_DIM ** -0.5)
    mq_all = proj_ref[:, OFF_MX:OFF_GATE]

    zeros = jnp.zeros((nkeys - WINDOW - t_new, LANES), F32)
    for i, rows in enumerate(elems):
        for buf_ref, cache_ref, c0, out_ref in ((kk_ref, kbuf_ref, SW_Q, nk_ref),
                                                (vv_ref, vbuf_ref, SW_Q + SW_KV, nv_ref)):
            buf_ref[i, 0:WINDOW, :] = cache_ref[i]
            buf_ref[i, WINDOW:WINDOW + t_new, :] = proj_ref[rows, c0:c0 + SW_KV]
            buf_ref[i, WINDOW + t_new:nkeys, :] = zeros
            out_ref[i] = buf_ref[i, t_new:t_new + WINDOW, :]

    s_att, s_mem, o_state, upd, b_last = {}, {}, {}, {}, {}
    for i, rows in enumerate(elems):
        kds = _dup_heads(kk_ref[i])
        for g in range(SW_KV_HEADS):
            s_att[i, g] = _dot_nt(_stack_group_queries(q_all[rows], g), kds[g].astype(BF16)) + bias_ref[g]
    for i, rows in enumerate(elems):
        mkb = mk_ref[i].astype(BF16)
        mq = pad(mq_all[rows]).astype(BF16)
        for h, sl in enumerate(heads):
            s_mem[i, h] = _dot_nt(mq[:, sl], mkb[:, sl]) * (MX_HEAD_DIM ** -0.5)
    for i, rows in enumerate(elems):
        b_last[i] = hb[(i + 1) * t_new - 1:(i + 1) * t_new]
        kd = pad(hk[rows] * jnp.exp(b_last[i] - hb[rows])).astype(BF16)
        vb = pad(hv[rows]).astype(BF16)
        qb = pad(qe[rows]).astype(BF16)
        for h, sl in enumerate(heads):
            o_state[i, h] = _dot(qb[:, sl], st_ref[i, h].astype(BF16))[0:t_new]
            upd[i, h] = _dot_tn(kd[:, sl], vb[:, sl])

    p_att, den_att, p_mem, den_mem = {}, {}, {}, {}
    for i in range(bt):
        for g in range(SW_KV_HEADS):
            ps, dens = [], []
            for hh in range(SW_GROUP):
                sh = s_att[i, g][hh * t_new:(hh + 1) * t_new]
                sink = sinks_ref[g * SW_GROUP + hh]
                m = jnp.maximum(jnp.max(sh, axis=-1, keepdims=True), sink)
                p = jnp.exp(sh - m)
                dens.append(jnp.sum(p, axis=-1, keepdims=True) + jnp.exp(sink - m))
                ps.append(p)
            p_att[i, g] = jnp.concatenate(ps, axis=0).astype(BF16)
            den_att[i, g] = jnp.concatenate(dens, axis=0)
        for h in range(MX_HEADS):
            p = jnp.exp(s_mem[i, h] - jnp.max(s_mem[i, h], axis=-1, keepdims=True))
            den_mem[i, h] = jnp.sum(p, axis=-1, keepdims=True)
            p_mem[i, h] = p.astype(BF16)
    for i, rows in enumerate(elems):
        dec = jnp.exp(b_last[i])
        for h, sl in enumerate(heads):
            ns_ref[i, h] = _col_bcast(dec[:, sl]) * st_ref[i, h] + upd[i, h]
            oh_ref[rows, sl] = o_state[i, h] + _hgrn_diag(hq[rows, sl], hk[rows, sl], hv[rows, sl], hb[rows, sl],
                                                          t_new)

    lo = _lane_lo((t_new, LANES))
    for i, rows in enumerate(elems):
        vds = _dup_heads(vv_ref[i])
        blocks = []
        for g in range(SW_KV_HEADS):
            o = _dot(p_att[i, g], vds[g].astype(BF16)) / den_att[i, g]
            blocks.append(jnp.where(lo, o[0:t_new], o[t_new:2 * t_new]))
            blocks.append(jnp.where(lo, o[2 * t_new:3 * t_new], o[3 * t_new:4 * t_new]))
        osw_ref[rows, :] = jnp.concatenate(blocks, axis=-1)
    for i, rows in enumerate(elems):
        mvb = mv_ref[i].astype(BF16)
        outs = [(_dot(p_mem[i, h], mvb[:, sl]) / den_mem[i, h])[0:t_new] for h, sl in enumerate(heads)]
        om_ref[rows, :] = jnp.concatenate(outs, axis=-1)
    oh_ref[...] = _rms_gate(oh_ref[...], hgate, hnw_ref[...])


def _sample_mix(proj, kbuf, vbuf, st, mk, mv, sinks, hlb, hnw, *, bt, t_new):
    nb = kbuf.shape[0]
    r = bt * t_new
    assert nb % bt == 0 and t_new == 8 and r % BF16_SUBLANES == 0
    bias = jnp.asarray(_sample_bias(t_new, 2 * WINDOW))
    kern = functools.partial(_sample_mix_kernel, bt=bt, t_new=t_new)
    row_blk = pl.BlockSpec((r, HG_W), lambda i: (i, 0))
    kv_blk = pl.BlockSpec((bt, WINDOW, SW_KV), lambda i: (i, 0, 0))
    st_blk = pl.BlockSpec((bt, HG_HEADS, HG_DIM, HG_DIM), lambda i: (i, 0, 0, 0))
    mem_blk = pl.BlockSpec((bt, MEM_LEN, MX_W), lambda i: (i, 0, 0))
    return pl.pallas_call(
        kern,
        grid=(nb // bt,),
        in_specs=[pl.BlockSpec(memory_space=pltpu.SMEM),
                  pl.BlockSpec((r, OFF_GATE), lambda i: (i, 0)),
                  kv_blk, kv_blk, st_blk, mem_blk, mem_blk,
                  _full(bias.shape), _full(hlb.shape), _full(hnw.shape)],
        out_specs=[row_blk, row_blk, row_blk, kv_blk, kv_blk, st_blk],
        out_shape=[jax.ShapeDtypeStruct((nb * t_new, HG_W), F32)] * 3
        + [jax.ShapeDtypeStruct((nb, WINDOW, SW_KV), F32)] * 2
        + [jax.ShapeDtypeStruct((nb, HG_HEADS, HG_DIM, HG_DIM), F32)],
        scratch_shapes=[pltpu.VMEM((bt, 2 * WINDOW, LANES), F32), pltpu.VMEM((bt, 2 * WINDOW, LANES), F32)],
        compiler_params=pltpu.CompilerParams(
            dimension_semantics=("arbitrary",), vmem_limit_bytes=VMEM_LIMIT_BYTES),
        name="sample_mix",
    )(sinks, proj, kbuf, vbuf, st, mk, mv, bias, hlb, hnw)


def _sample_out_kernel(xn_ref, osw_ref, oh_ref, om_ref, lnp_ref, w_in_ref, w_up_ref, w_o_ref, w_f1_ref, w_f2_ref,
                       y_ref):
    xn = xn_ref[...]
    xb = xn.astype(BF16)
    mix = None
    for j, br_ref in enumerate((osw_ref, oh_ref, om_ref)):
        term = _branch_gate(xb, j, w_in_ref) * _dot(br_ref[...].astype(BF16), w_up_ref[j])
        mix = term if mix is None else mix + term
    y_ref[...] = _out_ffn(xn, mix, w_o_ref, w_f1_ref, w_f2_ref, lnp_ref)


def _sample_out(xn, osw, oh, om, lnp, w_in, w_up, w_o, w_f1, w_f2, *, tr):
    rows = xn.shape[0]
    wide = pl.BlockSpec((tr, D_MODEL), lambda i: (i, 0))
    half = pl.BlockSpec((tr, HG_W), lambda i: (i, 0))
    return pl.pallas_call(
        _sample_out_kernel,
        grid=(rows // tr,),
        in_specs=[wide, half, half, half, _full(lnp.shape), _full(w_in.shape), _full(w_up.shape),
                  _full(w_o.shape), _full(w_f1.shape), _full(w_f2.shape)],
        out_specs=wide,
        out_shape=jax.ShapeDtypeStruct((rows, D_MODEL), F32),
        compiler_params=pltpu.CompilerParams(
            dimension_semantics=("arbitrary",), vmem_limit_bytes=VMEM_LIMIT_BYTES),
        name="sample_out",
    )(xn, osw, oh, om, lnp, w_in, w_up, w_o, w_f1, w_f2)


def _sample_group(x, p, *, bt, pp=None):
    pp = _prep_params(p) if pp is None else pp
    nb, t_new, _ = x.shape
    rows = nb * t_new
    tr = min(rows, 256)
    xn, proj = _sample_proj(x.reshape(rows, D_MODEL), pp['lnp'], pp['w_in'], tr=tr)
    osw, oh, om, nk, nv, ns = _sample_mix(
        proj,
        p['cache_k_win'][0].reshape(nb, WINDOW, SW_KV), p['cache_v_win'][0].reshape(nb, WINDOW, SW_KV),
        p['state_hgrn'][0],
        p['cache_mem_k'][0].reshape(nb, MEM_LEN, MX_W), p['cache_mem_v'][0].reshape(nb, MEM_LEN, MX_W),
        pp['sinks'], pp['hlb'], pp['hnw'], bt=bt, t_new=t_new)
    y = _sample_out(xn, osw, oh, om, pp['lnp'], pp['w_in'], pp['w_up'], pp['w_o'], pp['w_f1'], pp['w_f2'], tr=tr)
    return y.reshape(nb, t_new, D_MODEL), nk, nv, ns


def kernel(x_prompt, x_sample, cache_k_win, cache_v_win, state_hgrn, cache_mem_k, cache_mem_v, mem_prompt,
           ln0_w, ln0_b, w_in, w_up_sw, w_up_hg, w_up_mx, sw_sinks, hg_lower_bound, hg_norm_w, w_mem_kv,
           w_o, ln1_w, ln1_b, w_ffn_in, w_ffn_out, ln2_w, ln2_b):
    p = dict(ln0_w=ln0_w, ln0_b=ln0_b, w_in=w_in, w_up_sw=w_up_sw, w_up_hg=w_up_hg, w_up_mx=w_up_mx,
             sw_sinks=sw_sinks, hg_lower_bound=hg_lower_bound, hg_norm_w=hg_norm_w, w_mem_kv=w_mem_kv,
             w_o=w_o, ln1_w=ln1_w, ln1_b=ln1_b, w_ffn_in=w_ffn_in, w_ffn_out=w_ffn_out, ln2_w=ln2_w, ln2_b=ln2_b)
    p.update(cache_k_win=cache_k_win, cache_v_win=cache_v_win, state_hgrn=state_hgrn,
             cache_mem_k=cache_mem_k, cache_mem_v=cache_mem_v)
    pp = _prep_params(p)
    bp, bs = x_prompt.shape[0], x_sample.shape[0]
    y, nk, nv, s, mk, mv = _prompt_group(x_prompt, mem_prompt, p, tm=PROMPT_TILE, pp=pp)
    ys, nks, nvs, ss = _sample_group(x_sample, p, bt=SAMPLE_BATCH_TILE, pp=pp)
    win = (WINDOW, SW_KV_HEADS, SW_HEAD_DIM)
    mem = (MEM_LEN, MX_HEADS, MX_HEAD_DIM)
    return (y, ys,
            nk.reshape(1, bp, *win), nv.reshape(1, bp, *win), s[None],
            mk.reshape(1, bp, *mem), mv.reshape(1, bp, *mem),
            nks.reshape(1, bs, *win), nvs.reshape(1, bs, *win), ss[None])
```

```python
import functools

import numpy as np
import jax
import jax.numpy as jnp
from jax import lax
from jax.experimental import pallas as pl
from jax.experimental.pallas import tpu as pltpu

D_MODEL = 1024
DEPTH = 1
PAST_LEN = 16384
SW_HEADS, SW_KV_HEADS, SW_HEAD_DIM = 8, 2, 64
SW_GROUP = SW_HEADS // SW_KV_HEADS
WINDOW = 128
HG_HEADS, HG_DIM = 4, 128
MEM_LEN, MX_HEADS, MX_HEAD_DIM = 256, 4, 128
SW_Q = SW_HEADS * SW_HEAD_DIM
SW_KV = SW_KV_HEADS * SW_HEAD_DIM
HG_W = HG_HEADS * HG_DIM
MX_W = MX_HEADS * MX_HEAD_DIM
D_FF = 2816
ALPHA = (2.0 * DEPTH) ** 0.25
LN_EPS = 1e-5
RMS_EPS = 1e-6
OFF_SW = 0
OFF_HG = SW_Q + 2 * SW_KV
OFF_MX = OFF_HG + 4 * HG_W
OFF_GATE = OFF_MX + MX_W
IN_WIDTH = OFF_GATE + 3 * D_MODEL

LANES = 128
BF16_SUBLANES = 16
MXU_TILE = 256
FFN_CHUNK = 4 * MXU_TILE
VMEM_LIMIT_BYTES = 60 * 1024 * 1024

NEG = -1e30
HG_CHUNK = 64
HG_SUB = 16
HG_SAFE_RANGE = 80.0
PROMPT_TILE = 256
SAMPLE_BATCH_TILE = 4
F32 = jnp.float32
BF16 = jnp.bfloat16


def _dot(a, b):
    return jnp.dot(a, b, preferred_element_type=F32)


def _dot_nt(a, b):
    return lax.dot_general(a, b, (((1,), (1,)), ((), ())), preferred_element_type=F32)


def _dot_tn(a, b):
    return lax.dot_general(a, b, (((0,), (0,)), ((), ())), preferred_element_type=F32)


def _exact_dot(m_bf16, x):
    hi = x.astype(BF16)
    r1 = x - hi.astype(F32)
    mid = r1.astype(BF16)
    lo = (r1 - mid.astype(F32)).astype(BF16)
    return _dot(m_bf16, hi) + _dot(m_bf16, mid) + _dot(m_bf16, lo)


def _layer_norm(x, w, b):
    mu = jnp.mean(x, axis=-1, keepdims=True)
    xc = x - mu
    var = jnp.mean(xc * xc, axis=-1, keepdims=True)
    return xc * lax.rsqrt(var + LN_EPS) * w + b


def _sigmoid(x):
    return 1.0 / (1.0 + jnp.exp(-x))


def _silu(x):
    return x * _sigmoid(x)


def _lower_bound(hlb):
    m = jnp.max(hlb, axis=0, keepdims=True)
    e = jnp.exp(hlb - m)
    return e[0:1] / jnp.sum(e, axis=0, keepdims=True)


def _lane_lo(shape):
    return lax.broadcasted_iota(jnp.int32, shape, len(shape) - 1) < SW_HEAD_DIM


def _dup_heads(kv):
    rolled = pltpu.roll(kv, SW_HEAD_DIM, axis=1)
    lo = _lane_lo(kv.shape)
    return jnp.where(lo, kv, rolled), jnp.where(lo, rolled, kv)


def _sink_softmax_pv(s, vv, sinks_ref, g, rows):
    ps, dens = [], []
    for hh in range(SW_GROUP):
        sh = s[hh * rows:(hh + 1) * rows]
        sink = sinks_ref[g * SW_GROUP + hh]
        m = jnp.maximum(jnp.max(sh, axis=-1, keepdims=True), sink)
        p = jnp.exp(sh - m)
        dens.append(jnp.sum(p, axis=-1, keepdims=True) + jnp.exp(sink - m))
        ps.append(p)
    o = _dot(jnp.concatenate(ps, axis=0).astype(BF16), vv)
    o = o / jnp.concatenate(dens, axis=0)
    lo = _lane_lo((rows, LANES))
    return (jnp.where(lo, o[0:rows], o[rows:2 * rows]),
            jnp.where(lo, o[2 * rows:3 * rows], o[3 * rows:4 * rows]))


def _stack_group_queries(q, g):
    qa = q[:, 2 * g * LANES:(2 * g + 1) * LANES]
    qb = q[:, (2 * g + 1) * LANES:(2 * g + 2) * LANES]
    lo = _lane_lo(qa.shape)
    z = jnp.zeros_like(qa)
    return jnp.concatenate([jnp.where(lo, qa, z), jnp.where(lo, z, qa),
                            jnp.where(lo, qb, z), jnp.where(lo, z, qb)], axis=0).astype(BF16)


def _hgrn_diag(q, k, v, b, n):
    row = lax.broadcasted_iota(jnp.int32, (n, HG_DIM), 0)
    o = jnp.zeros((n, HG_DIM), F32)
    for s in range(n):
        t0 = (s // 8) * 8
        rr = row[t0:]
        e = jnp.exp(jnp.where(rr >= s, b[t0:] - b[s:s + 1], -jnp.inf))
        col = jnp.sum(q[t0:] * e * k[s:s + 1], axis=-1, keepdims=True)
        upd = col * v[s:s + 1]
        o = o + (upd if t0 == 0 else jnp.concatenate([jnp.zeros((t0, HG_DIM), F32), upd], axis=0))
    return o


def _col_bcast(rowvec):
    return jnp.transpose(jnp.broadcast_to(rowvec, (HG_DIM, HG_DIM)))


def _hgrn_chunk_exact(q, k, v, b, st):
    c = HG_CHUNK
    stb = st.astype(BF16)
    o = _dot_nt((q * jnp.exp(b)).astype(BF16), stb)
    b_last = b[c - 1:c]
    kd = (k * jnp.exp(b_last - b)).astype(BF16)
    vb = v.astype(BF16)
    s_new = st * jnp.exp(b_last) + _dot_tn(vb, kd)
    parts = []
    for i in range(c // HG_SUB):
        r0 = i * HG_SUB
        od = _hgrn_diag(q[r0:r0 + HG_SUB], k[r0:r0 + HG_SUB], v[r0:r0 + HG_SUB], b[r0:r0 + HG_SUB], HG_SUB)
        if i > 0:
            g_i = b[r0 - 1:r0]
            qi = (q[r0:r0 + HG_SUB] * jnp.exp(b[r0:r0 + HG_SUB] - g_i)).astype(BF16)
            ki = (k[0:r0] * jnp.exp(g_i - b[0:r0])).astype(BF16)
            a = _dot_nt(qi, ki)
            od = od + _dot(a.astype(BF16), vb[0:r0])
        parts.append(od)
    return o + jnp.concatenate(parts, axis=0), s_new


def _rms_gate(o, gate, nw):
    outs = []
    for h in range(HG_HEADS):
        oh = o[:, h * HG_DIM:(h + 1) * HG_DIM]
        ms = jnp.mean(oh * oh, axis=-1, keepdims=True)
        outs.append(oh * lax.rsqrt(ms + RMS_EPS) * nw * gate[:, h * HG_DIM:(h + 1) * HG_DIM])
    return jnp.concatenate(outs, axis=-1)


def _branch_gate(xb, j, w_in_ref):
    c0 = OFF_GATE + j * D_MODEL
    return _sigmoid(_dot(xb, w_in_ref[:, c0:c0 + D_MODEL]))


def _out_ffn(xn, mix, w_o_ref, w_f1_ref, w_f2_ref, lnp_ref):
    h = _layer_norm(ALPHA * xn + _dot(mix.astype(BF16), w_o_ref[...]), lnp_ref[2:3], lnp_ref[3:4])
    hb = h.astype(BF16)
    acc = None
    for c0 in range(0, D_FF, FFN_CHUNK):
        c1 = min(c0 + FFN_CHUNK, D_FF)
        gt = _dot(hb, w_f1_ref[:, c0:c1])
        up = _dot(hb, w_f1_ref[:, D_FF + c0:D_FF + c1])
        part = _dot((_silu(gt) * up).astype(BF16), w_f2_ref[c0:c1, :])
        acc = part if acc is None else acc + part
    return _layer_norm(ALPHA * h + acc, lnp_ref[4:5], lnp_ref[5:6])


def _hgrn_inputs(hp, lb):
    f = lb + (1.0 - lb) * _sigmoid(hp[:, HG_W:2 * HG_W])
    return (_silu(hp[:, 0:HG_W]), 1.0 - f, hp[:, 2 * HG_W:3 * HG_W], jnp.log(f), _silu(hp[:, 3 * HG_W:4 * HG_W]))


def _prompt_bias():
    i = np.arange(WINDOW)[:, None]
    j = np.arange(2 * WINDOW)[None, :]
    dist = i + WINDOW - j
    valid = (dist >= 0) & (dist < WINDOW)
    out = np.zeros((SW_KV_HEADS, SW_GROUP, WINDOW, 2 * WINDOW), np.float32)
    for h in range(SW_HEADS):
        slope = np.float32(2.0) ** np.float32(-8.0 * (h + 1) / SW_HEADS)
        out[h // SW_GROUP, h % SW_GROUP] = np.where(valid, -slope * dist.astype(np.float32), np.float32(NEG))
    return out.reshape(SW_KV_HEADS, SW_GROUP * WINDOW, 2 * WINDOW)


def _prompt_kernel(sinks_ref, x_ref, mk_ref, mv_ref, bias_ref, tri_ref, lnp_ref, hlb_ref, hnw_ref,
                   w_in_ref, w_up_ref, w_o_ref, w_f1_ref, w_f2_ref,
                   y_ref, nk_ref, nv_ref, s_ref,
                   kd_ref, vd_ref, st_ref, sto_ref, hq_ref, hk_ref, hv_ref, hb_ref, oh_ref, *, tm, nt):
    t = pl.program_id(1)

    @pl.when(t == 0)
    def _():
        kd_ref[:, 0:WINDOW, :] = jnp.zeros((SW_KV_HEADS, WINDOW, LANES), BF16)
        vd_ref[:, 0:WINDOW, :] = jnp.zeros((SW_KV_HEADS, WINDOW, LANES), BF16)
        st_ref[...] = jnp.zeros_like(st_ref)

    xn = _layer_norm(x_ref[0], lnp_ref[0:1], lnp_ref[1:2])
    xb = xn.astype(BF16)

    qkv = _dot(xb, w_in_ref[:, OFF_SW:OFF_HG])
    q = qkv[:, 0:SW_Q] * (SW_HEAD_DIM ** -0.5)
    k = qkv[:, SW_Q:SW_Q + SW_KV]
    v = qkv[:, SW_Q + SW_KV:SW_Q + 2 * SW_KV]

    nk_ref[0] = k[tm - WINDOW:tm]
    nv_ref[0] = v[tm - WINDOW:tm]

    for g, dup in enumerate(_dup_heads(k)):
        kd_ref[g, WINDOW:WINDOW + tm, :] = dup.astype(BF16)
    for g, dup in enumerate(_dup_heads(v)):
        vd_ref[g, WINDOW:WINDOW + tm, :] = dup.astype(BF16)

    col = lax.broadcasted_iota(jnp.int32, (1, 2 * WINDOW), 1)
    first = jnp.where(jnp.logical_and(t == 0, col < WINDOW), NEG, 0.0).astype(F32)
    def proj(c0, width):
        return _dot(xb, w_in_ref[:, c0:c0 + width])

    hp_parts = []
    o_rows = []
    for j in range(tm // WINDOW):
        qj = q[j * WINDOW:(j + 1) * WINDOW]
        blocks = []
        for g in range(SW_KV_HEADS):
            kk = kd_ref[g, j * WINDOW:(j + 2) * WINDOW, :]
            vv = vd_ref[g, j * WINDOW:(j + 2) * WINDOW, :]
            s = _dot_nt(_stack_group_queries(qj, g), kk) + bias_ref[g]
            if j == 0:
                s = s + first
            if len(hp_parts) < 4:
                hp_parts.append(proj(OFF_HG + len(hp_parts) * HG_W, HG_W))
            blocks.extend(_sink_softmax_pv(s, vv, sinks_ref, g, WINDOW))
        o_rows.append(jnp.concatenate(blocks, axis=-1))
    o_sw = jnp.concatenate(o_rows, axis=0)
    while len(hp_parts) < 4:
        hp_parts.append(proj(OFF_HG + len(hp_parts) * HG_W, HG_W))
    kd_ref[:, 0:WINDOW, :] = kd_ref[:, tm:tm + WINDOW, :]
    vd_ref[:, 0:WINDOW, :] = vd_ref[:, tm:tm + WINDOW, :]

    hq, hk, hv, hl, hgate = _hgrn_inputs(jnp.concatenate(hp_parts, axis=-1), _lower_bound(hlb_ref[...]))
    hb = _exact_dot(tri_ref[...], hl)
    mq = proj(OFF_MX, MX_W)
    gate_sw = _branch_gate(xb, 0, w_in_ref)
    sto_ref[...] = st_ref[...]
    qe = (hq * jnp.exp(hb)).astype(BF16)
    ke = (hk * jnp.exp(-hb)).astype(BF16)
    vb = hv.astype(BF16)
    ri = lax.broadcasted_iota(jnp.int32, (HG_CHUNK, HG_CHUNK), 0)
    ci = lax.broadcasted_iota(jnp.int32, (HG_CHUNK, HG_CHUNK), 1)
    causal = ri >= ci
    n_chunks = tm // HG_CHUNK
    heads = [slice(h * HG_DIM, (h + 1) * HG_DIM) for h in range(HG_HEADS)]
    chunks = [slice(c * HG_CHUNK, (c + 1) * HG_CHUNK) for c in range(n_chunks)]
    a_mat, upd, dec = {}, {}, []
    for c, rows in enumerate(chunks):
        b_last = hb[(c + 1) * HG_CHUNK - 1:(c + 1) * HG_CHUNK]
        kd = (hk[rows] * jnp.exp(b_last - hb[rows])).astype(BF16)
        dec.append(jnp.exp(b_last))
        for h, sl in enumerate(heads):
            a_mat[c, h] = jnp.where(causal, _dot_nt(qe[rows, sl], ke[rows, sl]), 0.0).astype(BF16)
            upd[c, h] = _dot_tn(vb[rows, sl], kd[:, sl])
    gate_mx = _branch_gate(xb, 2, w_in_ref)
    st = {(0, h): st_ref[h] for h in range(HG_HEADS)}
    for c in range(n_chunks):
        for h, sl in enumerate(heads):
            st[c + 1, h] = st[c, h] * dec[c][:, sl] + upd[c, h]
    for h in range(HG_HEADS):
        st_ref[h] = st[n_chunks, h]
    for c, rows in enumerate(chunks):
        for h, sl in enumerate(heads):
            oh_ref[rows, sl] = _dot_nt(qe[rows, sl], st[c, h].astype(BF16)) + _dot(a_mat[c, h], vb[rows, sl])
    up_sw = _dot(o_sw.astype(BF16), w_up_ref[0])

    fillers = [lambda: proj(OFF_GATE + D_MODEL, D_MODEL // 2), lambda: proj(OFF_GATE + D_MODEL + D_MODEL // 2, D_MODEL // 2)]
    filled = []
    mk = mk_ref[0]
    mv = mv_ref[0]
    outs = []
    for h in range(MX_HEADS):
        sl = slice(h * MX_HEAD_DIM, (h + 1) * MX_HEAD_DIM)
        s = _dot_nt(mq[:, sl].astype(BF16), mk[:, sl]) * (MX_HEAD_DIM ** -0.5)
        if h % 2 == 0:
            filled.append(fillers[h // 2]())
        p = jnp.exp(s - jnp.max(s, axis=-1, keepdims=True))
        den = jnp.sum(p, axis=-1, keepdims=True)
        outs.append(_dot(p.astype(BF16), mv[:, sl]) / den)
    o_m = jnp.concatenate(outs, axis=-1)
    gate_h = _sigmoid(jnp.concatenate(filled, axis=-1))
    mix = gate_sw * up_sw + gate_mx * _dot(o_m.astype(BF16), w_up_ref[2])

    @pl.when(jnp.min(hb) < -HG_SAFE_RANGE)
    def _():
        hq_ref[...] = hq
        hk_ref[...] = hk
        hv_ref[...] = hv
        hb_ref[...] = hb

        def chunk_body(c, carry):
            rows = pl.ds(pl.multiple_of(c * HG_CHUNK, HG_CHUNK), HG_CHUNK)
            for h in range(HG_HEADS):
                sl = slice(h * HG_DIM, (h + 1) * HG_DIM)
                o, st_new = _hgrn_chunk_exact(hq_ref[rows, sl], hk_ref[rows, sl], hv_ref[rows, sl], hb_ref[rows, sl],
                                              sto_ref[h])
                sto_ref[h] = st_new
                oh_ref[rows, sl] = o
            return carry

        lax.fori_loop(0, tm // HG_CHUNK, chunk_body, 0)
        st_ref[...] = sto_ref[...]

    mix = mix + gate_h * _dot(_rms_gate(oh_ref[...], hgate, hnw_ref[...]).astype(BF16), w_up_ref[1])
    y_ref[0] = _out_ffn(xn, mix, w_o_ref, w_f1_ref, w_f2_ref, lnp_ref)

    @pl.when(t == nt - 1)
    def _():
        for h in range(HG_HEADS):
            s_ref[0, h] = jnp.transpose(st_ref[h])


def _full(shape):
    nd = len(shape)
    return pl.BlockSpec(shape, lambda *_: (0,) * nd, pipeline_mode=pl.Buffered(1))


def _prompt_layer(x, mk_b, mv_b, sinks, lnp, hlb, hnw, w_in, w_up, w_o, w_f1, w_f2, *, tm):
    bsz, seq, _ = x.shape
    nt = seq // tm
    assert seq % tm == 0 and tm % WINDOW == 0 and tm % HG_CHUNK == 0
    bias = jnp.asarray(_prompt_bias())
    r = np.arange(tm)
    tri = jnp.asarray((r[:, None] // HG_CHUNK == r[None, :] // HG_CHUNK) & (r[:, None] >= r[None, :]), BF16)
    kern = functools.partial(_prompt_kernel, tm=tm, nt=nt)
    return pl.pallas_call(
        kern,
        grid=(bsz, nt),
        in_specs=[
            pl.BlockSpec(memory_space=pltpu.SMEM),
            pl.BlockSpec((1, tm, D_MODEL), lambda b, t: (b, t, 0)),
            pl.BlockSpec((1, MEM_LEN, MX_W), lambda b, t: (b, 0, 0)),
            pl.BlockSpec((1, MEM_LEN, MX_W), lambda b, t: (b, 0, 0)),
            _full(bias.shape), _full(tri.shape), _full(lnp.shape), _full(hlb.shape), _full(hnw.shape),
            _full(w_in.shape), _full(w_up.shape), _full(w_o.shape), _full(w_f1.shape), _full(w_f2.shape),
        ],
        out_specs=[
            pl.BlockSpec((1, tm, D_MODEL), lambda b, t: (b, t, 0)),
            pl.BlockSpec((1, WINDOW, SW_KV), lambda b, t: (b, 0, 0)),
            pl.BlockSpec((1, WINDOW, SW_KV), lambda b, t: (b, 0, 0)),
            pl.BlockSpec((1, HG_HEADS, HG_DIM, HG_DIM), lambda b, t: (b, 0, 0, 0)),
        ],
        out_shape=[
            jax.ShapeDtypeStruct((bsz, seq, D_MODEL), F32),
            jax.ShapeDtypeStruct((bsz, WINDOW, SW_KV), F32),
            jax.ShapeDtypeStruct((bsz, WINDOW, SW_KV), F32),
            jax.ShapeDtypeStruct((bsz, HG_HEADS, HG_DIM, HG_DIM), F32),
        ],
        scratch_shapes=[
            pltpu.VMEM((SW_KV_HEADS, WINDOW + tm, LANES), BF16),
            pltpu.VMEM((SW_KV_HEADS, WINDOW + tm, LANES), BF16),
            pltpu.VMEM((HG_HEADS, HG_DIM, HG_DIM), F32),
            pltpu.VMEM((HG_HEADS, HG_DIM, HG_DIM), F32),
            pltpu.VMEM((tm, HG_W), F32),
            pltpu.VMEM((tm, HG_W), F32),
            pltpu.VMEM((tm, HG_W), F32),
            pltpu.VMEM((tm, HG_W), F32),
            pltpu.VMEM((tm, HG_W), F32),
        ],
        compiler_params=pltpu.CompilerParams(
            dimension_semantics=("arbitrary", "arbitrary"), vmem_limit_bytes=VMEM_LIMIT_BYTES),
        name="prompt_layer",
    )(sinks, x, mk_b, mv_b, bias, tri, lnp, hlb, hnw, w_in, w_up, w_o, w_f1, w_f2)


def _mem_kv_kernel(m_ref, w_ref, mk_ref, mv_ref, mkb_ref, mvb_ref):
    kv = _dot(m_ref[0].astype(BF16), w_ref[...])
    mk_ref[0] = kv[:, 0:MX_W]
    mv_ref[0] = kv[:, MX_W:2 * MX_W]
    mkb_ref[0] = kv[:, 0:MX_W].astype(BF16)
    mvb_ref[0] = kv[:, MX_W:2 * MX_W].astype(BF16)


def _mem_kv(mem, w_kv):
    bsz = mem.shape[0]
    blk = pl.BlockSpec((1, MEM_LEN, MX_W), lambda b: (b, 0, 0))
    return pl.pallas_call(
        _mem_kv_kernel,
        grid=(bsz,),
        in_specs=[pl.BlockSpec((1, MEM_LEN, D_MODEL), lambda b: (b, 0, 0)), _full(w_kv.shape)],
        out_specs=[blk, blk, blk, blk],
        out_shape=[jax.ShapeDtypeStruct((bsz, MEM_LEN, MX_W), F32)] * 2
        + [jax.ShapeDtypeStruct((bsz, MEM_LEN, MX_W), BF16)] * 2,
        compiler_params=pltpu.CompilerParams(dimension_semantics=("arbitrary",)),
        name="mem_kv",
    )(mem, w_kv)


def _prep_params(p):
    assert p['w_in'].shape[0] == DEPTH == 1
    zeros = jnp.zeros((D_MODEL,), F32)
    return dict(
        lnp=jnp.stack([p['ln0_w'], p['ln0_b'], p['ln1_w'][0], p['ln1_b'][0], p['ln2_w'][0], p['ln2_b'][0],
                       zeros, zeros]).astype(F32),
        sinks=p['sw_sinks'][0].astype(F32),
        hlb=p['hg_lower_bound'].astype(F32),
        hnw=p['hg_norm_w'].astype(F32),
        w_in=p['w_in'][0].astype(BF16),
        w_up=jnp.stack([p['w_up_sw'][0], p['w_up_hg'][0], p['w_up_mx'][0]]).astype(BF16),
        w_o=p['w_o'][0].astype(BF16),
        w_f1=p['w_ffn_in'][0].astype(BF16),
        w_f2=p['w_ffn_out'][0].astype(BF16),
        w_kv=p['w_mem_kv'][0].astype(BF16),
    )


def _prompt_group(x, mem, p, *, tm, pp=None):
    pp = _prep_params(p) if pp is None else pp
    mk, mv, mk_b, mv_b = _mem_kv(mem, pp['w_kv'])
    y, nk, nv, s = _prompt_layer(x, mk_b, mv_b, pp['sinks'], pp['lnp'], pp['hlb'], pp['hnw'],
                                 pp['w_in'], pp['w_up'], pp['w_o'], pp['w_f1'], pp['w_f2'], tm=tm)
    return y, nk, nv, s, mk, mv


def _sample_proj_kernel(x_ref, lnp_ref, w_in_ref, xn_ref, proj_ref):
    xn = _layer_norm(x_ref[...], lnp_ref[0:1], lnp_ref[1:2])
    xn_ref[...] = xn
    proj_ref[...] = _dot(xn.astype(BF16), w_in_ref[:, 0:OFF_GATE])


def _sample_proj(x2, lnp, w_in, *, tr):
    rows = x2.shape[0]
    return pl.pallas_call(
        _sample_proj_kernel,
        grid=(rows // tr,),
        in_specs=[pl.BlockSpec((tr, D_MODEL), lambda i: (i, 0)), _full(lnp.shape), _full(w_in.shape)],
        out_specs=[pl.BlockSpec((tr, D_MODEL), lambda i: (i, 0)), pl.BlockSpec((tr, OFF_GATE), lambda i: (i, 0))],
        out_shape=[jax.ShapeDtypeStruct((rows, D_MODEL), F32), jax.ShapeDtypeStruct((rows, OFF_GATE), F32)],
        compiler_params=pltpu.CompilerParams(
            dimension_semantics=("arbitrary",), vmem_limit_bytes=VMEM_LIMIT_BYTES),
        name="sample_proj",
    )(x2, lnp, w_in)


def _sample_bias(t_new, n_keys):
    i = np.arange(t_new)[:, None]
    j = np.arange(n_keys)[None, :]
    kpos = np.where(j < WINDOW, PAST_LEN - WINDOW + j, PAST_LEN + j - WINDOW)
    dist = PAST_LEN + i - kpos
    valid = (dist >= 0) & (dist < WINDOW) & (j < WINDOW + t_new)
    out = np.zeros((SW_KV_HEADS, SW_GROUP, t_new, n_keys), np.float32)
    for h in range(SW_HEADS):
        slope = np.float32(2.0) ** np.float32(-8.0 * (h + 1) / SW_HEADS)
        out[h // SW_GROUP, h % SW_GROUP] = np.where(valid, -slope * dist.astype(np.float32), np.float32(NEG))
    return out.reshape(SW_KV_HEADS, SW_GROUP * t_new, n_keys)


def _sample_mix_kernel(sinks_ref, proj_ref, kbuf_ref, vbuf_ref, st_ref, mk_ref, mv_ref, bias_ref, hlb_ref, hnw_ref,
                       osw_ref, oh_ref, om_ref, nk_ref, nv_ref, ns_ref,
                       kk_ref, vv_ref, *, bt, t_new):
    nkeys = 2 * WINDOW
    r = bt * t_new
    elems = [slice(i * t_new, (i + 1) * t_new) for i in range(bt)]
    heads = [slice(h * HG_DIM, (h + 1) * HG_DIM) for h in range(HG_HEADS)]
    pad_rows = BF16_SUBLANES - t_new

    def pad(a):
        return jnp.concatenate([a, jnp.zeros((pad_rows, a.shape[1]), F32)], axis=0)

    def mem_head(ref, i, h):
        return ref[i, pl.ds(h, MEM_LEN, stride=MX_HEADS), :].astype(BF16)

    hq, hk, hv, hl, hgate = _hgrn_inputs(proj_ref[:, OFF_HG:OFF_MX], _lower_bound(hlb_ref[...]))
    ri = lax.broadcasted_iota(jnp.int32, (r, r), 0)
    ci = lax.broadcasted_iota(jnp.int32, (r, r), 1)
    same = (ri // t_new) == (ci // t_new)
    tri = jnp.where(jnp.logical_and(same, ri >= ci), 1.0, 0.0).astype(BF16)
    hb = _exact_dot(tri, hl)
    qe = hq * jnp.exp(hb)
    q_all = proj_ref[:, 0:SW_Q] * (SW_HEAD_DIM ** -0.5)
    mq_all = proj_ref[:, OFF_MX:OFF_GATE]

    zeros = jnp.zeros((nkeys - WINDOW - t_new, LANES), F32)
    for i, rows in enumerate(elems):
        for buf_ref, cache_ref, c0, out_ref in ((kk_ref, kbuf_ref, SW_Q, nk_ref),
                                                (vv_ref, vbuf_ref, SW_Q + SW_KV, nv_ref)):
            buf_ref[i, 0:WINDOW, :] = cache_ref[i]
            buf_ref[i, WINDOW:WINDOW + t_new, :] = proj_ref[rows, c0:c0 + SW_KV]
            buf_ref[i, WINDOW + t_new:nkeys, :] = zeros
            out_ref[i] = buf_ref[i, t_new:t_new + WINDOW, :]

    s_att, s_mem, o_state, upd, b_last = {}, {}, {}, {}, {}
    for i, rows in enumerate(elems):
        kds = _dup_heads(kk_ref[i])
        for g in range(SW_KV_HEADS):
            s_att[i, g] = _dot_nt(_stack_group_queries(q_all[rows], g), kds[g].astype(BF16)) + bias_ref[g]
    for i, rows in enumerate(elems):
        mq = pad(mq_all[rows]).astype(BF16)
        for h, sl in enumerate(heads):
            s_mem[i, h] = _dot_nt(mq[:, sl], mem_head(mk_ref, i, h)) * (MX_HEAD_DIM ** -0.5)
    for i, rows in enumerate(elems):
        b_last[i] = hb[(i + 1) * t_new - 1:(i + 1) * t_new]
        kd = pad(hk[rows] * jnp.exp(b_last[i] - hb[rows])).astype(BF16)
        vb = pad(hv[rows]).astype(BF16)
        qb = pad(qe[rows]).astype(BF16)
        for h, sl in enumerate(heads):
            o_state[i, h] = _dot(qb[:, sl], st_ref[i, h].astype(BF16))[0:t_new]
            upd[i, h] = _dot_tn(kd[:, sl], vb[:, sl])

    p_att, den_att, p_mem, den_mem = {}, {}, {}, {}
    for i in range(bt):
        for g in range(SW_KV_HEADS):
            ps, dens = [], []
            for hh in range(SW_GROUP):
                sh = s_att[i, g][hh * t_new:(hh + 1) * t_new]
                sink = sinks_ref[g * SW_GROUP + hh]
                m = jnp.maximum(jnp.max(sh, axis=-1, keepdims=True), sink)
                p = jnp.exp(sh - m)
                dens.append(jnp.sum(p, axis=-1, keepdims=True) + jnp.exp(sink - m))
                ps.append(p)
            p_att[i, g] = jnp.concatenate(ps, axis=0).astype(BF16)
            den_att[i, g] = jnp.concatenate(dens, axis=0)
        for h in range(MX_HEADS):
            p = jnp.exp(s_mem[i, h] - jnp.max(s_mem[i, h], axis=-1, keepdims=True))
            den_mem[i, h] = jnp.sum(p, axis=-1, keepdims=True)
            p_mem[i, h] = p.astype(BF16)
    for i, rows in enumerate(elems):
        dec = jnp.exp(b_last[i])
        for h, sl in enumerate(heads):
            ns_ref[i, h] = _col_bcast(dec[:, sl]) * st_ref[i, h] + upd[i, h]
            oh_ref[rows, sl] = o_state[i, h] + _hgrn_diag(hq[rows, sl], hk[rows, sl], hv[rows, sl], hb[rows, sl],
                                                          t_new)

    lo = _lane_lo((t_new, LANES))
    for i, rows in enumerate(elems):
        vds = _dup_heads(vv_ref[i])
        blocks = []
        for g in range(SW_KV_HEADS):
            o = _dot(p_att[i, g], vds[g].astype(BF16)) / den_att[i, g]
            blocks.append(jnp.where(lo, o[0:t_new], o[t_new:2 * t_new]))
            blocks.append(jnp.where(lo, o[2 * t_new:3 * t_new], o[3 * t_new:4 * t_new]))
        osw_ref[rows, :] = jnp.concatenate(blocks, axis=-1)
    for i, rows in enumerate(elems):
        outs = [(_dot(p_mem[i, h], mem_head(mv_ref, i, h)) / den_mem[i, h])[0:t_new] for h in range(MX_HEADS)]
        om_ref[rows, :] = jnp.concatenate(outs, axis=-1)
    oh_ref[...] = _rms_gate(oh_ref[...], hgate, hnw_ref[...])


def _sample_mix(proj, kbuf, vbuf, st, mk, mv, sinks, hlb, hnw, *, bt, t_new):
    nb = kbuf.shape[0]
    r = bt * t_new
    assert nb % bt == 0 and t_new == 8 and r % BF16_SUBLANES == 0
    bias = jnp.asarray(_sample_bias(t_new, 2 * WINDOW))
    kern = functools.partial(_sample_mix_kernel, bt=bt, t_new=t_new)
    row_blk = pl.BlockSpec((r, HG_W), lambda i: (i, 0))
    kv_blk = pl.BlockSpec((bt, WINDOW, SW_KV), lambda i: (i, 0, 0))
    st_blk = pl.BlockSpec((bt, HG_HEADS, HG_DIM, HG_DIM), lambda i: (i, 0, 0, 0))
    mem_blk = pl.BlockSpec((bt, MEM_LEN * MX_HEADS, MX_HEAD_DIM), lambda i: (i, 0, 0))
    return pl.pallas_call(
        kern,
        grid=(nb // bt,),
        in_specs=[pl.BlockSpec(memory_space=pltpu.SMEM),
                  pl.BlockSpec((r, OFF_GATE), lambda i: (i, 0)),
                  kv_blk, kv_blk, st_blk, mem_blk, mem_blk,
                  _full(bias.shape), _full(hlb.shape), _full(hnw.shape)],
        out_specs=[row_blk, row_blk, row_blk, kv_blk, kv_blk, st_blk],
        out_shape=[jax.ShapeDtypeStruct((nb * t_new, HG_W), F32)] * 3
        + [jax.ShapeDtypeStruct((nb, WINDOW, SW_KV), F32)] * 2
        + [jax.ShapeDtypeStruct((nb, HG_HEADS, HG_DIM, HG_DIM), F32)],
        scratch_shapes=[pltpu.VMEM((bt, 2 * WINDOW, LANES), F32), pltpu.VMEM((bt, 2 * WINDOW, LANES), F32)],
        compiler_params=pltpu.CompilerParams(
            dimension_semantics=("arbitrary",), vmem_limit_bytes=VMEM_LIMIT_BYTES),
        name="sample_mix",
    )(sinks, proj, kbuf, vbuf, st, mk, mv, bias, hlb, hnw)


def _sample_out_kernel(xn_ref, osw_ref, oh_ref, om_ref, lnp_ref, w_in_ref, w_up_ref, w_o_ref, w_f1_ref, w_f2_ref,
                       y_ref):
    xn = xn_ref[...]
    xb = xn.astype(BF16)
    mix = None
    for j, br_ref in enumerate((osw_ref, oh_ref, om_ref)):
        term = _branch_gate(xb, j, w_in_ref) * _dot(br_ref[...].astype(BF16), w_up_ref[j])
        mix = term if mix is None else mix + term
    y_ref[...] = _out_ffn(xn, mix, w_o_ref, w_f1_ref, w_f2_ref, lnp_ref)


def _sample_out(xn, osw, oh, om, lnp, w_in, w_up, w_o, w_f1, w_f2, *, tr):
    rows = xn.shape[0]
    wide = pl.BlockSpec((tr, D_MODEL), lambda i: (i, 0))
    half = pl.BlockSpec((tr, HG_W), lambda i: (i, 0))
    return pl.pallas_call(
        _sample_out_kernel,
        grid=(rows // tr,),
        in_specs=[wide, half, half, half, _full(lnp.shape), _full(w_in.shape), _full(w_up.shape),
                  _full(w_o.shape), _full(w_f1.shape), _full(w_f2.shape)],
        out_specs=wide,
        out_shape=jax.ShapeDtypeStruct((rows, D_MODEL), F32),
        compiler_params=pltpu.CompilerParams(
            dimension_semantics=("arbitrary",), vmem_limit_bytes=VMEM_LIMIT_BYTES),
        name="sample_out",
    )(xn, osw, oh, om, lnp, w_in, w_up, w_o, w_f1, w_f2)


def _sample_group(x, p, *, bt, pp=None):
    pp = _prep_params(p) if pp is None else pp
    nb, t_new, _ = x.shape
    rows = nb * t_new
    tr = min(rows, 256)
    xn, proj = _sample_proj(x.reshape(rows, D_MODEL), pp['lnp'], pp['w_in'], tr=tr)
    osw, oh, om, nk, nv, ns = _sample_mix(
        proj,
        p['cache_k_win'][0].reshape(nb, WINDOW, SW_KV), p['cache_v_win'][0].reshape(nb, WINDOW, SW_KV),
        p['state_hgrn'][0],
        p['cache_mem_k'][0].reshape(nb, MEM_LEN * MX_HEADS, MX_HEAD_DIM),
        p['cache_mem_v'][0].reshape(nb, MEM_LEN * MX_HEADS, MX_HEAD_DIM),
        pp['sinks'], pp['hlb'], pp['hnw'], bt=bt, t_new=t_new)
    y = _sample_out(xn, osw, oh, om, pp['lnp'], pp['w_in'], pp['w_up'], pp['w_o'], pp['w_f1'], pp['w_f2'], tr=tr)
    return y.reshape(nb, t_new, D_MODEL), nk, nv, ns


def kernel(x_prompt, x_sample, cache_k_win, cache_v_win, state_hgrn, cache_mem_k, cache_mem_v, mem_prompt,
           ln0_w, ln0_b, w_in, w_up_sw, w_up_hg, w_up_mx, sw_sinks, hg_lower_bound, hg_norm_w, w_mem_kv,
           w_o, ln1_w, ln1_b, w_ffn_in, w_ffn_out, ln2_w, ln2_b):
    p = dict(ln0_w=ln0_w, ln0_b=ln0_b, w_in=w_in, w_up_sw=w_up_sw, w_up_hg=w_up_hg, w_up_mx=w_up_mx,
             sw_sinks=sw_sinks, hg_lower_bound=hg_lower_bound, hg_norm_w=hg_norm_w, w_mem_kv=w_mem_kv,
             w_o=w_o, ln1_w=ln1_w, ln1_b=ln1_b, w_ffn_in=w_ffn_in, w_ffn_out=w_ffn_out, ln2_w=ln2_w, ln2_b=ln2_b)
    p.update(cache_k_win=cache_k_win, cache_v_win=cache_v_win, state_hgrn=state_hgrn,
             cache_mem_k=cache_mem_k, cache_mem_v=cache_mem_v)
    pp = _prep_params(p)
    bp, bs = x_prompt.shape[0], x_sample.shape[0]
    y, nk, nv, s, mk, mv = _prompt_group(x_prompt, mem_prompt, p, tm=PROMPT_TILE, pp=pp)
    ys, nks, nvs, ss = _sample_group(x_sample, p, bt=SAMPLE_BATCH_TILE, pp=pp)
    win = (WINDOW, SW_KV_HEADS, SW_HEAD_DIM)
    mem = (MEM_LEN, MX_HEADS, MX_HEAD_DIM)
    return (y, ys,
            nk.reshape(1, bp, *win), nv.reshape(1, bp, *win), s[None],
            mk.reshape(1, bp, *mem), mv.reshape(1, bp, *mem),
            nks.reshape(1, bs, *win), nvs.reshape(1, bs, *win), ss[None])
```

```python
import functools

import numpy as np
import jax
import jax.numpy as jnp
from jax import lax
from jax.experimental import pallas as pl
from jax.experimental.pallas import tpu as pltpu

D_MODEL = 1024
DEPTH = 1
PAST_LEN = 16384
SW_HEADS, SW_KV_HEADS, SW_HEAD_DIM = 8, 2, 64
SW_GROUP = SW_HEADS // SW_KV_HEADS
WINDOW = 128
HG_HEADS, HG_DIM = 4, 128
MEM_LEN, MX_HEADS, MX_HEAD_DIM = 256, 4, 128
SW_Q = SW_HEADS * SW_HEAD_DIM
SW_KV = SW_KV_HEADS * SW_HEAD_DIM
HG_W = HG_HEADS * HG_DIM
MX_W = MX_HEADS * MX_HEAD_DIM
D_FF = 2816
ALPHA = (2.0 * DEPTH) ** 0.25
LN_EPS = 1e-5
RMS_EPS = 1e-6
OFF_SW = 0
OFF_HG = SW_Q + 2 * SW_KV
OFF_MX = OFF_HG + 4 * HG_W
OFF_GATE = OFF_MX + MX_W
IN_WIDTH = OFF_GATE + 3 * D_MODEL

LANES = 128
BF16_SUBLANES = 16
MXU_TILE = 256
FFN_CHUNK = 4 * MXU_TILE
VMEM_LIMIT_BYTES = 60 * 1024 * 1024

NEG = -1e30
HG_CHUNK = 64
HG_SUB = 16
HG_SAFE_RANGE = 80.0
PROMPT_TILE = 256
SAMPLE_BATCH_TILE = 4
F32 = jnp.float32
BF16 = jnp.bfloat16


def _dot(a, b):
    return jnp.dot(a, b, preferred_element_type=F32)


def _dot_nt(a, b):
    return lax.dot_general(a, b, (((1,), (1,)), ((), ())), preferred_element_type=F32)


def _dot_tn(a, b):
    return lax.dot_general(a, b, (((0,), (0,)), ((), ())), preferred_element_type=F32)


def _exact_dot(m_bf16, x):
    hi = x.astype(BF16)
    r1 = x - hi.astype(F32)
    mid = r1.astype(BF16)
    lo = (r1 - mid.astype(F32)).astype(BF16)
    return _dot(m_bf16, hi) + _dot(m_bf16, mid) + _dot(m_bf16, lo)


def _layer_norm(x, w, b):
    mu = jnp.mean(x, axis=-1, keepdims=True)
    xc = x - mu
    var = jnp.mean(xc * xc, axis=-1, keepdims=True)
    return xc * lax.rsqrt(var + LN_EPS) * w + b


def _sigmoid(x):
    return 1.0 / (1.0 + jnp.exp(-x))


def _silu(x):
    return x * _sigmoid(x)


def _lower_bound(hlb):
    m = jnp.max(hlb, axis=0, keepdims=True)
    e = jnp.exp(hlb - m)
    return e[0:1] / jnp.sum(e, axis=0, keepdims=True)


def _lane_lo(shape):
    return lax.broadcasted_iota(jnp.int32, shape, len(shape) - 1) < SW_HEAD_DIM


def _dup_heads(kv):
    rolled = pltpu.roll(kv, SW_HEAD_DIM, axis=1)
    lo = _lane_lo(kv.shape)
    return jnp.where(lo, kv, rolled), jnp.where(lo, rolled, kv)


def _sink_softmax_pv(s, vv, sinks_ref, g, rows):
    ps, dens = [], []
    for hh in range(SW_GROUP):
        sh = s[hh * rows:(hh + 1) * rows]
        sink = sinks_ref[g * SW_GROUP + hh]
        m = jnp.maximum(jnp.max(sh, axis=-1, keepdims=True), sink)
        p = jnp.exp(sh - m)
        dens.append(jnp.sum(p, axis=-1, keepdims=True) + jnp.exp(sink - m))
        ps.append(p)
    o = _dot(jnp.concatenate(ps, axis=0).astype(BF16), vv)
    o = o / jnp.concatenate(dens, axis=0)
    lo = _lane_lo((rows, LANES))
    return (jnp.where(lo, o[0:rows], o[rows:2 * rows]),
            jnp.where(lo, o[2 * rows:3 * rows], o[3 * rows:4 * rows]))


def _stack_group_queries(q, g):
    qa = q[:, 2 * g * LANES:(2 * g + 1) * LANES]
    qb = q[:, (2 * g + 1) * LANES:(2 * g + 2) * LANES]
    lo = _lane_lo(qa.shape)
    z = jnp.zeros_like(qa)
    return jnp.concatenate([jnp.where(lo, qa, z), jnp.where(lo, z, qa),
                            jnp.where(lo, qb, z), jnp.where(lo, z, qb)], axis=0).astype(BF16)


def _hgrn_diag(q, k, v, b, n):
    row = lax.broadcasted_iota(jnp.int32, (n, HG_DIM), 0)
    o = jnp.zeros((n, HG_DIM), F32)
    for s in range(n):
        t0 = (s // 8) * 8
        rr = row[t0:]
        e = jnp.exp(jnp.where(rr >= s, b[t0:] - b[s:s + 1], -jnp.inf))
        col = jnp.sum(q[t0:] * e * k[s:s + 1], axis=-1, keepdims=True)
        upd = col * v[s:s + 1]
        o = o + (upd if t0 == 0 else jnp.concatenate([jnp.zeros((t0, HG_DIM), F32), upd], axis=0))
    return o


def _col_bcast(rowvec):
    return jnp.transpose(jnp.broadcast_to(rowvec, (HG_DIM, HG_DIM)))


def _hgrn_chunk_exact(q, k, v, b, st):
    c = HG_CHUNK
    stb = st.astype(BF16)
    o = _dot_nt((q * jnp.exp(b)).astype(BF16), stb)
    b_last = b[c - 1:c]
    kd = (k * jnp.exp(b_last - b)).astype(BF16)
    vb = v.astype(BF16)
    s_new = st * jnp.exp(b_last) + _dot_tn(vb, kd)
    parts = []
    for i in range(c // HG_SUB):
        r0 = i * HG_SUB
        od = _hgrn_diag(q[r0:r0 + HG_SUB], k[r0:r0 + HG_SUB], v[r0:r0 + HG_SUB], b[r0:r0 + HG_SUB], HG_SUB)
        if i > 0:
            g_i = b[r0 - 1:r0]
            qi = (q[r0:r0 + HG_SUB] * jnp.exp(b[r0:r0 + HG_SUB] - g_i)).astype(BF16)
            ki = (k[0:r0] * jnp.exp(g_i - b[0:r0])).astype(BF16)
            a = _dot_nt(qi, ki)
            od = od + _dot(a.astype(BF16), vb[0:r0])
        parts.append(od)
    return o + jnp.concatenate(parts, axis=0), s_new


def _rms_gate(o, gate, nw):
    outs = []
    for h in range(HG_HEADS):
        oh = o[:, h * HG_DIM:(h + 1) * HG_DIM]
        ms = jnp.mean(oh * oh, axis=-1, keepdims=True)
        outs.append(oh * lax.rsqrt(ms + RMS_EPS) * nw * gate[:, h * HG_DIM:(h + 1) * HG_DIM])
    return jnp.concatenate(outs, axis=-1)


def _branch_gate(xb, j, w_in_ref):
    c0 = OFF_GATE + j * D_MODEL
    return _sigmoid(_dot(xb, w_in_ref[:, c0:c0 + D_MODEL]))


def _out_ffn(xn, mix, w_o_ref, w_f1_ref, w_f2_ref, lnp_ref):
    h = _layer_norm(ALPHA * xn + _dot(mix.astype(BF16), w_o_ref[...]), lnp_ref[2:3], lnp_ref[3:4])
    hb = h.astype(BF16)
    acc = None
    for c0 in range(0, D_FF, FFN_CHUNK):
        c1 = min(c0 + FFN_CHUNK, D_FF)
        gt = _dot(hb, w_f1_ref[:, c0:c1])
        up = _dot(hb, w_f1_ref[:, D_FF + c0:D_FF + c1])
        part = _dot((_silu(gt) * up).astype(BF16), w_f2_ref[c0:c1, :])
        acc = part if acc is None else acc + part
    return _layer_norm(ALPHA * h + acc, lnp_ref[4:5], lnp_ref[5:6])


def _hgrn_inputs(hp, lb):
    f = lb + (1.0 - lb) * _sigmoid(hp[:, HG_W:2 * HG_W])
    return (_silu(hp[:, 0:HG_W]), 1.0 - f, hp[:, 2 * HG_W:3 * HG_W], jnp.log(f), _silu(hp[:, 3 * HG_W:4 * HG_W]))


def _prompt_bias():
    i = np.arange(WINDOW)[:, None]
    j = np.arange(2 * WINDOW)[None, :]
    dist = i + WINDOW - j
    valid = (dist >= 0) & (dist < WINDOW)
    out = np.zeros((SW_KV_HEADS, SW_GROUP, WINDOW, 2 * WINDOW), np.float32)
    for h in range(SW_HEADS):
        slope = np.float32(2.0) ** np.float32(-8.0 * (h + 1) / SW_HEADS)
        out[h // SW_GROUP, h % SW_GROUP] = np.where(valid, -slope * dist.astype(np.float32), np.float32(NEG))
    return out.reshape(SW_KV_HEADS, SW_GROUP * WINDOW, 2 * WINDOW)


def _prompt_kernel(sinks_ref, x_ref, mk_ref, mv_ref, bias_ref, tri_ref, lnp_ref, hlb_ref, hnw_ref,
                   w_in_ref, w_up_ref, w_o_ref, w_f1_ref, w_f2_ref,
                   y_ref, nk_ref, nv_ref, s_ref,
                   kd_ref, vd_ref, st_ref, sto_ref, hq_ref, hk_ref, hv_ref, hb_ref, oh_ref, *, tm, nt):
    t = pl.program_id(1)

    @pl.when(t == 0)
    def _():
        kd_ref[:, 0:WINDOW, :] = jnp.zeros((SW_KV_HEADS, WINDOW, LANES), BF16)
        vd_ref[:, 0:WINDOW, :] = jnp.zeros((SW_KV_HEADS, WINDOW, LANES), BF16)
        st_ref[...] = jnp.zeros_like(st_ref)

    half = tm // 2
    xn_h = [_layer_norm(x_ref[0, r0:r0 + half, :], lnp_ref[0:1], lnp_ref[1:2]) for r0 in (0, half)]
    qkv = jnp.concatenate([_dot(xh.astype(BF16), w_in_ref[:, OFF_SW:OFF_HG]) for xh in xn_h], axis=0)
    xn = jnp.concatenate(xn_h, axis=0)
    xb = xn.astype(BF16)
    q = qkv[:, 0:SW_Q] * (SW_HEAD_DIM ** -0.5)
    k = qkv[:, SW_Q:SW_Q + SW_KV]
    v = qkv[:, SW_Q + SW_KV:SW_Q + 2 * SW_KV]

    nk_ref[0] = k[tm - WINDOW:tm]
    nv_ref[0] = v[tm - WINDOW:tm]

    for g, dup in enumerate(_dup_heads(k)):
        kd_ref[g, WINDOW:WINDOW + tm, :] = dup.astype(BF16)
    for g, dup in enumerate(_dup_heads(v)):
        vd_ref[g, WINDOW:WINDOW + tm, :] = dup.astype(BF16)

    col = lax.broadcasted_iota(jnp.int32, (1, 2 * WINDOW), 1)
    first = jnp.where(jnp.logical_and(t == 0, col < WINDOW), NEG, 0.0).astype(F32)
    def proj(c0, width):
        return _dot(xb, w_in_ref[:, c0:c0 + width])

    hp_parts = []
    o_rows = []
    for j in range(tm // WINDOW):
        qj = q[j * WINDOW:(j + 1) * WINDOW]
        blocks = []
        for g in range(SW_KV_HEADS):
            kk = kd_ref[g, j * WINDOW:(j + 2) * WINDOW, :]
            vv = vd_ref[g, j * WINDOW:(j + 2) * WINDOW, :]
            s = _dot_nt(_stack_group_queries(qj, g), kk) + bias_ref[g]
            if j == 0:
                s = s + first
            if len(hp_parts) < 4:
                hp_parts.append(proj(OFF_HG + len(hp_parts) * HG_W, HG_W))
            blocks.extend(_sink_softmax_pv(s, vv, sinks_ref, g, WINDOW))
        o_rows.append(jnp.concatenate(blocks, axis=-1))
    o_sw = jnp.concatenate(o_rows, axis=0)
    while len(hp_parts) < 4:
        hp_parts.append(proj(OFF_HG + len(hp_parts) * HG_W, HG_W))
    kd_ref[:, 0:WINDOW, :] = kd_ref[:, tm:tm + WINDOW, :]
    vd_ref[:, 0:WINDOW, :] = vd_ref[:, tm:tm + WINDOW, :]

    hq, hk, hv, hl, hgate = _hgrn_inputs(jnp.concatenate(hp_parts, axis=-1), _lower_bound(hlb_ref[...]))
    hb = _exact_dot(tri_ref[...], hl)
    mq = proj(OFF_MX, MX_W)
    gate_sw = _branch_gate(xb, 0, w_in_ref)
    sto_ref[...] = st_ref[...]
    qe = (hq * jnp.exp(hb)).astype(BF16)
    ke = (hk * jnp.exp(-hb)).astype(BF16)
    vb = hv.astype(BF16)
    ri = lax.broadcasted_iota(jnp.int32, (HG_CHUNK, HG_CHUNK), 0)
    ci = lax.broadcasted_iota(jnp.int32, (HG_CHUNK, HG_CHUNK), 1)
    causal = ri >= ci
    n_chunks = tm // HG_CHUNK
    heads = [slice(h * HG_DIM, (h + 1) * HG_DIM) for h in range(HG_HEADS)]
    chunks = [slice(c * HG_CHUNK, (c + 1) * HG_CHUNK) for c in range(n_chunks)]
    a_mat, upd, dec = {}, {}, []
    for c, rows in enumerate(chunks):
        b_last = hb[(c + 1) * HG_CHUNK - 1:(c + 1) * HG_CHUNK]
        kd = (hk[rows] * jnp.exp(b_last - hb[rows])).astype(BF16)
        dec.append(jnp.exp(b_last))
        for h, sl in enumerate(heads):
            a_mat[c, h] = jnp.where(causal, _dot_nt(qe[rows, sl], ke[rows, sl]), 0.0).astype(BF16)
            upd[c, h] = _dot_tn(vb[rows, sl], kd[:, sl])
    gate_mx = _branch_gate(xb, 2, w_in_ref)
    st = {(0, h): st_ref[h] for h in range(HG_HEADS)}
    for c in range(n_chunks):
        for h, sl in enumerate(heads):
            st[c + 1, h] = st[c, h] * dec[c][:, sl] + upd[c, h]
    for h in range(HG_HEADS):
        st_ref[h] = st[n_chunks, h]
    for c, rows in enumerate(chunks):
        for h, sl in enumerate(heads):
            oh_ref[rows, sl] = _dot_nt(qe[rows, sl], st[c, h].astype(BF16)) + _dot(a_mat[c, h], vb[rows, sl])
    up_sw = _dot(o_sw.astype(BF16), w_up_ref[0])

    fillers = [lambda: proj(OFF_GATE + D_MODEL, D_MODEL // 2), lambda: proj(OFF_GATE + D_MODEL + D_MODEL // 2, D_MODEL // 2)]
    filled = []
    mk = mk_ref[0]
    mv = mv_ref[0]
    outs = []
    for h in range(MX_HEADS):
        sl = slice(h * MX_HEAD_DIM, (h + 1) * MX_HEAD_DIM)
        s = _dot_nt(mq[:, sl].astype(BF16), mk[:, sl]) * (MX_HEAD_DIM ** -0.5)
        if h % 2 == 0:
            filled.append(fillers[h // 2]())
        p = jnp.exp(s - jnp.max(s, axis=-1, keepdims=True))
        den = jnp.sum(p, axis=-1, keepdims=True)
        outs.append(_dot(p.astype(BF16), mv[:, sl]) / den)
    o_m = jnp.concatenate(outs, axis=-1)
    gate_h = _sigmoid(jnp.concatenate(filled, axis=-1))
    mix = gate_sw * up_sw + gate_mx * _dot(o_m.astype(BF16), w_up_ref[2])

    @pl.when(jnp.min(hb) < -HG_SAFE_RANGE)
    def _():
        hq_ref[...] = hq
        hk_ref[...] = hk
        hv_ref[...] = hv
        hb_ref[...] = hb

        def chunk_body(c, carry):
            rows = pl.ds(pl.multiple_of(c * HG_CHUNK, HG_CHUNK), HG_CHUNK)
            for h in range(HG_HEADS):
                sl = slice(h * HG_DIM, (h + 1) * HG_DIM)
                o, st_new = _hgrn_chunk_exact(hq_ref[rows, sl], hk_ref[rows, sl], hv_ref[rows, sl], hb_ref[rows, sl],
                                              sto_ref[h])
                sto_ref[h] = st_new
                oh_ref[rows, sl] = o
            return carry

        lax.fori_loop(0, tm // HG_CHUNK, chunk_body, 0)
        st_ref[...] = sto_ref[...]

    mix = mix + gate_h * _dot(_rms_gate(oh_ref[...], hgate, hnw_ref[...]).astype(BF16), w_up_ref[1])
    y_ref[0] = _out_ffn(xn, mix, w_o_ref, w_f1_ref, w_f2_ref, lnp_ref)

    @pl.when(t == nt - 1)
    def _():
        for h in range(HG_HEADS):
            s_ref[0, h] = jnp.transpose(st_ref[h])


def _full(shape):
    nd = len(shape)
    return pl.BlockSpec(shape, lambda *_: (0,) * nd, pipeline_mode=pl.Buffered(1))


def _prompt_layer(x, mk_b, mv_b, sinks, lnp, hlb, hnw, w_in, w_up, w_o, w_f1, w_f2, *, tm):
    bsz, seq, _ = x.shape
    nt = seq // tm
    assert seq % tm == 0 and tm % WINDOW == 0 and tm % HG_CHUNK == 0
    bias = jnp.asarray(_prompt_bias())
    r = np.arange(tm)
    tri = jnp.asarray((r[:, None] // HG_CHUNK == r[None, :] // HG_CHUNK) & (r[:, None] >= r[None, :]), BF16)
    kern = functools.partial(_prompt_kernel, tm=tm, nt=nt)
    return pl.pallas_call(
        kern,
        grid=(bsz, nt),
        in_specs=[
            pl.BlockSpec(memory_space=pltpu.SMEM),
            pl.BlockSpec((1, tm, D_MODEL), lambda b, t: (b, t, 0)),
            pl.BlockSpec((1, MEM_LEN, MX_W), lambda b, t: (b, 0, 0)),
            pl.BlockSpec((1, MEM_LEN, MX_W), lambda b, t: (b, 0, 0)),
            _full(bias.shape), _full(tri.shape), _full(lnp.shape), _full(hlb.shape), _full(hnw.shape),
            _full(w_in.shape), _full(w_up.shape), _full(w_o.shape), _full(w_f1.shape), _full(w_f2.shape),
        ],
        out_specs=[
            pl.BlockSpec((1, tm, D_MODEL), lambda b, t: (b, t, 0)),
            pl.BlockSpec((1, WINDOW, SW_KV), lambda b, t: (b, 0, 0)),
            pl.BlockSpec((1, WINDOW, SW_KV), lambda b, t: (b, 0, 0)),
            pl.BlockSpec((1, HG_HEADS, HG_DIM, HG_DIM), lambda b, t: (b, 0, 0, 0)),
        ],
        out_shape=[
            jax.ShapeDtypeStruct((bsz, seq, D_MODEL), F32),
            jax.ShapeDtypeStruct((bsz, WINDOW, SW_KV), F32),
            jax.ShapeDtypeStruct((bsz, WINDOW, SW_KV), F32),
            jax.ShapeDtypeStruct((bsz, HG_HEADS, HG_DIM, HG_DIM), F32),
        ],
        scratch_shapes=[
            pltpu.VMEM((SW_KV_HEADS, WINDOW + tm, LANES), BF16),
            pltpu.VMEM((SW_KV_HEADS, WINDOW + tm, LANES), BF16),
            pltpu.VMEM((HG_HEADS, HG_DIM, HG_DIM), F32),
            pltpu.VMEM((HG_HEADS, HG_DIM, HG_DIM), F32),
            pltpu.VMEM((tm, HG_W), F32),
            pltpu.VMEM((tm, HG_W), F32),
            pltpu.VMEM((tm, HG_W), F32),
            pltpu.VMEM((tm, HG_W), F32),
            pltpu.VMEM((tm, HG_W), F32),
        ],
        compiler_params=pltpu.CompilerParams(
            dimension_semantics=("arbitrary", "arbitrary"), vmem_limit_bytes=VMEM_LIMIT_BYTES),
        name="prompt_layer",
    )(sinks, x, mk_b, mv_b, bias, tri, lnp, hlb, hnw, w_in, w_up, w_o, w_f1, w_f2)


def _mem_kv_kernel(m_ref, w_ref, mk_ref, mv_ref, mkb_ref, mvb_ref):
    kv = _dot(m_ref[0].astype(BF16), w_ref[...])
    for h in range(MX_HEADS):
        head_rows = pl.ds(h, MEM_LEN, stride=MX_HEADS)
        mk_ref[0, head_rows, :] = kv[:, h * MX_HEAD_DIM:(h + 1) * MX_HEAD_DIM]
        mv_ref[0, head_rows, :] = kv[:, MX_W + h * MX_HEAD_DIM:MX_W + (h + 1) * MX_HEAD_DIM]
    mkb_ref[0] = kv[:, 0:MX_W].astype(BF16)
    mvb_ref[0] = kv[:, MX_W:2 * MX_W].astype(BF16)


def _mem_kv(mem, w_kv):
    bsz = mem.shape[0]
    blk = pl.BlockSpec((1, MEM_LEN, MX_W), lambda b: (b, 0, 0))
    row_blk = pl.BlockSpec((1, MEM_LEN * MX_HEADS, MX_HEAD_DIM), lambda b: (b, 0, 0))
    return pl.pallas_call(
        _mem_kv_kernel,
        grid=(bsz,),
        in_specs=[pl.BlockSpec((1, MEM_LEN, D_MODEL), lambda b: (b, 0, 0)), _full(w_kv.shape)],
        out_specs=[row_blk, row_blk, blk, blk],
        out_shape=[jax.ShapeDtypeStruct((bsz, MEM_LEN * MX_HEADS, MX_HEAD_DIM), F32)] * 2
        + [jax.ShapeDtypeStruct((bsz, MEM_LEN, MX_W), BF16)] * 2,
        compiler_params=pltpu.CompilerParams(dimension_semantics=("arbitrary",)),
        name="mem_kv",
    )(mem, w_kv)


def _prep_params(p):
    assert p['w_in'].shape[0] == DEPTH == 1
    zeros = jnp.zeros((D_MODEL,), F32)
    return dict(
        lnp=jnp.stack([p['ln0_w'], p['ln0_b'], p['ln1_w'][0], p['ln1_b'][0], p['ln2_w'][0], p['ln2_b'][0],
                       zeros, zeros]).astype(F32),
        sinks=p['sw_sinks'][0].astype(F32),
        hlb=p['hg_lower_bound'].astype(F32),
        hnw=p['hg_norm_w'].astype(F32),
        w_in=p['w_in'][0].astype(BF16),
        w_up=jnp.stack([p['w_up_sw'][0], p['w_up_hg'][0], p['w_up_mx'][0]]).astype(BF16),
        w_o=p['w_o'][0].astype(BF16),
        w_f1=p['w_ffn_in'][0].astype(BF16),
        w_f2=p['w_ffn_out'][0].astype(BF16),
        w_kv=p['w_mem_kv'][0].astype(BF16),
    )


def _prompt_group(x, mem, p, *, tm, pp=None):
    pp = _prep_params(p) if pp is None else pp
    mk, mv, mk_b, mv_b = _mem_kv(mem, pp['w_kv'])
    y, nk, nv, s = _prompt_layer(x, mk_b, mv_b, pp['sinks'], pp['lnp'], pp['hlb'], pp['hnw'],
                                 pp['w_in'], pp['w_up'], pp['w_o'], pp['w_f1'], pp['w_f2'], tm=tm)
    return y, nk, nv, s, mk, mv


def _sample_proj_kernel(x_ref, lnp_ref, w_in_ref, xn_ref, proj_ref):
    xn = _layer_norm(x_ref[...], lnp_ref[0:1], lnp_ref[1:2])
    xn_ref[...] = xn
    proj_ref[...] = _dot(xn.astype(BF16), w_in_ref[:, 0:OFF_GATE])


def _sample_proj(x2, lnp, w_in, *, tr):
    rows = x2.shape[0]
    return pl.pallas_call(
        _sample_proj_kernel,
        grid=(rows // tr,),
        in_specs=[pl.BlockSpec((tr, D_MODEL), lambda i: (i, 0)), _full(lnp.shape), _full(w_in.shape)],
        out_specs=[pl.BlockSpec((tr, D_MODEL), lambda i: (i, 0)), pl.BlockSpec((tr, OFF_GATE), lambda i: (i, 0))],
        out_shape=[jax.ShapeDtypeStruct((rows, D_MODEL), F32), jax.ShapeDtypeStruct((rows, OFF_GATE), F32)],
        compiler_params=pltpu.CompilerParams(
            dimension_semantics=("arbitrary",), vmem_limit_bytes=VMEM_LIMIT_BYTES),
        name="sample_proj",
    )(x2, lnp, w_in)


def _sample_bias(t_new, n_keys):
    i = np.arange(t_new)[:, None]
    j = np.arange(n_keys)[None, :]
    kpos = np.where(j < WINDOW, PAST_LEN - WINDOW + j, PAST_LEN + j - WINDOW)
    dist = PAST_LEN + i - kpos
    valid = (dist >= 0) & (dist < WINDOW) & (j < WINDOW + t_new)
    out = np.zeros((SW_KV_HEADS, SW_GROUP, t_new, n_keys), np.float32)
    for h in range(SW_HEADS):
        slope = np.float32(2.0) ** np.float32(-8.0 * (h + 1) / SW_HEADS)
        out[h // SW_GROUP, h % SW_GROUP] = np.where(valid, -slope * dist.astype(np.float32), np.float32(NEG))
    return out.reshape(SW_KV_HEADS, SW_GROUP * t_new, n_keys)


def _sample_mix_kernel(sinks_ref, proj_ref, kbuf_ref, vbuf_ref, st_ref, mk_ref, mv_ref, bias_ref, hlb_ref, hnw_ref,
                       osw_ref, oh_ref, om_ref, nk_ref, nv_ref, ns_ref,
                       kk_ref, vv_ref, *, bt, t_new):
    nkeys = 2 * WINDOW
    r = bt * t_new
    elems = [slice(i * t_new, (i + 1) * t_new) for i in range(bt)]
    heads = [slice(h * HG_DIM, (h + 1) * HG_DIM) for h in range(HG_HEADS)]
    pad_rows = BF16_SUBLANES - t_new

    def pad(a):
        return jnp.concatenate([a, jnp.zeros((pad_rows, a.shape[1]), F32)], axis=0)

    def mem_head(ref, i, h):
        return ref[i, pl.ds(h, MEM_LEN, stride=MX_HEADS), :].astype(BF16)

    hq, hk, hv, hl, hgate = _hgrn_inputs(proj_ref[:, OFF_HG:OFF_MX], _lower_bound(hlb_ref[...]))
    ri = lax.broadcasted_iota(jnp.int32, (r, r), 0)
    ci = lax.broadcasted_iota(jnp.int32, (r, r), 1)
    same = (ri // t_new) == (ci // t_new)
    tri = jnp.where(jnp.logical_and(same, ri >= ci), 1.0, 0.0).astype(BF16)
    hb = _exact_dot(tri, hl)
    qe = hq * jnp.exp(hb)
    q_all = proj_ref[:, 0:SW_Q] * (SW_HEAD_DIM ** -0.5)
    mq_all = proj_ref[:, OFF_MX:OFF_GATE]

    zeros = jnp.zeros((nkeys - WINDOW - t_new, LANES), F32)
    for i, rows in enumerate(elems):
        for buf_ref, cache_ref, c0, out_ref in ((kk_ref, kbuf_ref, SW_Q, nk_ref),
                                                (vv_ref, vbuf_ref, SW_Q + SW_KV, nv_ref)):
            new = _dup_heads(proj_ref[rows, c0:c0 + SW_KV])
            for g in range(SW_KV_HEADS):
                head_rows = pl.ds(g, WINDOW, stride=SW_KV_HEADS)
                old = cache_ref[i, head_rows, :]
                buf_ref[i, g, 0:WINDOW, :] = jnp.concatenate([old, old], axis=-1)
                buf_ref[i, g, WINDOW:WINDOW + t_new, :] = new[g]
                buf_ref[i, g, WINDOW + t_new:nkeys, :] = zeros
                out_ref[i, head_rows, :] = buf_ref[i, g, t_new:t_new + WINDOW, 0:SW_HEAD_DIM]

    s_att, s_mem, o_state, upd, b_last = {}, {}, {}, {}, {}
    for i, rows in enumerate(elems):
        for g in range(SW_KV_HEADS):
            s_att[i, g] = _dot_nt(_stack_group_queries(q_all[rows], g), kk_ref[i, g].astype(BF16)) + bias_ref[g]
    for i, rows in enumerate(elems):
        mq = pad(mq_all[rows]).astype(BF16)
        for h, sl in enumerate(heads):
            s_mem[i, h] = _dot_nt(mq[:, sl], mem_head(mk_ref, i, h)) * (MX_HEAD_DIM ** -0.5)
    for i, rows in enumerate(elems):
        b_last[i] = hb[(i + 1) * t_new - 1:(i + 1) * t_new]
        kd = pad(hk[rows] * jnp.exp(b_last[i] - hb[rows])).astype(BF16)
        vb = pad(hv[rows]).astype(BF16)
        qb = pad(qe[rows]).astype(BF16)
        for h, sl in enumerate(heads):
            o_state[i, h] = _dot(qb[:, sl], st_ref[i, h].astype(BF16))[0:t_new]
            upd[i, h] = _dot_tn(kd[:, sl], vb[:, sl])

    p_att, den_att, p_mem, den_mem = {}, {}, {}, {}
    for i in range(bt):
        for g in range(SW_KV_HEADS):
            ps, dens = [], []
            for hh in range(SW_GROUP):
                sh = s_att[i, g][hh * t_new:(hh + 1) * t_new]
                sink = sinks_ref[g * SW_GROUP + hh]
                m = jnp.maximum(jnp.max(sh, axis=-1, keepdims=True), sink)
                p = jnp.exp(sh - m)
                dens.append(jnp.sum(p, axis=-1, keepdims=True) + jnp.exp(sink - m))
                ps.append(p)
            p_att[i, g] = jnp.concatenate(ps, axis=0).astype(BF16)
            den_att[i, g] = jnp.concatenate(dens, axis=0)
        for h in range(MX_HEADS):
            p = jnp.exp(s_mem[i, h] - jnp.max(s_mem[i, h], axis=-1, keepdims=True))
            den_mem[i, h] = jnp.sum(p, axis=-1, keepdims=True)
            p_mem[i, h] = p.astype(BF16)
    for i, rows in enumerate(elems):
        dec = jnp.exp(b_last[i])
        for h, sl in enumerate(heads):
            ns_ref[i, h] = _col_bcast(dec[:, sl]) * st_ref[i, h] + upd[i, h]
            oh_ref[rows, sl] = o_state[i, h] + _hgrn_diag(hq[rows, sl], hk[rows, sl], hv[rows, sl], hb[rows, sl],
                                                          t_new)

    lo = _lane_lo((t_new, LANES))
    for i, rows in enumerate(elems):
        blocks = []
        for g in range(SW_KV_HEADS):
            o = _dot(p_att[i, g], vv_ref[i, g].astype(BF16)) / den_att[i, g]
            blocks.append(jnp.where(lo, o[0:t_new], o[t_new:2 * t_new]))
            blocks.append(jnp.where(lo, o[2 * t_new:3 * t_new], o[3 * t_new:4 * t_new]))
        osw_ref[rows, :] = jnp.concatenate(blocks, axis=-1)
    for i, rows in enumerate(elems):
        outs = [(_dot(p_mem[i, h], mem_head(mv_ref, i, h)) / den_mem[i, h])[0:t_new] for h in range(MX_HEADS)]
        om_ref[rows, :] = jnp.concatenate(outs, axis=-1)
    oh_ref[...] = _rms_gate(oh_ref[...], hgate, hnw_ref[...])


def _sample_mix(proj, kbuf, vbuf, st, mk, mv, sinks, hlb, hnw, *, bt, t_new):
    nb = kbuf.shape[0]
    r = bt * t_new
    assert nb % bt == 0 and t_new == 8 and r % BF16_SUBLANES == 0
    bias = jnp.asarray(_sample_bias(t_new, 2 * WINDOW))
    kern = functools.partial(_sample_mix_kernel, bt=bt, t_new=t_new)
    row_blk = pl.BlockSpec((r, HG_W), lambda i: (i, 0))
    kv_blk = pl.BlockSpec((bt, WINDOW * SW_KV_HEADS, SW_HEAD_DIM), lambda i: (i, 0, 0))
    st_blk = pl.BlockSpec((bt, HG_HEADS, HG_DIM, HG_DIM), lambda i: (i, 0, 0, 0))
    mem_blk = pl.BlockSpec((bt, MEM_LEN * MX_HEADS, MX_HEAD_DIM), lambda i: (i, 0, 0))
    return pl.pallas_call(
        kern,
        grid=(nb // bt,),
        in_specs=[pl.BlockSpec(memory_space=pltpu.SMEM),
                  pl.BlockSpec((r, OFF_GATE), lambda i: (i, 0)),
                  kv_blk, kv_blk, st_blk, mem_blk, mem_blk,
                  _full(bias.shape), _full(hlb.shape), _full(hnw.shape)],
        out_specs=[row_blk, row_blk, row_blk, kv_blk, kv_blk, st_blk],
        out_shape=[jax.ShapeDtypeStruct((nb * t_new, HG_W), F32)] * 3
        + [jax.ShapeDtypeStruct((nb, WINDOW * SW_KV_HEADS, SW_HEAD_DIM), F32)] * 2
        + [jax.ShapeDtypeStruct((nb, HG_HEADS, HG_DIM, HG_DIM), F32)],
        scratch_shapes=[pltpu.VMEM((bt, SW_KV_HEADS, 2 * WINDOW, LANES), F32)] * 2,
        compiler_params=pltpu.CompilerParams(
            dimension_semantics=("arbitrary",), vmem_limit_bytes=VMEM_LIMIT_BYTES),
        name="sample_mix",
    )(sinks, proj, kbuf, vbuf, st, mk, mv, bias, hlb, hnw)


def _sample_out_kernel(xn_ref, osw_ref, oh_ref, om_ref, lnp_ref, w_in_ref, w_up_ref, w_o_ref, w_f1_ref, w_f2_ref,
                       y_ref):
    xn = xn_ref[...]
    xb = xn.astype(BF16)
    mix = None
    for j, br_ref in enumerate((osw_ref, oh_ref, om_ref)):
        term = _branch_gate(xb, j, w_in_ref) * _dot(br_ref[...].astype(BF16), w_up_ref[j])
        mix = term if mix is None else mix + term
    y_ref[...] = _out_ffn(xn, mix, w_o_ref, w_f1_ref, w_f2_ref, lnp_ref)


def _sample_out(xn, osw, oh, om, lnp, w_in, w_up, w_o, w_f1, w_f2, *, tr):
    rows = xn.shape[0]
    wide = pl.BlockSpec((tr, D_MODEL), lambda i: (i, 0))
    half = pl.BlockSpec((tr, HG_W), lambda i: (i, 0))
    return pl.pallas_call(
        _sample_out_kernel,
        grid=(rows // tr,),
        in_specs=[wide, half, half, half, _full(lnp.shape), _full(w_in.shape), _full(w_up.shape),
                  _full(w_o.shape), _full(w_f1.shape), _full(w_f2.shape)],
        out_specs=wide,
        out_shape=jax.ShapeDtypeStruct((rows, D_MODEL), F32),
        compiler_params=pltpu.CompilerParams(
            dimension_semantics=("arbitrary",), vmem_limit_bytes=VMEM_LIMIT_BYTES),
        name="sample_out",
    )(xn, osw, oh, om, lnp, w_in, w_up, w_o, w_f1, w_f2)


def _sample_group(x, p, *, bt, pp=None):
    pp = _prep_params(p) if pp is None else pp
    nb, t_new, _ = x.shape
    rows = nb * t_new
    tr = min(rows, 256)
    xn, proj = _sample_proj(x.reshape(rows, D_MODEL), pp['lnp'], pp['w_in'], tr=tr)
    osw, oh, om, nk, nv, ns = _sample_mix(
        proj,
        p['cache_k_win'][0].reshape(nb, WINDOW * SW_KV_HEADS, SW_HEAD_DIM),
        p['cache_v_win'][0].reshape(nb, WINDOW * SW_KV_HEADS, SW_HEAD_DIM),
        p['state_hgrn'][0],
        p['cache_mem_k'][0].reshape(nb, MEM_LEN * MX_HEADS, MX_HEAD_DIM),
        p['cache_mem_v'][0].reshape(nb, MEM_LEN * MX_HEADS, MX_HEAD_DIM),
        pp['sinks'], pp['hlb'], pp['hnw'], bt=bt, t_new=t_new)
    y = _sample_out(xn, osw, oh, om, pp['lnp'], pp['w_in'], pp['w_up'], pp['w_o'], pp['w_f1'], pp['w_f2'], tr=tr)
    return y.reshape(nb, t_new, D_MODEL), nk, nv, ns


def kernel(x_prompt, x_sample, cache_k_win, cache_v_win, state_hgrn, cache_mem_k, cache_mem_v, mem_prompt,
           ln0_w, ln0_b, w_in, w_up_sw, w_up_hg, w_up_mx, sw_sinks, hg_lower_bound, hg_norm_w, w_mem_kv,
           w_o, ln1_w, ln1_b, w_ffn_in, w_ffn_out, ln2_w, ln2_b):
    p = dict(ln0_w=ln0_w, ln0_b=ln0_b, w_in=w_in, w_up_sw=w_up_sw, w_up_hg=w_up_hg, w_up_mx=w_up_mx,
             sw_sinks=sw_sinks, hg_lower_bound=hg_lower_bound, hg_norm_w=hg_norm_w, w_mem_kv=w_mem_kv,
             w_o=w_o, ln1_w=ln1_w, ln1_b=ln1_b, w_ffn_in=w_ffn_in, w_ffn_out=w_ffn_out, ln2_w=ln2_w, ln2_b=ln2_b)
    p.update(cache_k_win=cache_k_win, cache_v_win=cache_v_win, state_hgrn=state_hgrn,
             cache_mem_k=cache_mem_k, cache_mem_v=cache_mem_v)
    pp = _prep_params(p)
    bp, bs = x_prompt.shape[0], x_sample.shape[0]
    y, nk, nv, s, mk, mv = _prompt_group(x_prompt, mem_prompt, p, tm=PROMPT_TILE, pp=pp)
    ys, nks, nvs, ss = _sample_group(x_sample, p, bt=SAMPLE_BATCH_TILE, pp=pp)
    win = (WINDOW, SW_KV_HEADS, SW_HEAD_DIM)
    mem = (MEM_LEN, MX_HEADS, MX_HEAD_DIM)
    return (y, ys,
            nk.reshape(1, bp, *win), nv.reshape(1, bp, *win), s[None],
            mk.reshape(1, bp, *mem), mv.reshape(1, bp, *mem),
            nks.reshape(1, bs, *win), nvs.reshape(1, bs, *win), ss[None])
```

```python
import functools

import numpy as np
import jax
import jax.numpy as jnp
from jax import lax
from jax.experimental import pallas as pl
from jax.experimental.pallas import tpu as pltpu

D_MODEL = 1024
DEPTH = 1
PAST_LEN = 16384
SW_HEADS, SW_KV_HEADS, SW_HEAD_DIM = 8, 2, 64
SW_GROUP = SW_HEADS // SW_KV_HEADS
WINDOW = 128
HG_HEADS, HG_DIM = 4, 128
MEM_LEN, MX_HEADS, MX_HEAD_DIM = 256, 4, 128
SW_Q = SW_HEADS * SW_HEAD_DIM
SW_KV = SW_KV_HEADS * SW_HEAD_DIM
HG_W = HG_HEADS * HG_DIM
MX_W = MX_HEADS * MX_HEAD_DIM
D_FF = 2816
ALPHA = (2.0 * DEPTH) ** 0.25
LN_EPS = 1e-5
RMS_EPS = 1e-6
OFF_SW = 0
OFF_HG = SW_Q + 2 * SW_KV
OFF_MX = OFF_HG + 4 * HG_W
OFF_GATE = OFF_MX + MX_W
IN_WIDTH = OFF_GATE + 3 * D_MODEL

LANES = 128
BF16_SUBLANES = 16
MXU_TILE = 256
FFN_CHUNK = 4 * MXU_TILE
VMEM_LIMIT_BYTES = 60 * 1024 * 1024

NEG = -1e30
HG_CHUNK = 64
HG_SUB = 16
HG_SAFE_RANGE = 80.0
HG_BOUND_MARGIN = 1.0
PROMPT_TILE = 256
SAMPLE_BATCH_TILE = 4
F32 = jnp.float32
BF16 = jnp.bfloat16


def _dot(a, b):
    return jnp.dot(a, b, preferred_element_type=F32)


def _dot_nt(a, b):
    return lax.dot_general(a, b, (((1,), (1,)), ((), ())), preferred_element_type=F32)


def _dot_tn(a, b):
    return lax.dot_general(a, b, (((0,), (0,)), ((), ())), preferred_element_type=F32)


def _exact_dot(m_bf16, x):
    hi = x.astype(BF16)
    r1 = x - hi.astype(F32)
    mid = r1.astype(BF16)
    lo = (r1 - mid.astype(F32)).astype(BF16)
    return _dot(m_bf16, hi) + _dot(m_bf16, mid) + _dot(m_bf16, lo)


def _layer_norm(x, w, b):
    mu = jnp.mean(x, axis=-1, keepdims=True)
    xc = x - mu
    var = jnp.mean(xc * xc, axis=-1, keepdims=True)
    return xc * lax.rsqrt(var + LN_EPS) * w + b


def _sigmoid(x):
    return 1.0 / (1.0 + jnp.exp(-x))


def _silu(x):
    return x * _sigmoid(x)


def _lower_bound(hlb):
    m = jnp.max(hlb, axis=0, keepdims=True)
    e = jnp.exp(hlb - m)
    return e[0:1] / jnp.sum(e, axis=0, keepdims=True)


def _lane_lo(shape):
    return lax.broadcasted_iota(jnp.int32, shape, len(shape) - 1) < SW_HEAD_DIM


def _dup_heads(kv):
    rolled = pltpu.roll(kv, SW_HEAD_DIM, axis=1)
    lo = _lane_lo(kv.shape)
    return jnp.where(lo, kv, rolled), jnp.where(lo, rolled, kv)


def _sink_softmax_pv(s, vv, sinks_ref, g, rows):
    ps, dens = [], []
    for hh in range(SW_GROUP):
        sh = s[hh * rows:(hh + 1) * rows]
        sink = sinks_ref[g * SW_GROUP + hh]
        m = jnp.maximum(jnp.max(sh, axis=-1, keepdims=True), sink)
        p = jnp.exp(sh - m)
        dens.append(jnp.sum(p, axis=-1, keepdims=True) + jnp.exp(sink - m))
        ps.append(p)
    o = _dot(jnp.concatenate(ps, axis=0).astype(BF16), vv)
    o = o / jnp.concatenate(dens, axis=0)
    lo = _lane_lo((rows, LANES))
    return (jnp.where(lo, o[0:rows], o[rows:2 * rows]),
            jnp.where(lo, o[2 * rows:3 * rows], o[3 * rows:4 * rows]))


def _stack_group_queries(q, g):
    qa = q[:, 2 * g * LANES:(2 * g + 1) * LANES]
    qb = q[:, (2 * g + 1) * LANES:(2 * g + 2) * LANES]
    lo = _lane_lo(qa.shape)
    z = jnp.zeros_like(qa)
    return jnp.concatenate([jnp.where(lo, qa, z), jnp.where(lo, z, qa),
                            jnp.where(lo, qb, z), jnp.where(lo, z, qb)], axis=0).astype(BF16)


def _hgrn_diag(q, k, v, b, n):
    row = lax.broadcasted_iota(jnp.int32, (n, HG_DIM), 0)
    o = jnp.zeros((n, HG_DIM), F32)
    for s in range(n):
        t0 = (s // 8) * 8
        rr = row[t0:]
        e = jnp.exp(jnp.where(rr >= s, b[t0:] - b[s:s + 1], -jnp.inf))
        col = jnp.sum(q[t0:] * e * k[s:s + 1], axis=-1, keepdims=True)
        upd = col * v[s:s + 1]
        o = o + (upd if t0 == 0 else jnp.concatenate([jnp.zeros((t0, HG_DIM), F32), upd], axis=0))
    return o


def _col_bcast(rowvec):
    return jnp.transpose(jnp.broadcast_to(rowvec, (HG_DIM, HG_DIM)))


def _hgrn_chunk_exact(q, k, v, b, st):
    c = HG_CHUNK
    stb = st.astype(BF16)
    o = _dot_nt((q * jnp.exp(b)).astype(BF16), stb)
    b_last = b[c - 1:c]
    kd = (k * jnp.exp(b_last - b)).astype(BF16)
    vb = v.astype(BF16)
    s_new = st * jnp.exp(b_last) + _dot_tn(vb, kd)
    parts = []
    for i in range(c // HG_SUB):
        r0 = i * HG_SUB
        od = _hgrn_diag(q[r0:r0 + HG_SUB], k[r0:r0 + HG_SUB], v[r0:r0 + HG_SUB], b[r0:r0 + HG_SUB], HG_SUB)
        if i > 0:
            g_i = b[r0 - 1:r0]
            qi = (q[r0:r0 + HG_SUB] * jnp.exp(b[r0:r0 + HG_SUB] - g_i)).astype(BF16)
            ki = (k[0:r0] * jnp.exp(g_i - b[0:r0])).astype(BF16)
            a = _dot_nt(qi, ki)
            od = od + _dot(a.astype(BF16), vb[0:r0])
        parts.append(od)
    return o + jnp.concatenate(parts, axis=0), s_new


def _rms_gate(o, gate, nw):
    outs = []
    for h in range(HG_HEADS):
        oh = o[:, h * HG_DIM:(h + 1) * HG_DIM]
        ms = jnp.mean(oh * oh, axis=-1, keepdims=True)
        outs.append(oh * lax.rsqrt(ms + RMS_EPS) * nw * gate[:, h * HG_DIM:(h + 1) * HG_DIM])
    return jnp.concatenate(outs, axis=-1)


def _branch_gate(xb, j, w_in_ref):
    c0 = OFF_GATE + j * D_MODEL
    return _sigmoid(_dot(xb, w_in_ref[:, c0:c0 + D_MODEL]))


def _out_ffn(xn, mix, w_o_ref, w_f1_ref, w_f2_ref, lnp_ref):
    h = _layer_norm(ALPHA * xn + _dot(mix.astype(BF16), w_o_ref[...]), lnp_ref[2:3], lnp_ref[3:4])
    hb = h.astype(BF16)
    acc = None
    for c0 in range(0, D_FF, FFN_CHUNK):
        c1 = min(c0 + FFN_CHUNK, D_FF)
        gt = _dot(hb, w_f1_ref[:, c0:c1])
        up = _dot(hb, w_f1_ref[:, D_FF + c0:D_FF + c1])
        part = _dot((_silu(gt) * up).astype(BF16), w_f2_ref[c0:c1, :])
        acc = part if acc is None else acc + part
    return _layer_norm(ALPHA * h + acc, lnp_ref[4:5], lnp_ref[5:6])


def _hgrn_inputs(hp, lb):
    f = lb + (1.0 - lb) * _sigmoid(hp[:, HG_W:2 * HG_W])
    return (_silu(hp[:, 0:HG_W]), 1.0 - f, hp[:, 2 * HG_W:3 * HG_W], jnp.log(f), _silu(hp[:, 3 * HG_W:4 * HG_W]))


def _prompt_bias():
    i = np.arange(WINDOW)[:, None]
    j = np.arange(2 * WINDOW)[None, :]
    dist = i + WINDOW - j
    valid = (dist >= 0) & (dist < WINDOW)
    out = np.zeros((SW_KV_HEADS, SW_GROUP, WINDOW, 2 * WINDOW), np.float32)
    for h in range(SW_HEADS):
        slope = np.float32(2.0) ** np.float32(-8.0 * (h + 1) / SW_HEADS)
        out[h // SW_GROUP, h % SW_GROUP] = np.where(valid, -slope * dist.astype(np.float32), np.float32(NEG))
    return out.reshape(SW_KV_HEADS, SW_GROUP * WINDOW, 2 * WINDOW)


def _prompt_kernel(sinks_ref, x_ref, mk_ref, mv_ref, bias_ref, tri_ref, lnp_ref, hlb_ref, hnw_ref,
                   w_in_ref, w_up_ref, w_o_ref, w_f1_ref, w_f2_ref,
                   y_ref, nk_ref, nv_ref, s_ref,
                   kd_ref, vd_ref, st_ref, sto_ref, hq_ref, hk_ref, hv_ref, hb_ref, oh_ref, *, tm, nt, fallback):
    t = pl.program_id(1)

    @pl.when(t == 0)
    def _():
        kd_ref[:, 0:WINDOW, :] = jnp.zeros((SW_KV_HEADS, WINDOW, LANES), BF16)
        vd_ref[:, 0:WINDOW, :] = jnp.zeros((SW_KV_HEADS, WINDOW, LANES), BF16)
        st_ref[...] = jnp.zeros_like(st_ref)

    half = tm // 2
    xn_h = [_layer_norm(x_ref[0, r0:r0 + half, :], lnp_ref[0:1], lnp_ref[1:2]) for r0 in (0, half)]
    qkv = jnp.concatenate([_dot(xh.astype(BF16), w_in_ref[:, OFF_SW:OFF_HG]) for xh in xn_h], axis=0)
    xn = jnp.concatenate(xn_h, axis=0)
    xb = xn.astype(BF16)
    q = qkv[:, 0:SW_Q] * (SW_HEAD_DIM ** -0.5)
    k = qkv[:, SW_Q:SW_Q + SW_KV]
    v = qkv[:, SW_Q + SW_KV:SW_Q + 2 * SW_KV]

    nk_ref[0] = k[tm - WINDOW:tm]
    nv_ref[0] = v[tm - WINDOW:tm]

    for g, dup in enumerate(_dup_heads(k)):
        kd_ref[g, WINDOW:WINDOW + tm, :] = dup.astype(BF16)
    for g, dup in enumerate(_dup_heads(v)):
        vd_ref[g, WINDOW:WINDOW + tm, :] = dup.astype(BF16)

    col = lax.broadcasted_iota(jnp.int32, (1, 2 * WINDOW), 1)
    first = jnp.where(jnp.logical_and(t == 0, col < WINDOW), NEG, 0.0).astype(F32)
    def proj(c0, width):
        return _dot(xb, w_in_ref[:, c0:c0 + width])

    hp_parts = []
    o_rows = []
    for j in range(tm // WINDOW):
        qj = q[j * WINDOW:(j + 1) * WINDOW]
        blocks = []
        for g in range(SW_KV_HEADS):
            kk = kd_ref[g, j * WINDOW:(j + 2) * WINDOW, :]
            vv = vd_ref[g, j * WINDOW:(j + 2) * WINDOW, :]
            s = _dot_nt(_stack_group_queries(qj, g), kk) + bias_ref[g]
            if j == 0:
                s = s + first
            if len(hp_parts) < 4:
                hp_parts.append(proj(OFF_HG + len(hp_parts) * HG_W, HG_W))
            blocks.extend(_sink_softmax_pv(s, vv, sinks_ref, g, WINDOW))
        o_rows.append(jnp.concatenate(blocks, axis=-1))
    o_sw = jnp.concatenate(o_rows, axis=0)
    while len(hp_parts) < 4:
        hp_parts.append(proj(OFF_HG + len(hp_parts) * HG_W, HG_W))
    kd_ref[:, 0:WINDOW, :] = kd_ref[:, tm:tm + WINDOW, :]
    vd_ref[:, 0:WINDOW, :] = vd_ref[:, tm:tm + WINDOW, :]

    hq, hk, hv, hl, hgate = _hgrn_inputs(jnp.concatenate(hp_parts, axis=-1), _lower_bound(hlb_ref[...]))
    hb = _exact_dot(tri_ref[...], hl)
    mq = proj(OFF_MX, MX_W)
    gate_sw = _branch_gate(xb, 0, w_in_ref)
    if fallback:
        sto_ref[...] = st_ref[...]
    qe = (hq * jnp.exp(hb)).astype(BF16)
    ke = (hk * jnp.exp(-hb)).astype(BF16)
    vb = hv.astype(BF16)
    ri = lax.broadcasted_iota(jnp.int32, (HG_CHUNK, HG_CHUNK), 0)
    ci = lax.broadcasted_iota(jnp.int32, (HG_CHUNK, HG_CHUNK), 1)
    causal = ri >= ci
    n_chunks = tm // HG_CHUNK
    heads = [slice(h * HG_DIM, (h + 1) * HG_DIM) for h in range(HG_HEADS)]
    chunks = [slice(c * HG_CHUNK, (c + 1) * HG_CHUNK) for c in range(n_chunks)]
    a_mat, upd, dec = {}, {}, []
    for c, rows in enumerate(chunks):
        b_last = hb[(c + 1) * HG_CHUNK - 1:(c + 1) * HG_CHUNK]
        kd = (hk[rows] * jnp.exp(b_last - hb[rows])).astype(BF16)
        dec.append(jnp.exp(b_last))
        for h, sl in enumerate(heads):
            a_mat[c, h] = jnp.where(causal, _dot_nt(qe[rows, sl], ke[rows, sl]), 0.0).astype(BF16)
            upd[c, h] = _dot_tn(vb[rows, sl], kd[:, sl])
    gate_mx = _branch_gate(xb, 2, w_in_ref)
    st = {(0, h): st_ref[h] for h in range(HG_HEADS)}
    for c in range(n_chunks):
        for h, sl in enumerate(heads):
            st[c + 1, h] = st[c, h] * dec[c][:, sl] + upd[c, h]
    for h in range(HG_HEADS):
        st_ref[h] = st[n_chunks, h]
    for c, rows in enumerate(chunks):
        for h, sl in enumerate(heads):
            oh_ref[rows, sl] = _dot_nt(qe[rows, sl], st[c, h].astype(BF16)) + _dot(a_mat[c, h], vb[rows, sl])
    up_sw = _dot(o_sw.astype(BF16), w_up_ref[0])

    fillers = [lambda: proj(OFF_GATE + D_MODEL, D_MODEL // 2), lambda: proj(OFF_GATE + D_MODEL + D_MODEL // 2, D_MODEL // 2)]
    filled = []
    mk = mk_ref[0]
    mv = mv_ref[0]
    outs = []
    for h in range(MX_HEADS):
        sl = slice(h * MX_HEAD_DIM, (h + 1) * MX_HEAD_DIM)
        s = _dot_nt(mq[:, sl].astype(BF16), mk[:, sl]) * (MX_HEAD_DIM ** -0.5)
        if h % 2 == 0:
            filled.append(fillers[h // 2]())
        p = jnp.exp(s - jnp.max(s, axis=-1, keepdims=True))
        den = jnp.sum(p, axis=-1, keepdims=True)
        outs.append(_dot(p.astype(BF16), mv[:, sl]) / den)
    o_m = jnp.concatenate(outs, axis=-1)
    gate_h = _sigmoid(jnp.concatenate(filled, axis=-1))
    mix = gate_sw * up_sw + gate_mx * _dot(o_m.astype(BF16), w_up_ref[2])

    def redo_tile_exact():
        hq_ref[...] = hq
        hk_ref[...] = hk
        hv_ref[...] = hv
        hb_ref[...] = hb

        def chunk_body(c, carry):
            rows = pl.ds(pl.multiple_of(c * HG_CHUNK, HG_CHUNK), HG_CHUNK)
            for h in range(HG_HEADS):
                sl = slice(h * HG_DIM, (h + 1) * HG_DIM)
                o, st_new = _hgrn_chunk_exact(hq_ref[rows, sl], hk_ref[rows, sl], hv_ref[rows, sl], hb_ref[rows, sl],
                                              sto_ref[h])
                sto_ref[h] = st_new
                oh_ref[rows, sl] = o
            return carry

        lax.fori_loop(0, tm // HG_CHUNK, chunk_body, 0)
        st_ref[...] = sto_ref[...]

    if fallback:
        pl.when(jnp.min(hb) < -HG_SAFE_RANGE)(redo_tile_exact)

    mix = mix + gate_h * _dot(_rms_gate(oh_ref[...], hgate, hnw_ref[...]).astype(BF16), w_up_ref[1])
    y_ref[0] = _out_ffn(xn, mix, w_o_ref, w_f1_ref, w_f2_ref, lnp_ref)

    @pl.when(t == nt - 1)
    def _():
        for h in range(HG_HEADS):
            s_ref[0, h] = jnp.transpose(st_ref[h])


def _full(shape):
    nd = len(shape)
    return pl.BlockSpec(shape, lambda *_: (0,) * nd, pipeline_mode=pl.Buffered(1))


def _prompt_layer(x, mk_b, mv_b, sinks, lnp, hlb, hnw, w_in, w_up, w_o, w_f1, w_f2, *, tm, fallback):
    bsz, seq, _ = x.shape
    nt = seq // tm
    assert seq % tm == 0 and tm % WINDOW == 0 and tm % HG_CHUNK == 0
    bias = jnp.asarray(_prompt_bias())
    r = np.arange(tm)
    tri = jnp.asarray((r[:, None] // HG_CHUNK == r[None, :] // HG_CHUNK) & (r[:, None] >= r[None, :]), BF16)
    kern = functools.partial(_prompt_kernel, tm=tm, nt=nt, fallback=fallback)
    return pl.pallas_call(
        kern,
        grid=(bsz, nt),
        in_specs=[
            pl.BlockSpec(memory_space=pltpu.SMEM),
            pl.BlockSpec((1, tm, D_MODEL), lambda b, t: (b, t, 0)),
            pl.BlockSpec((1, MEM_LEN, MX_W), lambda b, t: (b, 0, 0)),
            pl.BlockSpec((1, MEM_LEN, MX_W), lambda b, t: (b, 0, 0)),
            _full(bias.shape), _full(tri.shape), _full(lnp.shape), _full(hlb.shape), _full(hnw.shape),
            _full(w_in.shape), _full(w_up.shape), _full(w_o.shape), _full(w_f1.shape), _full(w_f2.shape),
        ],
        out_specs=[
            pl.BlockSpec((1, tm, D_MODEL), lambda b, t: (b, t, 0)),
            pl.BlockSpec((1, WINDOW, SW_KV), lambda b, t: (b, 0, 0)),
            pl.BlockSpec((1, WINDOW, SW_KV), lambda b, t: (b, 0, 0)),
            pl.BlockSpec((1, HG_HEADS, HG_DIM, HG_DIM), lambda b, t: (b, 0, 0, 0)),
        ],
        out_shape=[
            jax.ShapeDtypeStruct((bsz, seq, D_MODEL), F32),
            jax.ShapeDtypeStruct((bsz, WINDOW, SW_KV), F32),
            jax.ShapeDtypeStruct((bsz, WINDOW, SW_KV), F32),
            jax.ShapeDtypeStruct((bsz, HG_HEADS, HG_DIM, HG_DIM), F32),
        ],
        scratch_shapes=[
            pltpu.VMEM((SW_KV_HEADS, WINDOW + tm, LANES), BF16),
            pltpu.VMEM((SW_KV_HEADS, WINDOW + tm, LANES), BF16),
            pltpu.VMEM((HG_HEADS, HG_DIM, HG_DIM), F32),
            pltpu.VMEM((HG_HEADS, HG_DIM, HG_DIM), F32),
            pltpu.VMEM((tm, HG_W), F32),
            pltpu.VMEM((tm, HG_W), F32),
            pltpu.VMEM((tm, HG_W), F32),
            pltpu.VMEM((tm, HG_W), F32),
            pltpu.VMEM((tm, HG_W), F32),
        ],
        compiler_params=pltpu.CompilerParams(
            dimension_semantics=("arbitrary", "arbitrary"), vmem_limit_bytes=VMEM_LIMIT_BYTES),
        name="prompt_layer",
    )(sinks, x, mk_b, mv_b, bias, tri, lnp, hlb, hnw, w_in, w_up, w_o, w_f1, w_f2)


def _mem_kv_kernel(m_ref, w_ref, mk_ref, mv_ref, mkb_ref, mvb_ref):
    kv = _dot(m_ref[0].astype(BF16), w_ref[...])
    for h in range(MX_HEADS):
        head_rows = pl.ds(h, MEM_LEN, stride=MX_HEADS)
        mk_ref[0, head_rows, :] = kv[:, h * MX_HEAD_DIM:(h + 1) * MX_HEAD_DIM]
        mv_ref[0, head_rows, :] = kv[:, MX_W + h * MX_HEAD_DIM:MX_W + (h + 1) * MX_HEAD_DIM]
    mkb_ref[0] = kv[:, 0:MX_W].astype(BF16)
    mvb_ref[0] = kv[:, MX_W:2 * MX_W].astype(BF16)


def _mem_kv(mem, w_kv):
    bsz = mem.shape[0]
    blk = pl.BlockSpec((1, MEM_LEN, MX_W), lambda b: (b, 0, 0))
    row_blk = pl.BlockSpec((1, MEM_LEN * MX_HEADS, MX_HEAD_DIM), lambda b: (b, 0, 0))
    return pl.pallas_call(
        _mem_kv_kernel,
        grid=(bsz,),
        in_specs=[pl.BlockSpec((1, MEM_LEN, D_MODEL), lambda b: (b, 0, 0)), _full(w_kv.shape)],
        out_specs=[row_blk, row_blk, blk, blk],
        out_shape=[jax.ShapeDtypeStruct((bsz, MEM_LEN * MX_HEADS, MX_HEAD_DIM), F32)] * 2
        + [jax.ShapeDtypeStruct((bsz, MEM_LEN, MX_W), BF16)] * 2,
        compiler_params=pltpu.CompilerParams(dimension_semantics=("arbitrary",)),
        name="mem_kv",
    )(mem, w_kv)


def _prep_params(p):
    assert p['w_in'].shape[0] == DEPTH == 1
    zeros = jnp.zeros((D_MODEL,), F32)
    return dict(
        lnp=jnp.stack([p['ln0_w'], p['ln0_b'], p['ln1_w'][0], p['ln1_b'][0], p['ln2_w'][0], p['ln2_b'][0],
                       zeros, zeros]).astype(F32),
        sinks=p['sw_sinks'][0].astype(F32),
        hlb=p['hg_lower_bound'].astype(F32),
        hnw=p['hg_norm_w'].astype(F32),
        w_in=p['w_in'][0].astype(BF16),
        w_up=jnp.stack([p['w_up_sw'][0], p['w_up_hg'][0], p['w_up_mx'][0]]).astype(BF16),
        w_o=p['w_o'][0].astype(BF16),
        w_f1=p['w_ffn_in'][0].astype(BF16),
        w_f2=p['w_ffn_out'][0].astype(BF16),
        w_kv=p['w_mem_kv'][0].astype(BF16),
    )


def _prompt_group(x, mem, p, *, tm, pp=None):
    pp = _prep_params(p) if pp is None else pp
    mk, mv, mk_b, mv_b = _mem_kv(mem, pp['w_kv'])
    args = (x, mk_b, mv_b, pp['sinks'], pp['lnp'], pp['hlb'], pp['hnw'],
            pp['w_in'], pp['w_up'], pp['w_o'], pp['w_f1'], pp['w_f2'])
    lb_min = jnp.min(jax.nn.softmax(pp['hlb'], axis=0)[0])
    never_redo = HG_CHUNK * jnp.log(lb_min) >= -(HG_SAFE_RANGE - HG_BOUND_MARGIN)
    y, nk, nv, s = lax.cond(never_redo,
                            functools.partial(_prompt_layer, tm=tm, fallback=False),
                            functools.partial(_prompt_layer, tm=tm, fallback=True), *args)
    return y, nk, nv, s, mk, mv


def _sample_proj_kernel(x_ref, lnp_ref, w_in_ref, xn_ref, proj_ref):
    xn = _layer_norm(x_ref[...], lnp_ref[0:1], lnp_ref[1:2])
    xn_ref[...] = xn
    proj_ref[...] = _dot(xn.astype(BF16), w_in_ref[:, 0:OFF_GATE])


def _sample_proj(x2, lnp, w_in, *, tr):
    rows = x2.shape[0]
    return pl.pallas_call(
        _sample_proj_kernel,
        grid=(rows // tr,),
        in_specs=[pl.BlockSpec((tr, D_MODEL), lambda i: (i, 0)), _full(lnp.shape), _full(w_in.shape)],
        out_specs=[pl.BlockSpec((tr, D_MODEL), lambda i: (i, 0)), pl.BlockSpec((tr, OFF_GATE), lambda i: (i, 0))],
        out_shape=[jax.ShapeDtypeStruct((rows, D_MODEL), F32), jax.ShapeDtypeStruct((rows, OFF_GATE), F32)],
        compiler_params=pltpu.CompilerParams(
            dimension_semantics=("arbitrary",), vmem_limit_bytes=VMEM_LIMIT_BYTES),
        name="sample_proj",
    )(x2, lnp, w_in)


def _sample_bias(t_new, n_keys):
    i = np.arange(t_new)[:, None]
    j = np.arange(n_keys)[None, :]
    kpos = np.where(j < WINDOW, PAST_LEN - WINDOW + j, PAST_LEN + j - WINDOW)
    dist = PAST_LEN + i - kpos
    valid = (dist >= 0) & (dist < WINDOW) & (j < WINDOW + t_new)
    out = np.zeros((SW_KV_HEADS, SW_GROUP, t_new, n_keys), np.float32)
    for h in range(SW_HEADS):
        slope = np.float32(2.0) ** np.float32(-8.0 * (h + 1) / SW_HEADS)
        out[h // SW_GROUP, h % SW_GROUP] = np.where(valid, -slope * dist.astype(np.float32), np.float32(NEG))
    return out.reshape(SW_KV_HEADS, SW_GROUP * t_new, n_keys)


def _sample_mix_kernel(sinks_ref, proj_ref, kbuf_ref, vbuf_ref, st_ref, mk_ref, mv_ref, bias_ref, hlb_ref, hnw_ref,
                       osw_ref, oh_ref, om_ref, nk_ref, nv_ref, ns_ref,
                       kk_ref, vv_ref, *, bt, t_new):
    nkeys = 2 * WINDOW
    r = bt * t_new
    elems = [slice(i * t_new, (i + 1) * t_new) for i in range(bt)]
    heads = [slice(h * HG_DIM, (h + 1) * HG_DIM) for h in range(HG_HEADS)]
    pad_rows = BF16_SUBLANES - t_new

    def pad(a):
        return jnp.concatenate([a, jnp.zeros((pad_rows, a.shape[1]), F32)], axis=0)

    def mem_head(ref, i, h):
        return ref[i, pl.ds(h, MEM_LEN, stride=MX_HEADS), :].astype(BF16)

    hq, hk, hv, hl, hgate = _hgrn_inputs(proj_ref[:, OFF_HG:OFF_MX], _lower_bound(hlb_ref[...]))
    ri = lax.broadcasted_iota(jnp.int32, (r, r), 0)
    ci = lax.broadcasted_iota(jnp.int32, (r, r), 1)
    same = (ri // t_new) == (ci // t_new)
    tri = jnp.where(jnp.logical_and(same, ri >= ci), 1.0, 0.0).astype(BF16)
    hb = _exact_dot(tri, hl)
    qe = hq * jnp.exp(hb)
    q_all = proj_ref[:, 0:SW_Q] * (SW_HEAD_DIM ** -0.5)
    mq_all = proj_ref[:, OFF_MX:OFF_GATE]

    zeros = jnp.zeros((nkeys - WINDOW - t_new, LANES), F32)
    for i, rows in enumerate(elems):
        for buf_ref, cache_ref, c0, out_ref in ((kk_ref, kbuf_ref, SW_Q, nk_ref),
                                                (vv_ref, vbuf_ref, SW_Q + SW_KV, nv_ref)):
            buf_ref[i, 0:WINDOW, :] = cache_ref[i]
            buf_ref[i, WINDOW:WINDOW + t_new, :] = proj_ref[rows, c0:c0 + SW_KV]
            buf_ref[i, WINDOW + t_new:nkeys, :] = zeros
            out_ref[i] = buf_ref[i, t_new:t_new + WINDOW, :]

    s_att, s_mem, o_state, upd, b_last = {}, {}, {}, {}, {}
    for i, rows in enumerate(elems):
        kds = _dup_heads(kk_ref[i])
        for g in range(SW_KV_HEADS):
            s_att[i, g] = _dot_nt(_stack_group_queries(q_all[rows], g), kds[g].astype(BF16)) + bias_ref[g]
    for i, rows in enumerate(elems):
        mq = pad(mq_all[rows]).astype(BF16)
        for h, sl in enumerate(heads):
            s_mem[i, h] = _dot_nt(mq[:, sl], mem_head(mk_ref, i, h)) * (MX_HEAD_DIM ** -0.5)
    for i, rows in enumerate(elems):
        b_last[i] = hb[(i + 1) * t_new - 1:(i + 1) * t_new]
        kd = pad(hk[rows] * jnp.exp(b_last[i] - hb[rows])).astype(BF16)
        vb = pad(hv[rows]).astype(BF16)
        qb = pad(qe[rows]).astype(BF16)
        for h, sl in enumerate(heads):
            o_state[i, h] = _dot(qb[:, sl], st_ref[i, h].astype(BF16))[0:t_new]
            upd[i, h] = _dot_tn(kd[:, sl], vb[:, sl])

    p_att, den_att, p_mem, den_mem = {}, {}, {}, {}
    for i in range(bt):
        for g in range(SW_KV_HEADS):
            ps, dens = [], []
            for hh in range(SW_GROUP):
                sh = s_att[i, g][hh * t_new:(hh + 1) * t_new]
                sink = sinks_ref[g * SW_GROUP + hh]
                m = jnp.maximum(jnp.max(sh, axis=-1, keepdims=True), sink)
                p = jnp.exp(sh - m)
                dens.append(jnp.sum(p, axis=-1, keepdims=True) + jnp.exp(sink - m))
                ps.append(p)
            p_att[i, g] = jnp.concatenate(ps, axis=0).astype(BF16)
            den_att[i, g] = jnp.concatenate(dens, axis=0)
        for h in range(MX_HEADS):
            p = jnp.exp(s_mem[i, h] - jnp.max(s_mem[i, h], axis=-1, keepdims=True))
            den_mem[i, h] = jnp.sum(p, axis=-1, keepdims=True)
            p_mem[i, h] = p.astype(BF16)
    for i, rows in enumerate(elems):
        dec = jnp.exp(b_last[i])
        for h, sl in enumerate(heads):
            ns_ref[i, h] = _col_bcast(dec[:, sl]) * st_ref[i, h] + upd[i, h]
            oh_ref[rows, sl] = o_state[i, h] + _hgrn_diag(hq[rows, sl], hk[rows, sl], hv[rows, sl], hb[rows, sl],
                                                          t_new)

    lo = _lane_lo((t_new, LANES))
    for i, rows in enumerate(elems):
        vds = _dup_heads(vv_ref[i])
        blocks = []
        for g in range(SW_KV_HEADS):
            o = _dot(p_att[i, g], vds[g].astype(BF16)) / den_att[i, g]
            blocks.append(jnp.where(lo, o[0:t_new], o[t_new:2 * t_new]))
            blocks.append(jnp.where(lo, o[2 * t_new:3 * t_new], o[3 * t_new:4 * t_new]))
        osw_ref[rows, :] = jnp.concatenate(blocks, axis=-1)
    for i, rows in enumerate(elems):
        outs = [(_dot(p_mem[i, h], mem_head(mv_ref, i, h)) / den_mem[i, h])[0:t_new] for h in range(MX_HEADS)]
        om_ref[rows, :] = jnp.concatenate(outs, axis=-1)
    oh_ref[...] = _rms_gate(oh_ref[...], hgate, hnw_ref[...])


def _sample_mix(proj, kbuf, vbuf, st, mk, mv, sinks, hlb, hnw, *, bt, t_new):
    nb = kbuf.shape[0]
    r = bt * t_new
    assert nb % bt == 0 and t_new == 8 and r % BF16_SUBLANES == 0
    bias = jnp.asarray(_sample_bias(t_new, 2 * WINDOW))
    kern = functools.partial(_sample_mix_kernel, bt=bt, t_new=t_new)
    row_blk = pl.BlockSpec((r, HG_W), lambda i: (i, 0))
    kv_blk = pl.BlockSpec((bt, WINDOW, SW_KV), lambda i: (i, 0, 0))
    st_blk = pl.BlockSpec((bt, HG_HEADS, HG_DIM, HG_DIM), lambda i: (i, 0, 0, 0))
    mem_blk = pl.BlockSpec((bt, MEM_LEN * MX_HEADS, MX_HEAD_DIM), lambda i: (i, 0, 0))
    return pl.pallas_call(
        kern,
        grid=(nb // bt,),
        in_specs=[pl.BlockSpec(memory_space=pltpu.SMEM),
                  pl.BlockSpec((r, OFF_GATE), lambda i: (i, 0)),
                  kv_blk, kv_blk, st_blk, mem_blk, mem_blk,
                  _full(bias.shape), _full(hlb.shape), _full(hnw.shape)],
        out_specs=[row_blk, row_blk, row_blk, kv_blk, kv_blk, st_blk],
        out_shape=[jax.ShapeDtypeStruct((nb * t_new, HG_W), F32)] * 3
        + [jax.ShapeDtypeStruct((nb, WINDOW, SW_KV), F32)] * 2
        + [jax.ShapeDtypeStruct((nb, HG_HEADS, HG_DIM, HG_DIM), F32)],
        scratch_shapes=[pltpu.VMEM((bt, 2 * WINDOW, LANES), F32)] * 2,
        compiler_params=pltpu.CompilerParams(
            dimension_semantics=("arbitrary",), vmem_limit_bytes=VMEM_LIMIT_BYTES),
        name="sample_mix",
    )(sinks, proj, kbuf, vbuf, st, mk, mv, bias, hlb, hnw)


def _sample_out_kernel(xn_ref, osw_ref, oh_ref, om_ref, lnp_ref, w_in_ref, w_up_ref, w_o_ref, w_f1_ref, w_f2_ref,
                       y_ref):
    xn = xn_ref[...]
    xb = xn.astype(BF16)
    mix = None
    for j, br_ref in enumerate((osw_ref, oh_ref, om_ref)):
        term = _branch_gate(xb, j, w_in_ref) * _dot(br_ref[...].astype(BF16), w_up_ref[j])
        mix = term if mix is None else mix + term
    y_ref[...] = _out_ffn(xn, mix, w_o_ref, w_f1_ref, w_f2_ref, lnp_ref)


def _sample_out(xn, osw, oh, om, lnp, w_in, w_up, w_o, w_f1, w_f2, *, tr):
    rows = xn.shape[0]
    wide = pl.BlockSpec((tr, D_MODEL), lambda i: (i, 0))
    half = pl.BlockSpec((tr, HG_W), lambda i: (i, 0))
    return pl.pallas_call(
        _sample_out_kernel,
        grid=(rows // tr,),
        in_specs=[wide, half, half, half, _full(lnp.shape), _full(w_in.shape), _full(w_up.shape),
                  _full(w_o.shape), _full(w_f1.shape), _full(w_f2.shape)],
        out_specs=wide,
        out_shape=jax.ShapeDtypeStruct((rows, D_MODEL), F32),
        compiler_params=pltpu.CompilerParams(
            dimension_semantics=("arbitrary",), vmem_limit_bytes=VMEM_LIMIT_BYTES),
        name="sample_out",
    )(xn, osw, oh, om, lnp, w_in, w_up, w_o, w_f1, w_f2)


def _sample_group(x, p, *, bt, pp=None):
    pp = _prep_params(p) if pp is None else pp
    nb, t_new, _ = x.shape
    rows = nb * t_new
    tr = min(rows, 256)
    xn, proj = _sample_proj(x.reshape(rows, D_MODEL), pp['lnp'], pp['w_in'], tr=tr)
    osw, oh, om, nk, nv, ns = _sample_mix(
        proj,
        p['cache_k_win'][0].reshape(nb, WINDOW, SW_KV), p['cache_v_win'][0].reshape(nb, WINDOW, SW_KV),
        p['state_hgrn'][0],
        p['cache_mem_k'][0].reshape(nb, MEM_LEN * MX_HEADS, MX_HEAD_DIM),
        p['cache_mem_v'][0].reshape(nb, MEM_LEN * MX_HEADS, MX_HEAD_DIM),
        pp['sinks'], pp['hlb'], pp['hnw'], bt=bt, t_new=t_new)
    y = _sample_out(xn, osw, oh, om, pp['lnp'], pp['w_in'], pp['w_up'], pp['w_o'], pp['w_f1'], pp['w_f2'], tr=tr)
    return y.reshape(nb, t_new, D_MODEL), nk, nv, ns


def kernel(x_prompt, x_sample, cache_k_win, cache_v_win, state_hgrn, cache_mem_k, cache_mem_v, mem_prompt,
           ln0_w, ln0_b, w_in, w_up_sw, w_up_hg, w_up_mx, sw_sinks, hg_lower_bound, hg_norm_w, w_mem_kv,
           w_o, ln1_w, ln1_b, w_ffn_in, w_ffn_out, ln2_w, ln2_b):
    p = dict(ln0_w=ln0_w, ln0_b=ln0_b, w_in=w_in, w_up_sw=w_up_sw, w_up_hg=w_up_hg, w_up_mx=w_up_mx,
             sw_sinks=sw_sinks, hg_lower_bound=hg_lower_bound, hg_norm_w=hg_norm_w, w_mem_kv=w_mem_kv,
             w_o=w_o, ln1_w=ln1_w, ln1_b=ln1_b, w_ffn_in=w_ffn_in, w_ffn_out=w_ffn_out, ln2_w=ln2_w, ln2_b=ln2_b)
    p.update(cache_k_win=cache_k_win, cache_v_win=cache_v_win, state_hgrn=state_hgrn,
             cache_mem_k=cache_mem_k, cache_mem_v=cache_mem_v)
    pp = _prep_params(p)
    bp, bs = x_prompt.shape[0], x_sample.shape[0]
    y, nk, nv, s, mk, mv = _prompt_group(x_prompt, mem_prompt, p, tm=PROMPT_TILE, pp=pp)
    ys, nks, nvs, ss = _sample_group(x_sample, p, bt=SAMPLE_BATCH_TILE, pp=pp)
    win = (WINDOW, SW_KV_HEADS, SW_HEAD_DIM)
    mem = (MEM_LEN, MX_HEADS, MX_HEAD_DIM)
    return (y, ys,
            nk.reshape(1, bp, *win), nv.reshape(1, bp, *win), s[None],
            mk.reshape(1, bp, *mem), mv.reshape(1, bp, *mem),
            nks.reshape(1, bs, *win), nvs.reshape(1, bs, *win), ss[None])
```

```python
import functools

import numpy as np
import jax
import jax.numpy as jnp
from jax import lax
from jax.experimental import pallas as pl
from jax.experimental.pallas import tpu as pltpu

D_MODEL = 1024
DEPTH = 1
PAST_LEN = 16384
SW_HEADS, SW_KV_HEADS, SW_HEAD_DIM = 8, 2, 64
SW_GROUP = SW_HEADS // SW_KV_HEADS
WINDOW = 128
HG_HEADS, HG_DIM = 4, 128
MEM_LEN, MX_HEADS, MX_HEAD_DIM = 256, 4, 128
SW_Q = SW_HEADS * SW_HEAD_DIM
SW_KV = SW_KV_HEADS * SW_HEAD_DIM
HG_W = HG_HEADS * HG_DIM
MX_W = MX_HEADS * MX_HEAD_DIM
D_FF = 2816
ALPHA = (2.0 * DEPTH) ** 0.25
LN_EPS = 1e-5
RMS_EPS = 1e-6
OFF_SW = 0
OFF_HG = SW_Q + 2 * SW_KV
OFF_MX = OFF_HG + 4 * HG_W
OFF_GATE = OFF_MX + MX_W
IN_WIDTH = OFF_GATE + 3 * D_MODEL

LANES = 128
BF16_SUBLANES = 16
MXU_TILE = 256
FFN_CHUNK = 4 * MXU_TILE
VMEM_LIMIT_BYTES = 60 * 1024 * 1024

NEG = -1e30
HG_CHUNK = 64
HG_SUB = 16
HG_SAFE_RANGE = 80.0
HG_BOUND_MARGIN = 1.0
PROMPT_TILE = 256
SAMPLE_BATCH_TILE = 4
F32 = jnp.float32
BF16 = jnp.bfloat16


def _dot(a, b):
    return jnp.dot(a, b, preferred_element_type=F32)


def _dot_nt(a, b):
    return lax.dot_general(a, b, (((1,), (1,)), ((), ())), preferred_element_type=F32)


def _dot_tn(a, b):
    return lax.dot_general(a, b, (((0,), (0,)), ((), ())), preferred_element_type=F32)


def _exact_dot(m_bf16, x):
    hi = x.astype(BF16)
    r1 = x - hi.astype(F32)
    mid = r1.astype(BF16)
    lo = (r1 - mid.astype(F32)).astype(BF16)
    return _dot(m_bf16, hi) + _dot(m_bf16, mid) + _dot(m_bf16, lo)


def _layer_norm(x, w, b):
    mu = jnp.mean(x, axis=-1, keepdims=True)
    xc = x - mu
    var = jnp.mean(xc * xc, axis=-1, keepdims=True)
    return xc * lax.rsqrt(var + LN_EPS) * w + b


def _sigmoid(x):
    return 1.0 / (1.0 + jnp.exp(-x))


def _silu(x):
    return x * _sigmoid(x)


def _lower_bound(hlb):
    m = jnp.max(hlb, axis=0, keepdims=True)
    e = jnp.exp(hlb - m)
    return e[0:1] / jnp.sum(e, axis=0, keepdims=True)


def _lane_lo(shape):
    return lax.broadcasted_iota(jnp.int32, shape, len(shape) - 1) < SW_HEAD_DIM


def _dup_heads(kv):
    rolled = pltpu.roll(kv, SW_HEAD_DIM, axis=1)
    lo = _lane_lo(kv.shape)
    return jnp.where(lo, kv, rolled), jnp.where(lo, rolled, kv)


def _sink_softmax_pv(s, vv, sinks_ref, g, rows):
    ps, dens = [], []
    for hh in range(SW_GROUP):
        sh = s[hh * rows:(hh + 1) * rows]
        sink = sinks_ref[g * SW_GROUP + hh]
        m = jnp.maximum(jnp.max(sh, axis=-1, keepdims=True), sink)
        p = jnp.exp(sh - m)
        dens.append(jnp.sum(p, axis=-1, keepdims=True) + jnp.exp(sink - m))
        ps.append(p)
    o = _dot(jnp.concatenate(ps, axis=0).astype(BF16), vv)
    o = o / jnp.concatenate(dens, axis=0)
    lo = _lane_lo((rows, LANES))
    return (jnp.where(lo, o[0:rows], o[rows:2 * rows]),
            jnp.where(lo, o[2 * rows:3 * rows], o[3 * rows:4 * rows]))


def _stack_group_queries(q, g):
    qa = q[:, 2 * g * LANES:(2 * g + 1) * LANES]
    qb = q[:, (2 * g + 1) * LANES:(2 * g + 2) * LANES]
    lo = _lane_lo(qa.shape)
    z = jnp.zeros_like(qa)
    return jnp.concatenate([jnp.where(lo, qa, z), jnp.where(lo, z, qa),
                            jnp.where(lo, qb, z), jnp.where(lo, z, qb)], axis=0).astype(BF16)


def _hgrn_diag(q, k, v, b, n):
    row = lax.broadcasted_iota(jnp.int32, (n, HG_DIM), 0)
    o = jnp.zeros((n, HG_DIM), F32)
    for s in range(n):
        t0 = (s // 8) * 8
        rr = row[t0:]
        e = jnp.exp(jnp.where(rr >= s, b[t0:] - b[s:s + 1], -jnp.inf))
        col = jnp.sum(q[t0:] * e * k[s:s + 1], axis=-1, keepdims=True)
        upd = col * v[s:s + 1]
        o = o + (upd if t0 == 0 else jnp.concatenate([jnp.zeros((t0, HG_DIM), F32), upd], axis=0))
    return o


def _col_bcast(rowvec):
    return jnp.transpose(jnp.broadcast_to(rowvec, (HG_DIM, HG_DIM)))


def _hgrn_chunk_exact(q, k, v, b, st):
    c = HG_CHUNK
    o = _dot((q * jnp.exp(b)).astype(BF16), st.astype(BF16))
    b_last = b[c - 1:c]
    kd = (k * jnp.exp(b_last - b)).astype(BF16)
    vb = v.astype(BF16)
    s_new = _col_bcast(jnp.exp(b_last)) * st + _dot_tn(kd, vb)
    parts = []
    for i in range(c // HG_SUB):
        r0 = i * HG_SUB
        od = _hgrn_diag(q[r0:r0 + HG_SUB], k[r0:r0 + HG_SUB], v[r0:r0 + HG_SUB], b[r0:r0 + HG_SUB], HG_SUB)
        if i > 0:
            g_i = b[r0 - 1:r0]
            qi = (q[r0:r0 + HG_SUB] * jnp.exp(b[r0:r0 + HG_SUB] - g_i)).astype(BF16)
            ki = (k[0:r0] * jnp.exp(g_i - b[0:r0])).astype(BF16)
            a = _dot_nt(qi, ki)
            od = od + _dot(a.astype(BF16), vb[0:r0])
        parts.append(od)
    return o + jnp.concatenate(parts, axis=0), s_new


def _rms_gate(o, gate, nw):
    outs = []
    for h in range(HG_HEADS):
        oh = o[:, h * HG_DIM:(h + 1) * HG_DIM]
        ms = jnp.mean(oh * oh, axis=-1, keepdims=True)
        outs.append(oh * lax.rsqrt(ms + RMS_EPS) * nw * gate[:, h * HG_DIM:(h + 1) * HG_DIM])
    return jnp.concatenate(outs, axis=-1)


def _branch_gate(xb, j, w_in_ref):
    c0 = OFF_GATE + j * D_MODEL
    return _sigmoid(_dot(xb, w_in_ref[:, c0:c0 + D_MODEL]))


def _out_ffn(xn, mix, w_o_ref, w_f1_ref, w_f2_ref, lnp_ref):
    h = _layer_norm(ALPHA * xn + _dot(mix.astype(BF16), w_o_ref[...]), lnp_ref[2:3], lnp_ref[3:4])
    hb = h.astype(BF16)
    acc = None
    for c0 in range(0, D_FF, FFN_CHUNK):
        c1 = min(c0 + FFN_CHUNK, D_FF)
        gt = _dot(hb, w_f1_ref[:, c0:c1])
        up = _dot(hb, w_f1_ref[:, D_FF + c0:D_FF + c1])
        part = _dot((_silu(gt) * up).astype(BF16), w_f2_ref[c0:c1, :])
        acc = part if acc is None else acc + part
    return _layer_norm(ALPHA * h + acc, lnp_ref[4:5], lnp_ref[5:6])


def _hgrn_inputs(hp, lb):
    f = lb + (1.0 - lb) * _sigmoid(hp[:, HG_W:2 * HG_W])
    return (_silu(hp[:, 0:HG_W]), 1.0 - f, hp[:, 2 * HG_W:3 * HG_W], jnp.log(f), _silu(hp[:, 3 * HG_W:4 * HG_W]))


def _prompt_bias():
    i = np.arange(WINDOW)[:, None]
    j = np.arange(2 * WINDOW)[None, :]
    dist = i + WINDOW - j
    valid = (dist >= 0) & (dist < WINDOW)
    out = np.zeros((SW_KV_HEADS, SW_GROUP, WINDOW, 2 * WINDOW), np.float32)
    for h in range(SW_HEADS):
        slope = np.float32(2.0) ** np.float32(-8.0 * (h + 1) / SW_HEADS)
        out[h // SW_GROUP, h % SW_GROUP] = np.where(valid, -slope * dist.astype(np.float32), np.float32(NEG))
    return out.reshape(SW_KV_HEADS, SW_GROUP * WINDOW, 2 * WINDOW)


def _prompt_kernel(sinks_ref, x_ref, mk_ref, mv_ref, bias_ref, tri_ref, lnp_ref, hlb_ref, hnw_ref,
                   w_in_ref, w_up_ref, w_o_ref, w_f1_ref, w_f2_ref,
                   y_ref, nk_ref, nv_ref, s_ref,
                   kd_ref, vd_ref, st_ref, sto_ref, hq_ref, hk_ref, hv_ref, hb_ref, oh_ref, *, tm, nt, fallback):
    t = pl.program_id(1)

    @pl.when(t == 0)
    def _():
        kd_ref[:, 0:WINDOW, :] = jnp.zeros((SW_KV_HEADS, WINDOW, LANES), BF16)
        vd_ref[:, 0:WINDOW, :] = jnp.zeros((SW_KV_HEADS, WINDOW, LANES), BF16)
        st_ref[...] = jnp.zeros_like(st_ref)

    half = tm // 2
    xn_h = [_layer_norm(x_ref[0, r0:r0 + half, :], lnp_ref[0:1], lnp_ref[1:2]) for r0 in (0, half)]
    qkv = jnp.concatenate([_dot(xh.astype(BF16), w_in_ref[:, OFF_SW:OFF_HG]) for xh in xn_h], axis=0)
    xn = jnp.concatenate(xn_h, axis=0)
    xb = xn.astype(BF16)
    q = qkv[:, 0:SW_Q] * (SW_HEAD_DIM ** -0.5)
    k = qkv[:, SW_Q:SW_Q + SW_KV]
    v = qkv[:, SW_Q + SW_KV:SW_Q + 2 * SW_KV]

    nk_ref[0] = k[tm - WINDOW:tm]
    nv_ref[0] = v[tm - WINDOW:tm]

    for g, dup in enumerate(_dup_heads(k)):
        kd_ref[g, WINDOW:WINDOW + tm, :] = dup.astype(BF16)
    for g, dup in enumerate(_dup_heads(v)):
        vd_ref[g, WINDOW:WINDOW + tm, :] = dup.astype(BF16)

    col = lax.broadcasted_iota(jnp.int32, (1, 2 * WINDOW), 1)
    first = jnp.where(jnp.logical_and(t == 0, col < WINDOW), NEG, 0.0).astype(F32)
    def proj(c0, width):
        return _dot(xb, w_in_ref[:, c0:c0 + width])

    hp_parts = []
    o_rows = []
    for j in range(tm // WINDOW):
        qj = q[j * WINDOW:(j + 1) * WINDOW]
        blocks = []
        for g in range(SW_KV_HEADS):
            kk = kd_ref[g, j * WINDOW:(j + 2) * WINDOW, :]
            vv = vd_ref[g, j * WINDOW:(j + 2) * WINDOW, :]
            s = _dot_nt(_stack_group_queries(qj, g), kk) + bias_ref[g]
            if j == 0:
                s = s + first
            if len(hp_parts) < 4:
                hp_parts.append(proj(OFF_HG + len(hp_parts) * HG_W, HG_W))
            blocks.extend(_sink_softmax_pv(s, vv, sinks_ref, g, WINDOW))
        o_rows.append(jnp.concatenate(blocks, axis=-1))
    o_sw = jnp.concatenate(o_rows, axis=0)
    while len(hp_parts) < 4:
        hp_parts.append(proj(OFF_HG + len(hp_parts) * HG_W, HG_W))
    kd_ref[:, 0:WINDOW, :] = kd_ref[:, tm:tm + WINDOW, :]
    vd_ref[:, 0:WINDOW, :] = vd_ref[:, tm:tm + WINDOW, :]

    hq, hk, hv, hl, hgate = _hgrn_inputs(jnp.concatenate(hp_parts, axis=-1), _lower_bound(hlb_ref[...]))
    hb = _exact_dot(tri_ref[...], hl)
    mq = proj(OFF_MX, MX_W)
    gate_sw = _branch_gate(xb, 0, w_in_ref)
    if fallback:
        sto_ref[...] = st_ref[...]
    qe = (hq * jnp.exp(hb)).astype(BF16)
    ke = (hk * jnp.exp(-hb)).astype(BF16)
    vb = hv.astype(BF16)
    ri = lax.broadcasted_iota(jnp.int32, (HG_CHUNK, HG_CHUNK), 0)
    ci = lax.broadcasted_iota(jnp.int32, (HG_CHUNK, HG_CHUNK), 1)
    causal = ri >= ci
    n_chunks = tm // HG_CHUNK
    heads = [slice(h * HG_DIM, (h + 1) * HG_DIM) for h in range(HG_HEADS)]
    chunks = [slice(c * HG_CHUNK, (c + 1) * HG_CHUNK) for c in range(n_chunks)]
    a_mat, upd, dec = {}, {}, []
    for c, rows in enumerate(chunks):
        b_last = hb[(c + 1) * HG_CHUNK - 1:(c + 1) * HG_CHUNK]
        kd = (hk[rows] * jnp.exp(b_last - hb[rows])).astype(BF16)
        dec.append(jnp.exp(b_last))
        for h, sl in enumerate(heads):
            a_mat[c, h] = jnp.where(causal, _dot_nt(qe[rows, sl], ke[rows, sl]), 0.0).astype(BF16)
            upd[c, h] = _dot_tn(kd[:, sl], vb[rows, sl])
    gate_mx = _branch_gate(xb, 2, w_in_ref)
    st = {(0, h): st_ref[h] for h in range(HG_HEADS)}
    for c in range(n_chunks):
        for h, sl in enumerate(heads):
            st[c + 1, h] = _col_bcast(dec[c][:, sl]) * st[c, h] + upd[c, h]
    for h in range(HG_HEADS):
        st_ref[h] = st[n_chunks, h]
    for c, rows in enumerate(chunks):
        for h, sl in enumerate(heads):
            oh_ref[rows, sl] = _dot(jnp.concatenate([qe[rows, sl], a_mat[c, h]], axis=1),
                                    jnp.concatenate([st[c, h].astype(BF16), vb[rows, sl]], axis=0))
    up_sw = _dot(o_sw.astype(BF16), w_up_ref[0])

    fillers = [lambda: proj(OFF_GATE + D_MODEL, D_MODEL // 2), lambda: proj(OFF_GATE + D_MODEL + D_MODEL // 2, D_MODEL // 2)]
    filled = []
    mk = mk_ref[0]
    mv = mv_ref[0]
    outs = []
    for h in range(MX_HEADS):
        sl = slice(h * MX_HEAD_DIM, (h + 1) * MX_HEAD_DIM)
        s = _dot_nt(mq[:, sl].astype(BF16), mk[:, sl]) * (MX_HEAD_DIM ** -0.5)
        if h % 2 == 0:
            filled.append(fillers[h // 2]())
        p = jnp.exp(s - jnp.max(s, axis=-1, keepdims=True))
        den = jnp.sum(p, axis=-1, keepdims=True)
        outs.append(_dot(p.astype(BF16), mv[:, sl]) / den)
    o_m = jnp.concatenate(outs, axis=-1)
    gate_h = _sigmoid(jnp.concatenate(filled, axis=-1))
    mix = gate_sw * up_sw + gate_mx * _dot(o_m.astype(BF16), w_up_ref[2])

    def redo_tile_exact():
        hq_ref[...] = hq
        hk_ref[...] = hk
        hv_ref[...] = hv
        hb_ref[...] = hb

        def chunk_body(c, carry):
            rows = pl.ds(pl.multiple_of(c * HG_CHUNK, HG_CHUNK), HG_CHUNK)
            for h in range(HG_HEADS):
                sl = slice(h * HG_DIM, (h + 1) * HG_DIM)
                o, st_new = _hgrn_chunk_exact(hq_ref[rows, sl], hk_ref[rows, sl], hv_ref[rows, sl], hb_ref[rows, sl],
                                              sto_ref[h])
                sto_ref[h] = st_new
                oh_ref[rows, sl] = o
            return carry

        lax.fori_loop(0, tm // HG_CHUNK, chunk_body, 0)
        st_ref[...] = sto_ref[...]

    if fallback:
        pl.when(jnp.min(hb) < -HG_SAFE_RANGE)(redo_tile_exact)

    mix = mix + gate_h * _dot(_rms_gate(oh_ref[...], hgate, hnw_ref[...]).astype(BF16), w_up_ref[1])
    y_ref[0] = _out_ffn(xn, mix, w_o_ref, w_f1_ref, w_f2_ref, lnp_ref)

    @pl.when(t == nt - 1)
    def _():
        for h in range(HG_HEADS):
            s_ref[0, h] = st_ref[h]


def _full(shape):
    nd = len(shape)
    return pl.BlockSpec(shape, lambda *_: (0,) * nd, pipeline_mode=pl.Buffered(1))


def _prompt_layer(x, mk_b, mv_b, sinks, lnp, hlb, hnw, w_in, w_up, w_o, w_f1, w_f2, *, tm, fallback):
    bsz, seq, _ = x.shape
    nt = seq // tm
    assert seq % tm == 0 and tm % WINDOW == 0 and tm % HG_CHUNK == 0
    bias = jnp.asarray(_prompt_bias())
    r = np.arange(tm)
    tri = jnp.asarray((r[:, None] // HG_CHUNK == r[None, :] // HG_CHUNK) & (r[:, None] >= r[None, :]), BF16)
    kern = functools.partial(_prompt_kernel, tm=tm, nt=nt, fallback=fallback)
    return pl.pallas_call(
        kern,
        grid=(bsz, nt),
        in_specs=[
            pl.BlockSpec(memory_space=pltpu.SMEM),
            pl.BlockSpec((1, tm, D_MODEL), lambda b, t: (b, t, 0)),
            pl.BlockSpec((1, MEM_LEN, MX_W), lambda b, t: (b, 0, 0)),
            pl.BlockSpec((1, MEM_LEN, MX_W), lambda b, t: (b, 0, 0)),
            _full(bias.shape), _full(tri.shape), _full(lnp.shape), _full(hlb.shape), _full(hnw.shape),
            _full(w_in.shape), _full(w_up.shape), _full(w_o.shape), _full(w_f1.shape), _full(w_f2.shape),
        ],
        out_specs=[
            pl.BlockSpec((1, tm, D_MODEL), lambda b, t: (b, t, 0)),
            pl.BlockSpec((1, WINDOW, SW_KV), lambda b, t: (b, 0, 0)),
            pl.BlockSpec((1, WINDOW, SW_KV), lambda b, t: (b, 0, 0)),
            pl.BlockSpec((1, HG_HEADS, HG_DIM, HG_DIM), lambda b, t: (b, 0, 0, 0)),
        ],
        out_shape=[
            jax.ShapeDtypeStruct((bsz, seq, D_MODEL), F32),
            jax.ShapeDtypeStruct((bsz, WINDOW, SW_KV), F32),
            jax.ShapeDtypeStruct((bsz, WINDOW, SW_KV), F32),
            jax.ShapeDtypeStruct((bsz, HG_HEADS, HG_DIM, HG_DIM), F32),
        ],
        scratch_shapes=[
            pltpu.VMEM((SW_KV_HEADS, WINDOW + tm, LANES), BF16),
            pltpu.VMEM((SW_KV_HEADS, WINDOW + tm, LANES), BF16),
            pltpu.VMEM((HG_HEADS, HG_DIM, HG_DIM), F32),
            pltpu.VMEM((HG_HEADS, HG_DIM, HG_DIM), F32),
            pltpu.VMEM((tm, HG_W), F32),
            pltpu.VMEM((tm, HG_W), F32),
            pltpu.VMEM((tm, HG_W), F32),
            pltpu.VMEM((tm, HG_W), F32),
            pltpu.VMEM((tm, HG_W), F32),
        ],
        compiler_params=pltpu.CompilerParams(
            dimension_semantics=("arbitrary", "arbitrary"), vmem_limit_bytes=VMEM_LIMIT_BYTES),
        name="prompt_layer",
    )(sinks, x, mk_b, mv_b, bias, tri, lnp, hlb, hnw, w_in, w_up, w_o, w_f1, w_f2)


def _mem_kv_kernel(m_ref, w_ref, mk_ref, mv_ref, mkb_ref, mvb_ref):
    kv = _dot(m_ref[0].astype(BF16), w_ref[...])
    for h in range(MX_HEADS):
        head_rows = pl.ds(h, MEM_LEN, stride=MX_HEADS)
        mk_ref[0, head_rows, :] = kv[:, h * MX_HEAD_DIM:(h + 1) * MX_HEAD_DIM]
        mv_ref[0, head_rows, :] = kv[:, MX_W + h * MX_HEAD_DIM:MX_W + (h + 1) * MX_HEAD_DIM]
    mkb_ref[0] = kv[:, 0:MX_W].astype(BF16)
    mvb_ref[0] = kv[:, MX_W:2 * MX_W].astype(BF16)


def _mem_kv(mem, w_kv):
    bsz = mem.shape[0]
    blk = pl.BlockSpec((1, MEM_LEN, MX_W), lambda b: (b, 0, 0))
    row_blk = pl.BlockSpec((1, MEM_LEN * MX_HEADS, MX_HEAD_DIM), lambda b: (b, 0, 0))
    return pl.pallas_call(
        _mem_kv_kernel,
        grid=(bsz,),
        in_specs=[pl.BlockSpec((1, MEM_LEN, D_MODEL), lambda b: (b, 0, 0)), _full(w_kv.shape)],
        out_specs=[row_blk, row_blk, blk, blk],
        out_shape=[jax.ShapeDtypeStruct((bsz, MEM_LEN * MX_HEADS, MX_HEAD_DIM), F32)] * 2
        + [jax.ShapeDtypeStruct((bsz, MEM_LEN, MX_W), BF16)] * 2,
        compiler_params=pltpu.CompilerParams(dimension_semantics=("arbitrary",)),
        name="mem_kv",
    )(mem, w_kv)


def _prep_params(p):
    assert p['w_in'].shape[0] == DEPTH == 1
    zeros = jnp.zeros((D_MODEL,), F32)
    return dict(
        lnp=jnp.stack([p['ln0_w'], p['ln0_b'], p['ln1_w'][0], p['ln1_b'][0], p['ln2_w'][0], p['ln2_b'][0],
                       zeros, zeros]).astype(F32),
        sinks=p['sw_sinks'][0].astype(F32),
        hlb=p['hg_lower_bound'].astype(F32),
        hnw=p['hg_norm_w'].astype(F32),
        w_in=p['w_in'][0].astype(BF16),
        w_up=jnp.stack([p['w_up_sw'][0], p['w_up_hg'][0], p['w_up_mx'][0]]).astype(BF16),
        w_o=p['w_o'][0].astype(BF16),
        w_f1=p['w_ffn_in'][0].astype(BF16),
        w_f2=p['w_ffn_out'][0].astype(BF16),
        w_kv=p['w_mem_kv'][0].astype(BF16),
    )


def _prompt_group(x, mem, p, *, tm, pp=None):
    pp = _prep_params(p) if pp is None else pp
    mk, mv, mk_b, mv_b = _mem_kv(mem, pp['w_kv'])
    args = (x, mk_b, mv_b, pp['sinks'], pp['lnp'], pp['hlb'], pp['hnw'],
            pp['w_in'], pp['w_up'], pp['w_o'], pp['w_f1'], pp['w_f2'])
    lb_min = jnp.min(jax.nn.softmax(pp['hlb'], axis=0)[0])
    never_redo = HG_CHUNK * jnp.log(lb_min) >= -(HG_SAFE_RANGE - HG_BOUND_MARGIN)
    y, nk, nv, s = lax.cond(never_redo,
                            functools.partial(_prompt_layer, tm=tm, fallback=False),
                            functools.partial(_prompt_layer, tm=tm, fallback=True), *args)
    return y, nk, nv, s, mk, mv


def _sample_proj_kernel(x_ref, lnp_ref, w_in_ref, xn_ref, proj_ref):
    xn = _layer_norm(x_ref[...], lnp_ref[0:1], lnp_ref[1:2])
    xn_ref[...] = xn
    proj_ref[...] = _dot(xn.astype(BF16), w_in_ref[:, 0:OFF_GATE])


def _sample_proj(x2, lnp, w_in, *, tr):
    rows = x2.shape[0]
    return pl.pallas_call(
        _sample_proj_kernel,
        grid=(rows // tr,),
        in_specs=[pl.BlockSpec((tr, D_MODEL), lambda i: (i, 0)), _full(lnp.shape), _full(w_in.shape)],
        out_specs=[pl.BlockSpec((tr, D_MODEL), lambda i: (i, 0)), pl.BlockSpec((tr, OFF_GATE), lambda i: (i, 0))],
        out_shape=[jax.ShapeDtypeStruct((rows, D_MODEL), F32), jax.ShapeDtypeStruct((rows, OFF_GATE), F32)],
        compiler_params=pltpu.CompilerParams(
            dimension_semantics=("arbitrary",), vmem_limit_bytes=VMEM_LIMIT_BYTES),
        name="sample_proj",
    )(x2, lnp, w_in)


def _sample_bias(t_new, n_keys):
    i = np.arange(t_new)[:, None]
    j = np.arange(n_keys)[None, :]
    kpos = np.where(j < WINDOW, PAST_LEN - WINDOW + j, PAST_LEN + j - WINDOW)
    dist = PAST_LEN + i - kpos
    valid = (dist >= 0) & (dist < WINDOW) & (j < WINDOW + t_new)
    out = np.zeros((SW_KV_HEADS, SW_GROUP, t_new, n_keys), np.float32)
    for h in range(SW_HEADS):
        slope = np.float32(2.0) ** np.float32(-8.0 * (h + 1) / SW_HEADS)
        out[h // SW_GROUP, h % SW_GROUP] = np.where(valid, -slope * dist.astype(np.float32), np.float32(NEG))
    return out.reshape(SW_KV_HEADS, SW_GROUP * t_new, n_keys)


def _sample_mix_kernel(sinks_ref, proj_ref, kbuf_ref, vbuf_ref, st_ref, mk_ref, mv_ref, bias_ref, hlb_ref, hnw_ref,
                       osw_ref, oh_ref, om_ref, nk_ref, nv_ref, ns_ref,
                       kk_ref, vv_ref, *, bt, t_new):
    nkeys = 2 * WINDOW
    r = bt * t_new
    elems = [slice(i * t_new, (i + 1) * t_new) for i in range(bt)]
    heads = [slice(h * HG_DIM, (h + 1) * HG_DIM) for h in range(HG_HEADS)]
    pad_rows = BF16_SUBLANES - t_new

    def pad(a):
        return jnp.concatenate([a, jnp.zeros((pad_rows, a.shape[1]), F32)], axis=0)

    def mem_head(ref, i, h):
        return ref[i, pl.ds(h, MEM_LEN, stride=MX_HEADS), :].astype(BF16)

    hq, hk, hv, hl, hgate = _hgrn_inputs(proj_ref[:, OFF_HG:OFF_MX], _lower_bound(hlb_ref[...]))
    ri = lax.broadcasted_iota(jnp.int32, (r, r), 0)
    ci = lax.broadcasted_iota(jnp.int32, (r, r), 1)
    same = (ri // t_new) == (ci // t_new)
    tri = jnp.where(jnp.logical_and(same, ri >= ci), 1.0, 0.0).astype(BF16)
    hb = _exact_dot(tri, hl)
    qe = hq * jnp.exp(hb)
    q_all = proj_ref[:, 0:SW_Q] * (SW_HEAD_DIM ** -0.5)
    mq_all = proj_ref[:, OFF_MX:OFF_GATE]

    zeros = jnp.zeros((nkeys - WINDOW - t_new, LANES), F32)
    for i, rows in enumerate(elems):
        for buf_ref, cache_ref, c0, out_ref in ((kk_ref, kbuf_ref, SW_Q, nk_ref),
                                                (vv_ref, vbuf_ref, SW_Q + SW_KV, nv_ref)):
            buf_ref[i, 0:WINDOW, :] = cache_ref[i]
            buf_ref[i, WINDOW:WINDOW + t_new, :] = proj_ref[rows, c0:c0 + SW_KV]
            buf_ref[i, WINDOW + t_new:nkeys, :] = zeros
            out_ref[i] = buf_ref[i, t_new:t_new + WINDOW, :]

    s_att, s_mem, o_state, upd, b_last = {}, {}, {}, {}, {}
    for i, rows in enumerate(elems):
        kds = _dup_heads(kk_ref[i])
        for g in range(SW_KV_HEADS):
            s_att[i, g] = _dot_nt(_stack_group_queries(q_all[rows], g), kds[g].astype(BF16)) + bias_ref[g]
    for i, rows in enumerate(elems):
        mq = pad(mq_all[rows]).astype(BF16)
        for h, sl in enumerate(heads):
            s_mem[i, h] = _dot_nt(mq[:, sl], mem_head(mk_ref, i, h)) * (MX_HEAD_DIM ** -0.5)
    for i, rows in enumerate(elems):
        b_last[i] = hb[(i + 1) * t_new - 1:(i + 1) * t_new]
        kd = pad(hk[rows] * jnp.exp(b_last[i] - hb[rows])).astype(BF16)
        vb = pad(hv[rows]).astype(BF16)
        qb = pad(qe[rows]).astype(BF16)
        for h, sl in enumerate(heads):
            o_state[i, h] = _dot(qb[:, sl], st_ref[i, h].astype(BF16))[0:t_new]
            upd[i, h] = _dot_tn(kd[:, sl], vb[:, sl])

    p_att, den_att, p_mem, den_mem = {}, {}, {}, {}
    for i in range(bt):
        for g in range(SW_KV_HEADS):
            ps, dens = [], []
            for hh in range(SW_GROUP):
                sh = s_att[i, g][hh * t_new:(hh + 1) * t_new]
                sink = sinks_ref[g * SW_GROUP + hh]
                m = jnp.maximum(jnp.max(sh, axis=-1, keepdims=True), sink)
                p = jnp.exp(sh - m)
                dens.append(jnp.sum(p, axis=-1, keepdims=True) + jnp.exp(sink - m))
                ps.append(p)
            p_att[i, g] = jnp.concatenate(ps, axis=0).astype(BF16)
            den_att[i, g] = jnp.concatenate(dens, axis=0)
        for h in range(MX_HEADS):
            p = jnp.exp(s_mem[i, h] - jnp.max(s_mem[i, h], axis=-1, keepdims=True))
            den_mem[i, h] = jnp.sum(p, axis=-1, keepdims=True)
            p_mem[i, h] = p.astype(BF16)
    for i, rows in enumerate(elems):
        dec = jnp.exp(b_last[i])
        for h, sl in enumerate(heads):
            ns_ref[i, h] = _col_bcast(dec[:, sl]) * st_ref[i, h] + upd[i, h]
            oh_ref[rows, sl] = o_state[i, h] + _hgrn_diag(hq[rows, sl], hk[rows, sl], hv[rows, sl], hb[rows, sl],
                                                          t_new)

    lo = _lane_lo((t_new, LANES))
    for i, rows in enumerate(elems):
        vds = _dup_heads(vv_ref[i])
        blocks = []
        for g in range(SW_KV_HEADS):
            o = _dot(p_att[i, g], vds[g].astype(BF16)) / den_att[i, g]
            blocks.append(jnp.where(lo, o[0:t_new], o[t_new:2 * t_new]))
            blocks.append(jnp.where(lo, o[2 * t_new:3 * t_new], o[3 * t_new:4 * t_new]))
        osw_ref[rows, :] = jnp.concatenate(blocks, axis=-1)
    for i, rows in enumerate(elems):
        outs = [(_dot(p_mem[i, h], mem_head(mv_ref, i, h)) / den_mem[i, h])[0:t_new] for h in range(MX_HEADS)]
        om_ref[rows, :] = jnp.concatenate(outs, axis=-1)
    oh_ref[...] = _rms_gate(oh_ref[...], hgate, hnw_ref[...])


def _sample_mix(proj, kbuf, vbuf, st, mk, mv, sinks, hlb, hnw, *, bt, t_new):
    nb = kbuf.shape[0]
    r = bt * t_new
    assert nb % bt == 0 and t_new == 8 and r % BF16_SUBLANES == 0
    bias = jnp.asarray(_sample_bias(t_new, 2 * WINDOW))
    kern = functools.partial(_sample_mix_kernel, bt=bt, t_new=t_new)
    row_blk = pl.BlockSpec((r, HG_W), lambda i: (i, 0))
    kv_blk = pl.BlockSpec((bt, WINDOW, SW_KV), lambda i: (i, 0, 0))
    st_blk = pl.BlockSpec((bt, HG_HEADS, HG_DIM, HG_DIM), lambda i: (i, 0, 0, 0))
    mem_blk = pl.BlockSpec((bt, MEM_LEN * MX_HEADS, MX_HEAD_DIM), lambda i: (i, 0, 0))
    return pl.pallas_call(
        kern,
        grid=(nb // bt,),
        in_specs=[pl.BlockSpec(memory_space=pltpu.SMEM),
                  pl.BlockSpec((r, OFF_GATE), lambda i: (i, 0)),
                  kv_blk, kv_blk, st_blk, mem_blk, mem_blk,
                  _full(bias.shape), _full(hlb.shape), _full(hnw.shape)],
        out_specs=[row_blk, row_blk, row_blk, kv_blk, kv_blk, st_blk],
        out_shape=[jax.ShapeDtypeStruct((nb * t_new, HG_W), F32)] * 3
        + [jax.ShapeDtypeStruct((nb, WINDOW, SW_KV), F32)] * 2
        + [jax.ShapeDtypeStruct((nb, HG_HEADS, HG_DIM, HG_DIM), F32)],
        scratch_shapes=[pltpu.VMEM((bt, 2 * WINDOW, LANES), F32)] * 2,
        compiler_params=pltpu.CompilerParams(
            dimension_semantics=("arbitrary",), vmem_limit_bytes=VMEM_LIMIT_BYTES),
        name="sample_mix",
    )(sinks, proj, kbuf, vbuf, st, mk, mv, bias, hlb, hnw)


def _sample_out_kernel(xn_ref, osw_ref, oh_ref, om_ref, lnp_ref, w_in_ref, w_up_ref, w_o_ref, w_f1_ref, w_f2_ref,
                       y_ref):
    xn = xn_ref[...]
    xb = xn.astype(BF16)
    mix = None
    for j, br_ref in enumerate((osw_ref, oh_ref, om_ref)):
        term = _branch_gate(xb, j, w_in_ref) * _dot(br_ref[...].astype(BF16), w_up_ref[j])
        mix = term if mix is None else mix + term
    y_ref[...] = _out_ffn(xn, mix, w_o_ref, w_f1_ref, w_f2_ref, lnp_ref)


def _sample_out(xn, osw, oh, om, lnp, w_in, w_up, w_o, w_f1, w_f2, *, tr):
    rows = xn.shape[0]
    wide = pl.BlockSpec((tr, D_MODEL), lambda i: (i, 0))
    half = pl.BlockSpec((tr, HG_W), lambda i: (i, 0))
    return pl.pallas_call(
        _sample_out_kernel,
        grid=(rows // tr,),
        in_specs=[wide, half, half, half, _full(lnp.shape), _full(w_in.shape), _full(w_up.shape),
                  _full(w_o.shape), _full(w_f1.shape), _full(w_f2.shape)],
        out_specs=wide,
        out_shape=jax.ShapeDtypeStruct((rows, D_MODEL), F32),
        compiler_params=pltpu.CompilerParams(
            dimension_semantics=("arbitrary",), vmem_limit_bytes=VMEM_LIMIT_BYTES),
        name="sample_out",
    )(xn, osw, oh, om, lnp, w_in, w_up, w_o, w_f1, w_f2)


def _sample_group(x, p, *, bt, pp=None):
    pp = _prep_params(p) if pp is None else pp
    nb, t_new, _ = x.shape
    rows = nb * t_new
    tr = min(rows, 256)
    xn, proj = _sample_proj(x.reshape(rows, D_MODEL), pp['lnp'], pp['w_in'], tr=tr)
    osw, oh, om, nk, nv, ns = _sample_mix(
        proj,
        p['cache_k_win'][0].reshape(nb, WINDOW, SW_KV), p['cache_v_win'][0].reshape(nb, WINDOW, SW_KV),
        p['state_hgrn'][0],
        p['cache_mem_k'][0].reshape(nb, MEM_LEN * MX_HEADS, MX_HEAD_DIM),
        p['cache_mem_v'][0].reshape(nb, MEM_LEN * MX_HEADS, MX_HEAD_DIM),
        pp['sinks'], pp['hlb'], pp['hnw'], bt=bt, t_new=t_new)
    y = _sample_out(xn, osw, oh, om, pp['lnp'], pp['w_in'], pp['w_up'], pp['w_o'], pp['w_f1'], pp['w_f2'], tr=tr)
    return y.reshape(nb, t_new, D_MODEL), nk, nv, ns


def kernel(x_prompt, x_sample, cache_k_win, cache_v_win, state_hgrn, cache_mem_k, cache_mem_v, mem_prompt,
           ln0_w, ln0_b, w_in, w_up_sw, w_up_hg, w_up_mx, sw_sinks, hg_lower_bound, hg_norm_w, w_mem_kv,
           w_o, ln1_w, ln1_b, w_ffn_in, w_ffn_out, ln2_w, ln2_b):
    p = dict(ln0_w=ln0_w, ln0_b=ln0_b, w_in=w_in, w_up_sw=w_up_sw, w_up_hg=w_up_hg, w_up_mx=w_up_mx,
             sw_sinks=sw_sinks, hg_lower_bound=hg_lower_bound, hg_norm_w=hg_norm_w, w_mem_kv=w_mem_kv,
             w_o=w_o, ln1_w=ln1_w, ln1_b=ln1_b, w_ffn_in=w_ffn_in, w_ffn_out=w_ffn_out, ln2_w=ln2_w, ln2_b=ln2_b)
    p.update(cache_k_win=cache_k_win, cache_v_win=cache_v_win, state_hgrn=state_hgrn,
             cache_mem_k=cache_mem_k, cache_mem_v=cache_mem_v)
    pp = _prep_params(p)
    bp, bs = x_prompt.shape[0], x_sample.shape[0]
    y, nk, nv, s, mk, mv = _prompt_group(x_prompt, mem_prompt, p, tm=PROMPT_TILE, pp=pp)
    ys, nks, nvs, ss = _sample_group(x_sample, p, bt=SAMPLE_BATCH_TILE, pp=pp)
    win = (WINDOW, SW_KV_HEADS, SW_HEAD_DIM)
    mem = (MEM_LEN, MX_HEADS, MX_HEAD_DIM)
    return (y, ys,
            nk.reshape(1, bp, *win), nv.reshape(1, bp, *win), s[None],
            mk.reshape(1, bp, *mem), mv.reshape(1, bp, *mem),
            nks.reshape(1, bs, *win), nvs.reshape(1, bs, *win), ss[None])
```

```python
import functools

import numpy as np
import jax
import jax.numpy as jnp
from jax import lax
from jax.experimental import pallas as pl
from jax.experimental.pallas import tpu as pltpu

D_MODEL = 1024
DEPTH = 1
PAST_LEN = 16384
SW_HEADS, SW_KV_HEADS, SW_HEAD_DIM = 8, 2, 64
SW_GROUP = SW_HEADS // SW_KV_HEADS
WINDOW = 128
HG_HEADS, HG_DIM = 4, 128
MEM_LEN, MX_HEADS, MX_HEAD_DIM = 256, 4, 128
SW_Q = SW_HEADS * SW_HEAD_DIM
SW_KV = SW_KV_HEADS * SW_HEAD_DIM
HG_W = HG_HEADS * HG_DIM
MX_W = MX_HEADS * MX_HEAD_DIM
D_FF = 2816
ALPHA = (2.0 * DEPTH) ** 0.25
LN_EPS = 1e-5
RMS_EPS = 1e-6
OFF_SW = 0
OFF_HG = SW_Q + 2 * SW_KV
OFF_MX = OFF_HG + 4 * HG_W
OFF_GATE = OFF_MX + MX_W
IN_WIDTH = OFF_GATE + 3 * D_MODEL

LANES = 128
BF16_SUBLANES = 16
MXU_TILE = 256
FFN_CHUNK = 4 * MXU_TILE
VMEM_LIMIT_BYTES = 60 * 1024 * 1024

NEG = -1e30
HG_CHUNK = 64
HG_SUB = 16
HG_SAFE_RANGE = 80.0
HG_BOUND_MARGIN = 1.0
PROMPT_TILE = 256
SAMPLE_BATCH_TILE = 8
F32 = jnp.float32
BF16 = jnp.bfloat16


def _dot(a, b):
    return jnp.dot(a, b, preferred_element_type=F32)


def _dot_nt(a, b):
    return lax.dot_general(a, b, (((1,), (1,)), ((), ())), preferred_element_type=F32)


def _dot_tn(a, b):
    return lax.dot_general(a, b, (((0,), (0,)), ((), ())), preferred_element_type=F32)


def _exact_dot(m_bf16, x):
    hi = x.astype(BF16)
    r1 = x - hi.astype(F32)
    mid = r1.astype(BF16)
    lo = (r1 - mid.astype(F32)).astype(BF16)
    return _dot(m_bf16, hi) + _dot(m_bf16, mid) + _dot(m_bf16, lo)


def _layer_norm(x, w, b):
    mu = jnp.mean(x, axis=-1, keepdims=True)
    xc = x - mu
    var = jnp.mean(xc * xc, axis=-1, keepdims=True)
    return xc * lax.rsqrt(var + LN_EPS) * w + b


def _sigmoid(x):
    return 1.0 / (1.0 + jnp.exp(-x))


def _silu(x):
    return x * _sigmoid(x)


def _lower_bound(hlb):
    m = jnp.max(hlb, axis=0, keepdims=True)
    e = jnp.exp(hlb - m)
    return e[0:1] / jnp.sum(e, axis=0, keepdims=True)


def _lane_lo(shape):
    return lax.broadcasted_iota(jnp.int32, shape, len(shape) - 1) < SW_HEAD_DIM


def _dup_heads(kv):
    rolled = pltpu.roll(kv, SW_HEAD_DIM, axis=1)
    lo = _lane_lo(kv.shape)
    return jnp.where(lo, kv, rolled), jnp.where(lo, rolled, kv)


def _sink_softmax_pv(s, vv, sinks_ref, g, rows):
    ps, dens = [], []
    for hh in range(SW_GROUP):
        sh = s[hh * rows:(hh + 1) * rows]
        sink = sinks_ref[g * SW_GROUP + hh]
        m = jnp.maximum(jnp.max(sh, axis=-1, keepdims=True), sink)
        p = jnp.exp(sh - m)
        dens.append(jnp.sum(p, axis=-1, keepdims=True) + jnp.exp(sink - m))
        ps.append(p)
    o = _dot(jnp.concatenate(ps, axis=0).astype(BF16), vv)
    o = o / jnp.concatenate(dens, axis=0)
    lo = _lane_lo((rows, LANES))
    return (jnp.where(lo, o[0:rows], o[rows:2 * rows]),
            jnp.where(lo, o[2 * rows:3 * rows], o[3 * rows:4 * rows]))


def _stack_group_queries(q, g):
    qa = q[:, 2 * g * LANES:(2 * g + 1) * LANES]
    qb = q[:, (2 * g + 1) * LANES:(2 * g + 2) * LANES]
    lo = _lane_lo(qa.shape)
    z = jnp.zeros_like(qa)
    return jnp.concatenate([jnp.where(lo, qa, z), jnp.where(lo, z, qa),
                            jnp.where(lo, qb, z), jnp.where(lo, z, qb)], axis=0).astype(BF16)


def _hgrn_diag(q, k, v, b, n):
    row = lax.broadcasted_iota(jnp.int32, (n, HG_DIM), 0)
    o = jnp.zeros((n, HG_DIM), F32)
    for s in range(n):
        t0 = (s // 8) * 8
        rr = row[t0:]
        e = jnp.exp(jnp.where(rr >= s, b[t0:] - b[s:s + 1], -jnp.inf))
        col = jnp.sum(q[t0:] * e * k[s:s + 1], axis=-1, keepdims=True)
        upd = col * v[s:s + 1]
        o = o + (upd if t0 == 0 else jnp.concatenate([jnp.zeros((t0, HG_DIM), F32), upd], axis=0))
    return o


def _col_bcast(rowvec):
    return jnp.transpose(jnp.broadcast_to(rowvec, (HG_DIM, HG_DIM)))


def _hgrn_chunk_exact(q, k, v, b, st):
    c = HG_CHUNK
    o = _dot((q * jnp.exp(b)).astype(BF16), st.astype(BF16))
    b_last = b[c - 1:c]
    kd = (k * jnp.exp(b_last - b)).astype(BF16)
    vb = v.astype(BF16)
    s_new = _col_bcast(jnp.exp(b_last)) * st + _dot_tn(kd, vb)
    parts = []
    for i in range(c // HG_SUB):
        r0 = i * HG_SUB
        od = _hgrn_diag(q[r0:r0 + HG_SUB], k[r0:r0 + HG_SUB], v[r0:r0 + HG_SUB], b[r0:r0 + HG_SUB], HG_SUB)
        if i > 0:
            g_i = b[r0 - 1:r0]
            qi = (q[r0:r0 + HG_SUB] * jnp.exp(b[r0:r0 + HG_SUB] - g_i)).astype(BF16)
            ki = (k[0:r0] * jnp.exp(g_i - b[0:r0])).astype(BF16)
            a = _dot_nt(qi, ki)
            od = od + _dot(a.astype(BF16), vb[0:r0])
        parts.append(od)
    return o + jnp.concatenate(parts, axis=0), s_new


def _rms_gate(o, gate, nw):
    outs = []
    for h in range(HG_HEADS):
        oh = o[:, h * HG_DIM:(h + 1) * HG_DIM]
        ms = jnp.mean(oh * oh, axis=-1, keepdims=True)
        outs.append(oh * lax.rsqrt(ms + RMS_EPS) * nw * gate[:, h * HG_DIM:(h + 1) * HG_DIM])
    return jnp.concatenate(outs, axis=-1)


def _branch_gate(xb, j, w_in_ref):
    c0 = OFF_GATE + j * D_MODEL
    return _sigmoid(_dot(xb, w_in_ref[:, c0:c0 + D_MODEL]))


def _out_ffn(xn, mix, w_o_ref, w_f1_ref, w_f2_ref, lnp_ref, before_last=None):
    h = _layer_norm(ALPHA * xn + _dot(mix.astype(BF16), w_o_ref[...]), lnp_ref[2:3], lnp_ref[3:4])
    hb = h.astype(BF16)
    acc = None
    for c0 in range(0, D_FF, FFN_CHUNK):
        c1 = min(c0 + FFN_CHUNK, D_FF)
        gt = _dot(hb, w_f1_ref[:, c0:c1])
        up = _dot(hb, w_f1_ref[:, D_FF + c0:D_FF + c1])
        if c1 == D_FF and before_last is not None:
            before_last()
        part = _dot((_silu(gt) * up).astype(BF16), w_f2_ref[c0:c1, :])
        acc = part if acc is None else acc + part
    return _layer_norm(ALPHA * h + acc, lnp_ref[4:5], lnp_ref[5:6])


def _hgrn_inputs(hp, lb):
    f = lb + (1.0 - lb) * _sigmoid(hp[:, HG_W:2 * HG_W])
    return (_silu(hp[:, 0:HG_W]), 1.0 - f, hp[:, 2 * HG_W:3 * HG_W], jnp.log(f), _silu(hp[:, 3 * HG_W:4 * HG_W]))


def _prompt_bias():
    i = np.arange(WINDOW)[:, None]
    j = np.arange(2 * WINDOW)[None, :]
    dist = i + WINDOW - j
    valid = (dist >= 0) & (dist < WINDOW)
    out = np.zeros((SW_KV_HEADS, SW_GROUP, WINDOW, 2 * WINDOW), np.float32)
    for h in range(SW_HEADS):
        slope = np.float32(2.0) ** np.float32(-8.0 * (h + 1) / SW_HEADS)
        out[h // SW_GROUP, h % SW_GROUP] = np.where(valid, -slope * dist.astype(np.float32), np.float32(NEG))
    return out.reshape(SW_KV_HEADS, SW_GROUP * WINDOW, 2 * WINDOW)


def _prompt_kernel(sinks_ref, x0_ref, xnext_ref, mk_ref, mv_ref, bias_ref, tri_ref, lnp_ref, hlb_ref, hnw_ref,
                   w_in_ref, w_up_ref, w_o_ref, w_f1_ref, w_f2_ref,
                   y_ref, nk_ref, nv_ref, s_ref,
                   kd_ref, vd_ref, st_ref, sto_ref, hq_ref, hk_ref, hv_ref, hb_ref, oh_ref, xn_sc, qkv_sc,
                   *, tm, nt, fallback):
    s_idx = pl.program_id(0)
    t = lax.rem(s_idx, nt)

    def ln0_q(x):
        x_norm = _layer_norm(x, lnp_ref[0:1], lnp_ref[1:2])
        xn_sc[...] = x_norm
        qkv_sc[:, 0:SW_Q] = _dot(x_norm.astype(BF16), w_in_ref[:, OFF_SW:OFF_SW + SW_Q])

    def kv_proj():
        qkv_sc[:, SW_Q:OFF_HG] = _dot(xn_sc[...].astype(BF16), w_in_ref[:, OFF_SW + SW_Q:OFF_HG])

    @pl.when(s_idx == 0)
    def _():
        ln0_q(x0_ref[0])
        kv_proj()

    @pl.when(t == 0)
    def _():
        kd_ref[:, 0:WINDOW, :] = jnp.zeros((SW_KV_HEADS, WINDOW, LANES), BF16)
        vd_ref[:, 0:WINDOW, :] = jnp.zeros((SW_KV_HEADS, WINDOW, LANES), BF16)
        st_ref[...] = jnp.zeros_like(st_ref)

    xn = xn_sc[...]
    xb = xn.astype(BF16)
    qkv = qkv_sc[...]
    q = qkv[:, 0:SW_Q] * (SW_HEAD_DIM ** -0.5)
    k = qkv[:, SW_Q:SW_Q + SW_KV]
    v = qkv[:, SW_Q + SW_KV:SW_Q + 2 * SW_KV]

    nk_ref[0] = k[tm - WINDOW:tm]
    nv_ref[0] = v[tm - WINDOW:tm]

    for g, dup in enumerate(_dup_heads(k)):
        kd_ref[g, WINDOW:WINDOW + tm, :] = dup.astype(BF16)
    for g, dup in enumerate(_dup_heads(v)):
        vd_ref[g, WINDOW:WINDOW + tm, :] = dup.astype(BF16)

    col = lax.broadcasted_iota(jnp.int32, (1, 2 * WINDOW), 1)
    first = jnp.where(jnp.logical_and(t == 0, col < WINDOW), NEG, 0.0).astype(F32)
    def proj(c0, width):
        return _dot(xb, w_in_ref[:, c0:c0 + width])

    hp_parts = []
    o_rows = []
    for j in range(tm // WINDOW):
        qj = q[j * WINDOW:(j + 1) * WINDOW]
        blocks = []
        for g in range(SW_KV_HEADS):
            kk = kd_ref[g, j * WINDOW:(j + 2) * WINDOW, :]
            vv = vd_ref[g, j * WINDOW:(j + 2) * WINDOW, :]
            s = _dot_nt(_stack_group_queries(qj, g), kk) + bias_ref[g]
            if j == 0:
                s = s + first
            if len(hp_parts) < 4:
                hp_parts.append(proj(OFF_HG + len(hp_parts) * HG_W, HG_W))
            blocks.extend(_sink_softmax_pv(s, vv, sinks_ref, g, WINDOW))
        o_rows.append(jnp.concatenate(blocks, axis=-1))
    o_sw = jnp.concatenate(o_rows, axis=0)
    while len(hp_parts) < 4:
        hp_parts.append(proj(OFF_HG + len(hp_parts) * HG_W, HG_W))
    kd_ref[:, 0:WINDOW, :] = kd_ref[:, tm:tm + WINDOW, :]
    vd_ref[:, 0:WINDOW, :] = vd_ref[:, tm:tm + WINDOW, :]

    hq, hk, hv, hl, hgate = _hgrn_inputs(jnp.concatenate(hp_parts, axis=-1), _lower_bound(hlb_ref[...]))
    hb = _exact_dot(tri_ref[...], hl)
    mq = proj(OFF_MX, MX_W)
    gate_sw = _branch_gate(xb, 0, w_in_ref)
    if fallback:
        sto_ref[...] = st_ref[...]
    qe = (hq * jnp.exp(hb)).astype(BF16)
    ke = (hk * jnp.exp(-hb)).astype(BF16)
    vb = hv.astype(BF16)
    ri = lax.broadcasted_iota(jnp.int32, (HG_CHUNK, HG_CHUNK), 0)
    ci = lax.broadcasted_iota(jnp.int32, (HG_CHUNK, HG_CHUNK), 1)
    causal = ri >= ci
    n_chunks = tm // HG_CHUNK
    heads = [slice(h * HG_DIM, (h + 1) * HG_DIM) for h in range(HG_HEADS)]
    chunks = [slice(c * HG_CHUNK, (c + 1) * HG_CHUNK) for c in range(n_chunks)]
    a_mat, upd, dec = {}, {}, []
    for c, rows in enumerate(chunks):
        b_last = hb[(c + 1) * HG_CHUNK - 1:(c + 1) * HG_CHUNK]
        kd = (hk[rows] * jnp.exp(b_last - hb[rows])).astype(BF16)
        dec.append(jnp.exp(b_last))
        for h, sl in enumerate(heads):
            a_mat[c, h] = jnp.where(causal, _dot_nt(qe[rows, sl], ke[rows, sl]), 0.0).astype(BF16)
            upd[c, h] = _dot_tn(kd[:, sl], vb[rows, sl])
    gate_mx = _branch_gate(xb, 2, w_in_ref)
    st = {(0, h): st_ref[h] for h in range(HG_HEADS)}
    for c in range(n_chunks):
        for h, sl in enumerate(heads):
            st[c + 1, h] = _col_bcast(dec[c][:, sl]) * st[c, h] + upd[c, h]
    for h in range(HG_HEADS):
        st_ref[h] = st[n_chunks, h]
    for c, rows in enumerate(chunks):
        for h, sl in enumerate(heads):
            oh_ref[rows, sl] = _dot(jnp.concatenate([qe[rows, sl], a_mat[c, h]], axis=1),
                                    jnp.concatenate([st[c, h].astype(BF16), vb[rows, sl]], axis=0))
    up_sw = _dot(o_sw.astype(BF16), w_up_ref[0])

    fillers = [lambda: proj(OFF_GATE + D_MODEL, D_MODEL // 2), lambda: proj(OFF_GATE + D_MODEL + D_MODEL // 2, D_MODEL // 2)]
    filled = []
    mk = mk_ref[0]
    mv = mv_ref[0]
    outs = []
    for h in range(MX_HEADS):
        sl = slice(h * MX_HEAD_DIM, (h + 1) * MX_HEAD_DIM)
        s = _dot_nt(mq[:, sl].astype(BF16), mk[:, sl]) * (MX_HEAD_DIM ** -0.5)
        if h % 2 == 0:
            filled.append(fillers[h // 2]())
        p = jnp.exp(s - jnp.max(s, axis=-1, keepdims=True))
        den = jnp.sum(p, axis=-1, keepdims=True)
        outs.append(_dot(p.astype(BF16), mv[:, sl]) / den)
    o_m = jnp.concatenate(outs, axis=-1)
    gate_h = _sigmoid(jnp.concatenate(filled, axis=-1))
    mix = gate_sw * up_sw + gate_mx * _dot(o_m.astype(BF16), w_up_ref[2])

    def redo_tile_exact():
        hq_ref[...] = hq
        hk_ref[...] = hk
        hv_ref[...] = hv
        hb_ref[...] = hb

        def chunk_body(c, carry):
            rows = pl.ds(pl.multiple_of(c * HG_CHUNK, HG_CHUNK), HG_CHUNK)
            for h in range(HG_HEADS):
                sl = slice(h * HG_DIM, (h + 1) * HG_DIM)
                o, st_new = _hgrn_chunk_exact(hq_ref[rows, sl], hk_ref[rows, sl], hv_ref[rows, sl], hb_ref[rows, sl],
                                              sto_ref[h])
                sto_ref[h] = st_new
                oh_ref[rows, sl] = o
            return carry

        lax.fori_loop(0, tm // HG_CHUNK, chunk_body, 0)
        st_ref[...] = sto_ref[...]

    if fallback:
        pl.when(jnp.min(hb) < -HG_SAFE_RANGE)(redo_tile_exact)

    mix = mix + gate_h * _dot(_rms_gate(oh_ref[...], hgate, hnw_ref[...]).astype(BF16), w_up_ref[1])
    y_ref[0] = _out_ffn(xn, mix, w_o_ref, w_f1_ref, w_f2_ref, lnp_ref, before_last=lambda: ln0_q(xnext_ref[0]))
    kv_proj()

    @pl.when(t == nt - 1)
    def _():
        for h in range(HG_HEADS):
            s_ref[0, h] = st_ref[h]


def _full(shape):
    nd = len(shape)
    return pl.BlockSpec(shape, lambda *_: (0,) * nd, pipeline_mode=pl.Buffered(1))


def _prompt_layer(x, mk_b, mv_b, sinks, lnp, hlb, hnw, w_in, w_up, w_o, w_f1, w_f2, *, tm, fallback):
    bsz, seq, _ = x.shape
    nt = seq // tm
    assert seq % tm == 0 and tm % WINDOW == 0 and tm % HG_CHUNK == 0
    bias = jnp.asarray(_prompt_bias())
    r = np.arange(tm)
    tri = jnp.asarray((r[:, None] // HG_CHUNK == r[None, :] // HG_CHUNK) & (r[:, None] >= r[None, :]), BF16)
    kern = functools.partial(_prompt_kernel, tm=tm, nt=nt, fallback=fallback)
    n = bsz * nt

    def tile_block(tile):
        return (tile // nt, tile % nt, 0)

    return pl.pallas_call(
        kern,
        grid=(n,),
        in_specs=[
            pl.BlockSpec(memory_space=pltpu.SMEM),
            pl.BlockSpec((1, tm, D_MODEL), lambda s: (0, 0, 0), pipeline_mode=pl.Buffered(1)),
            pl.BlockSpec((1, tm, D_MODEL), lambda s: tile_block(jnp.minimum(s + 1, n - 1))),
            pl.BlockSpec((1, MEM_LEN, MX_W), lambda s: (s // nt, 0, 0)),
            pl.BlockSpec((1, MEM_LEN, MX_W), lambda s: (s // nt, 0, 0)),
            _full(bias.shape), _full(tri.shape), _full(lnp.shape), _full(hlb.shape), _full(hnw.shape),
            _full(w_in.shape), _full(w_up.shape), _full(w_o.shape), _full(w_f1.shape), _full(w_f2.shape),
        ],
        out_specs=[
            pl.BlockSpec((1, tm, D_MODEL), lambda s: tile_block(s)),
            pl.BlockSpec((1, WINDOW, SW_KV), lambda s: (s // nt, 0, 0)),
            pl.BlockSpec((1, WINDOW, SW_KV), lambda s: (s // nt, 0, 0)),
            pl.BlockSpec((1, HG_HEADS, HG_DIM, HG_DIM), lambda s: (s // nt, 0, 0, 0)),
        ],
        out_shape=[
            jax.ShapeDtypeStruct((bsz, seq, D_MODEL), F32),
            jax.ShapeDtypeStruct((bsz, WINDOW, SW_KV), F32),
            jax.ShapeDtypeStruct((bsz, WINDOW, SW_KV), F32),
            jax.ShapeDtypeStruct((bsz, HG_HEADS, HG_DIM, HG_DIM), F32),
        ],
        scratch_shapes=[
            pltpu.VMEM((SW_KV_HEADS, WINDOW + tm, LANES), BF16),
            pltpu.VMEM((SW_KV_HEADS, WINDOW + tm, LANES), BF16),
            pltpu.VMEM((HG_HEADS, HG_DIM, HG_DIM), F32),
            pltpu.VMEM((HG_HEADS, HG_DIM, HG_DIM), F32),
            pltpu.VMEM((tm, HG_W), F32),
            pltpu.VMEM((tm, HG_W), F32),
            pltpu.VMEM((tm, HG_W), F32),
            pltpu.VMEM((tm, HG_W), F32),
            pltpu.VMEM((tm, HG_W), F32),
            pltpu.VMEM((tm, D_MODEL), F32),
            pltpu.VMEM((tm, OFF_HG), F32),
        ],
        compiler_params=pltpu.CompilerParams(
            dimension_semantics=("arbitrary",), vmem_limit_bytes=VMEM_LIMIT_BYTES),
        name="prompt_layer",
    )(sinks, x, x, mk_b, mv_b, bias, tri, lnp, hlb, hnw, w_in, w_up, w_o, w_f1, w_f2)


def _mem_kv_kernel(m_ref, w_ref, mk_ref, mv_ref, mkb_ref, mvb_ref):
    kv = _dot(m_ref[0].astype(BF16), w_ref[...])
    for h in range(MX_HEADS):
        head_rows = pl.ds(h, MEM_LEN, stride=MX_HEADS)
        mk_ref[0, head_rows, :] = kv[:, h * MX_HEAD_DIM:(h + 1) * MX_HEAD_DIM]
        mv_ref[0, head_rows, :] = kv[:, MX_W + h * MX_HEAD_DIM:MX_W + (h + 1) * MX_HEAD_DIM]
    mkb_ref[0] = kv[:, 0:MX_W].astype(BF16)
    mvb_ref[0] = kv[:, MX_W:2 * MX_W].astype(BF16)


def _mem_kv(mem, w_kv):
    bsz = mem.shape[0]
    blk = pl.BlockSpec((1, MEM_LEN, MX_W), lambda b: (b, 0, 0))
    row_blk = pl.BlockSpec((1, MEM_LEN * MX_HEADS, MX_HEAD_DIM), lambda b: (b, 0, 0))
    return pl.pallas_call(
        _mem_kv_kernel,
        grid=(bsz,),
        in_specs=[pl.BlockSpec((1, MEM_LEN, D_MODEL), lambda b: (b, 0, 0)), _full(w_kv.shape)],
        out_specs=[row_blk, row_blk, blk, blk],
        out_shape=[jax.ShapeDtypeStruct((bsz, MEM_LEN * MX_HEADS, MX_HEAD_DIM), F32)] * 2
        + [jax.ShapeDtypeStruct((bsz, MEM_LEN, MX_W), BF16)] * 2,
        compiler_params=pltpu.CompilerParams(dimension_semantics=("arbitrary",)),
        name="mem_kv",
    )(mem, w_kv)


def _prep_params(p):
    assert p['w_in'].shape[0] == DEPTH == 1
    zeros = jnp.zeros((D_MODEL,), F32)
    return dict(
        lnp=jnp.stack([p['ln0_w'], p['ln0_b'], p['ln1_w'][0], p['ln1_b'][0], p['ln2_w'][0], p['ln2_b'][0],
                       zeros, zeros]).astype(F32),
        sinks=p['sw_sinks'][0].astype(F32),
        hlb=p['hg_lower_bound'].astype(F32),
        hnw=p['hg_norm_w'].astype(F32),
        w_in=p['w_in'][0].astype(BF16),
        w_up=jnp.stack([p['w_up_sw'][0], p['w_up_hg'][0], p['w_up_mx'][0]]).astype(BF16),
        w_o=p['w_o'][0].astype(BF16),
        w_f1=p['w_ffn_in'][0].astype(BF16),
        w_f2=p['w_ffn_out'][0].astype(BF16),
        w_kv=p['w_mem_kv'][0].astype(BF16),
    )


def _prompt_group(x, mem, p, *, tm, pp=None):
    pp = _prep_params(p) if pp is None else pp
    mk, mv, mk_b, mv_b = _mem_kv(mem, pp['w_kv'])
    args = (x, mk_b, mv_b, pp['sinks'], pp['lnp'], pp['hlb'], pp['hnw'],
            pp['w_in'], pp['w_up'], pp['w_o'], pp['w_f1'], pp['w_f2'])
    lb_min = jnp.min(jax.nn.softmax(pp['hlb'], axis=0)[0])
    never_redo = HG_CHUNK * jnp.log(lb_min) >= -(HG_SAFE_RANGE - HG_BOUND_MARGIN)
    y, nk, nv, s = lax.cond(never_redo,
                            functools.partial(_prompt_layer, tm=tm, fallback=False),
                            functools.partial(_prompt_layer, tm=tm, fallback=True), *args)
    return y, nk, nv, s, mk, mv


def _sample_proj_kernel(x_ref, lnp_ref, w_in_ref, xn_ref, proj_ref):
    xn = _layer_norm(x_ref[...], lnp_ref[0:1], lnp_ref[1:2])
    xn_ref[...] = xn
    proj_ref[...] = _dot(xn.astype(BF16), w_in_ref[:, 0:OFF_GATE])


def _sample_proj(x2, lnp, w_in, *, tr):
    rows = x2.shape[0]
    return pl.pallas_call(
        _sample_proj_kernel,
        grid=(rows // tr,),
        in_specs=[pl.BlockSpec((tr, D_MODEL), lambda i: (i, 0)), _full(lnp.shape), _full(w_in.shape)],
        out_specs=[pl.BlockSpec((tr, D_MODEL), lambda i: (i, 0)), pl.BlockSpec((tr, OFF_GATE), lambda i: (i, 0))],
        out_shape=[jax.ShapeDtypeStruct((rows, D_MODEL), F32), jax.ShapeDtypeStruct((rows, OFF_GATE), F32)],
        compiler_params=pltpu.CompilerParams(
            dimension_semantics=("arbitrary",), vmem_limit_bytes=VMEM_LIMIT_BYTES),
        name="sample_proj",
    )(x2, lnp, w_in)


def _sample_bias(t_new, n_keys):
    i = np.arange(t_new)[:, None]
    j = np.arange(n_keys)[None, :]
    kpos = np.where(j < WINDOW, PAST_LEN - WINDOW + j, PAST_LEN + j - WINDOW)
    dist = PAST_LEN + i - kpos
    valid = (dist >= 0) & (dist < WINDOW) & (j < WINDOW + t_new)
    out = np.zeros((SW_KV_HEADS, SW_GROUP, t_new, n_keys), np.float32)
    for h in range(SW_HEADS):
        slope = np.float32(2.0) ** np.float32(-8.0 * (h + 1) / SW_HEADS)
        out[h // SW_GROUP, h % SW_GROUP] = np.where(valid, -slope * dist.astype(np.float32), np.float32(NEG))
    return out.reshape(SW_KV_HEADS, SW_GROUP * t_new, n_keys)


def _sample_mix_kernel(sinks_ref, proj_ref, kbuf_ref, vbuf_ref, st_ref, mk_ref, mv_ref, bias_ref, hlb_ref, hnw_ref,
                       osw_ref, oh_ref, om_ref, nk_ref, nv_ref, ns_ref,
                       kk_ref, vv_ref, *, bt, t_new):
    nkeys = 2 * WINDOW
    r = bt * t_new
    elems = [slice(i * t_new, (i + 1) * t_new) for i in range(bt)]
    heads = [slice(h * HG_DIM, (h + 1) * HG_DIM) for h in range(HG_HEADS)]
    pad_rows = BF16_SUBLANES - t_new

    def pad(a):
        return jnp.concatenate([a, jnp.zeros((pad_rows, a.shape[1]), F32)], axis=0)

    def mem_head(ref, i, h):
        return ref[i, pl.ds(h, MEM_LEN, stride=MX_HEADS), :].astype(BF16)

    hq, hk, hv, hl, hgate = _hgrn_inputs(proj_ref[:, OFF_HG:OFF_MX], _lower_bound(hlb_ref[...]))
    ri = lax.broadcasted_iota(jnp.int32, (r, r), 0)
    ci = lax.broadcasted_iota(jnp.int32, (r, r), 1)
    same = (ri // t_new) == (ci // t_new)
    tri = jnp.where(jnp.logical_and(same, ri >= ci), 1.0, 0.0).astype(BF16)
    hb = _exact_dot(tri, hl)
    qe = hq * jnp.exp(hb)
    q_all = proj_ref[:, 0:SW_Q] * (SW_HEAD_DIM ** -0.5)
    mq_all = proj_ref[:, OFF_MX:OFF_GATE]

    zeros = jnp.zeros((nkeys - WINDOW - t_new, LANES), F32)
    for i, rows in enumerate(elems):
        for buf_ref, cache_ref, c0, out_ref in ((kk_ref, kbuf_ref, SW_Q, nk_ref),
                                                (vv_ref, vbuf_ref, SW_Q + SW_KV, nv_ref)):
            buf_ref[i, 0:WINDOW, :] = cache_ref[i]
            buf_ref[i, WINDOW:WINDOW + t_new, :] = proj_ref[rows, c0:c0 + SW_KV]
            buf_ref[i, WINDOW + t_new:nkeys, :] = zeros
            out_ref[i] = buf_ref[i, t_new:t_new + WINDOW, :]

    s_att, s_mem, o_state, upd, b_last = {}, {}, {}, {}, {}
    for i, rows in enumerate(elems):
        kds = _dup_heads(kk_ref[i])
        for g in range(SW_KV_HEADS):
            s_att[i, g] = _dot_nt(_stack_group_queries(q_all[rows], g), kds[g].astype(BF16)) + bias_ref[g]
    for i, rows in enumerate(elems):
        mq = pad(mq_all[rows]).astype(BF16)
        for h, sl in enumerate(heads):
            s_mem[i, h] = _dot_nt(mq[:, sl], mem_head(mk_ref, i, h)) * (MX_HEAD_DIM ** -0.5)
    for i, rows in enumerate(elems):
        b_last[i] = hb[(i + 1) * t_new - 1:(i + 1) * t_new]
        kd = pad(hk[rows] * jnp.exp(b_last[i] - hb[rows])).astype(BF16)
        vb = pad(hv[rows]).astype(BF16)
        qb = pad(qe[rows]).astype(BF16)
        for h, sl in enumerate(heads):
            o_state[i, h] = _dot(qb[:, sl], st_ref[i, h].astype(BF16))[0:t_new]
            upd[i, h] = _dot_tn(kd[:, sl], vb[:, sl])

    p_att, den_att, p_mem, den_mem = {}, {}, {}, {}
    for i in range(bt):
        for g in range(SW_KV_HEADS):
            ps, dens = [], []
            for hh in range(SW_GROUP):
                sh = s_att[i, g][hh * t_new:(hh + 1) * t_new]
                sink = sinks_ref[g * SW_GROUP + hh]
                m = jnp.maximum(jnp.max(sh, axis=-1, keepdims=True), sink)
                p = jnp.exp(sh - m)
                dens.append(jnp.sum(p, axis=-1, keepdims=True) + jnp.exp(sink - m))
                ps.append(p)
            p_att[i, g] = jnp.concatenate(ps, axis=0).astype(BF16)
            den_att[i, g] = jnp.concatenate(dens, axis=0)
        for h in range(MX_HEADS):
            p = jnp.exp(s_mem[i, h] - jnp.max(s_mem[i, h], axis=-1, keepdims=True))
            den_mem[i, h] = jnp.sum(p, axis=-1, keepdims=True)
            p_mem[i, h] = p.astype(BF16)
    for i, rows in enumerate(elems):
        dec = jnp.exp(b_last[i])
        for h, sl in enumerate(heads):
            ns_ref[i, h] = _col_bcast(dec[:, sl]) * st_ref[i, h] + upd[i, h]
            oh_ref[rows, sl] = o_state[i, h] + _hgrn_diag(hq[rows, sl], hk[rows, sl], hv[rows, sl], hb[rows, sl],
                                                          t_new)

    lo = _lane_lo((t_new, LANES))
    for i, rows in enumerate(elems):
        vds = _dup_heads(vv_ref[i])
        blocks = []
        for g in range(SW_KV_HEADS):
            o = _dot(p_att[i, g], vds[g].astype(BF16)) / den_att[i, g]
            blocks.append(jnp.where(lo, o[0:t_new], o[t_new:2 * t_new]))
            blocks.append(jnp.where(lo, o[2 * t_new:3 * t_new], o[3 * t_new:4 * t_new]))
        osw_ref[rows, :] = jnp.concatenate(blocks, axis=-1)
    for i, rows in enumerate(elems):
        outs = [(_dot(p_mem[i, h], mem_head(mv_ref, i, h)) / den_mem[i, h])[0:t_new] for h in range(MX_HEADS)]
        om_ref[rows, :] = jnp.concatenate(outs, axis=-1)
    oh_ref[...] = _rms_gate(oh_ref[...], hgate, hnw_ref[...])


def _sample_mix(proj, kbuf, vbuf, st, mk, mv, sinks, hlb, hnw, *, bt, t_new):
    nb = kbuf.shape[0]
    r = bt * t_new
    assert nb % bt == 0 and t_new == 8 and r % BF16_SUBLANES == 0
    bias = jnp.asarray(_sample_bias(t_new, 2 * WINDOW))
    kern = functools.partial(_sample_mix_kernel, bt=bt, t_new=t_new)
    row_blk = pl.BlockSpec((r, HG_W), lambda i: (i, 0))
    kv_blk = pl.BlockSpec((bt, WINDOW, SW_KV), lambda i: (i, 0, 0))
    st_blk = pl.BlockSpec((bt, HG_HEADS, HG_DIM, HG_DIM), lambda i: (i, 0, 0, 0))
    mem_blk = pl.BlockSpec((bt, MEM_LEN * MX_HEADS, MX_HEAD_DIM), lambda i: (i, 0, 0))
    return pl.pallas_call(
        kern,
        grid=(nb // bt,),
        in_specs=[pl.BlockSpec(memory_space=pltpu.SMEM),
                  pl.BlockSpec((r, OFF_GATE), lambda i: (i, 0)),
                  kv_blk, kv_blk, st_blk, mem_blk, mem_blk,
                  _full(bias.shape), _full(hlb.shape), _full(hnw.shape)],
        out_specs=[row_blk, row_blk, row_blk, kv_blk, kv_blk, st_blk],
        out_shape=[jax.ShapeDtypeStruct((nb * t_new, HG_W), F32)] * 3
        + [jax.ShapeDtypeStruct((nb, WINDOW, SW_KV), F32)] * 2
        + [jax.ShapeDtypeStruct((nb, HG_HEADS, HG_DIM, HG_DIM), F32)],
        scratch_shapes=[pltpu.VMEM((bt, 2 * WINDOW, LANES), F32)] * 2,
        compiler_params=pltpu.CompilerParams(
            dimension_semantics=("arbitrary",), vmem_limit_bytes=VMEM_LIMIT_BYTES),
        name="sample_mix",
    )(sinks, proj, kbuf, vbuf, st, mk, mv, bias, hlb, hnw)


def _sample_out_kernel(xn_ref, osw_ref, oh_ref, om_ref, lnp_ref, w_in_ref, w_up_ref, w_o_ref, w_f1_ref, w_f2_ref,
                       y_ref):
    xn = xn_ref[...]
    xb = xn.astype(BF16)
    mix = None
    for j, br_ref in enumerate((osw_ref, oh_ref, om_ref)):
        term = _branch_gate(xb, j, w_in_ref) * _dot(br_ref[...].astype(BF16), w_up_ref[j])
        mix = term if mix is None else mix + term
    y_ref[...] = _out_ffn(xn, mix, w_o_ref, w_f1_ref, w_f2_ref, lnp_ref)


def _sample_out(xn, osw, oh, om, lnp, w_in, w_up, w_o, w_f1, w_f2, *, tr):
    rows = xn.shape[0]
    wide = pl.BlockSpec((tr, D_MODEL), lambda i: (i, 0))
    half = pl.BlockSpec((tr, HG_W), lambda i: (i, 0))
    return pl.pallas_call(
        _sample_out_kernel,
        grid=(rows // tr,),
        in_specs=[wide, half, half, half, _full(lnp.shape), _full(w_in.shape), _full(w_up.shape),
                  _full(w_o.shape), _full(w_f1.shape), _full(w_f2.shape)],
        out_specs=wide,
        out_shape=jax.ShapeDtypeStruct((rows, D_MODEL), F32),
        compiler_params=pltpu.CompilerParams(
            dimension_semantics=("arbitrary",), vmem_limit_bytes=VMEM_LIMIT_BYTES),
        name="sample_out",
    )(xn, osw, oh, om, lnp, w_in, w_up, w_o, w_f1, w_f2)


def _sample_group(x, p, *, bt, pp=None):
    pp = _prep_params(p) if pp is None else pp
    nb, t_new, _ = x.shape
    rows = nb * t_new
    tr = min(rows, 256)
    xn, proj = _sample_proj(x.reshape(rows, D_MODEL), pp['lnp'], pp['w_in'], tr=tr)
    osw, oh, om, nk, nv, ns = _sample_mix(
        proj,
        p['cache_k_win'][0].reshape(nb, WINDOW, SW_KV), p['cache_v_win'][0].reshape(nb, WINDOW, SW_KV),
        p['state_hgrn'][0],
        p['cache_mem_k'][0].reshape(nb, MEM_LEN * MX_HEADS, MX_HEAD_DIM),
        p['cache_mem_v'][0].reshape(nb, MEM_LEN * MX_HEADS, MX_HEAD_DIM),
        pp['sinks'], pp['hlb'], pp['hnw'], bt=bt, t_new=t_new)
    y = _sample_out(xn, osw, oh, om, pp['lnp'], pp['w_in'], pp['w_up'], pp['w_o'], pp['w_f1'], pp['w_f2'], tr=tr)
    return y.reshape(nb, t_new, D_MODEL), nk, nv, ns


def kernel(x_prompt, x_sample, cache_k_win, cache_v_win, state_hgrn, cache_mem_k, cache_mem_v, mem_prompt,
           ln0_w, ln0_b, w_in, w_up_sw, w_up_hg, w_up_mx, sw_sinks, hg_lower_bound, hg_norm_w, w_mem_kv,
           w_o, ln1_w, ln1_b, w_ffn_in, w_ffn_out, ln2_w, ln2_b):
    p = dict(ln0_w=ln0_w, ln0_b=ln0_b, w_in=w_in, w_up_sw=w_up_sw, w_up_hg=w_up_hg, w_up_mx=w_up_mx,
             sw_sinks=sw_sinks, hg_lower_bound=hg_lower_bound, hg_norm_w=hg_norm_w, w_mem_kv=w_mem_kv,
             w_o=w_o, ln1_w=ln1_w, ln1_b=ln1_b, w_ffn_in=w_ffn_in, w_ffn_out=w_ffn_out, ln2_w=ln2_w, ln2_b=ln2_b)
    p.update(cache_k_win=cache_k_win, cache_v_win=cache_v_win, state_hgrn=state_hgrn,
             cache_mem_k=cache_mem_k, cache_mem_v=cache_mem_v)
    pp = _prep_params(p)
    bp, bs = x_prompt.shape[0], x_sample.shape[0]
    y, nk, nv, s, mk, mv = _prompt_group(x_prompt, mem_prompt, p, tm=PROMPT_TILE, pp=pp)
    ys, nks, nvs, ss = _sample_group(x_sample, p, bt=SAMPLE_BATCH_TILE, pp=pp)
    win = (WINDOW, SW_KV_HEADS, SW_HEAD_DIM)
    mem = (MEM_LEN, MX_HEADS, MX_HEAD_DIM)
    return (y, ys,
            nk.reshape(1, bp, *win), nv.reshape(1, bp, *win), s[None],
            mk.reshape(1, bp, *mem), mv.reshape(1, bp, *mem),
            nks.reshape(1, bs, *win), nvs.reshape(1, bs, *win), ss[None])
```

```python
import functools

import numpy as np
import jax
import jax.numpy as jnp
from jax import lax
from jax.experimental import pallas as pl
from jax.experimental.pallas import tpu as pltpu

D_MODEL = 1024
DEPTH = 1
PAST_LEN = 16384
SW_HEADS, SW_KV_HEADS, SW_HEAD_DIM = 8, 2, 64
SW_GROUP = SW_HEADS // SW_KV_HEADS
WINDOW = 128
HG_HEADS, HG_DIM = 4, 128
MEM_LEN, MX_HEADS, MX_HEAD_DIM = 256, 4, 128
SW_Q = SW_HEADS * SW_HEAD_DIM
SW_KV = SW_KV_HEADS * SW_HEAD_DIM
HG_W = HG_HEADS * HG_DIM
MX_W = MX_HEADS * MX_HEAD_DIM
D_FF = 2816
ALPHA = (2.0 * DEPTH) ** 0.25
LN_EPS = 1e-5
RMS_EPS = 1e-6
OFF_SW = 0
OFF_HG = SW_Q + 2 * SW_KV
OFF_MX = OFF_HG + 4 * HG_W
OFF_GATE = OFF_MX + MX_W
IN_WIDTH = OFF_GATE + 3 * D_MODEL

LANES = 128
BF16_SUBLANES = 16
MXU_TILE = 256
FFN_CHUNK = 4 * MXU_TILE
VMEM_LIMIT_BYTES = 60 * 1024 * 1024

NEG = -1e30
HG_CHUNK = 64
HG_SUB = 16
HG_SAFE_RANGE = 80.0
HG_BOUND_MARGIN = 1.0
PROMPT_TILE = 256
SAMPLE_BATCH_TILE = 8
F32 = jnp.float32
BF16 = jnp.bfloat16


def _dot(a, b):
    return jnp.dot(a, b, preferred_element_type=F32)


def _dot_nt(a, b):
    return lax.dot_general(a, b, (((1,), (1,)), ((), ())), preferred_element_type=F32)


def _dot_tn(a, b):
    return lax.dot_general(a, b, (((0,), (0,)), ((), ())), preferred_element_type=F32)


def _exact_dot(m_bf16, x):
    hi = x.astype(BF16)
    r1 = x - hi.astype(F32)
    mid = r1.astype(BF16)
    lo = (r1 - mid.astype(F32)).astype(BF16)
    return _dot(m_bf16, hi) + _dot(m_bf16, mid) + _dot(m_bf16, lo)


def _layer_norm(x, w, b):
    mu = jnp.mean(x, axis=-1, keepdims=True)
    xc = x - mu
    var = jnp.mean(xc * xc, axis=-1, keepdims=True)
    return xc * lax.rsqrt(var + LN_EPS) * w + b


def _sigmoid(x):
    return 1.0 / (1.0 + jnp.exp(-x))


def _silu(x):
    return x * _sigmoid(x)


def _lower_bound(hlb):
    m = jnp.max(hlb, axis=0, keepdims=True)
    e = jnp.exp(hlb - m)
    return e[0:1] / jnp.sum(e, axis=0, keepdims=True)


def _lane_lo(shape):
    return lax.broadcasted_iota(jnp.int32, shape, len(shape) - 1) < SW_HEAD_DIM


def _dup_heads(kv):
    rolled = pltpu.roll(kv, SW_HEAD_DIM, axis=1)
    lo = _lane_lo(kv.shape)
    return jnp.where(lo, kv, rolled), jnp.where(lo, rolled, kv)


def _sink_softmax_pv(s, vv, sinks_ref, g, rows):
    ps, dens = [], []
    for hh in range(SW_GROUP):
        sh = s[hh * rows:(hh + 1) * rows]
        sink = sinks_ref[g * SW_GROUP + hh]
        m = jnp.maximum(jnp.max(sh, axis=-1, keepdims=True), sink)
        p = jnp.exp(sh - m)
        dens.append(jnp.sum(p, axis=-1, keepdims=True) + jnp.exp(sink - m))
        ps.append(p)
    o = _dot(jnp.concatenate(ps, axis=0).astype(BF16), vv)
    o = o / jnp.concatenate(dens, axis=0)
    lo = _lane_lo((rows, LANES))
    return (jnp.where(lo, o[0:rows], o[rows:2 * rows]),
            jnp.where(lo, o[2 * rows:3 * rows], o[3 * rows:4 * rows]))


def _stack_group_queries(q, g):
    qa = q[:, 2 * g * LANES:(2 * g + 1) * LANES]
    qb = q[:, (2 * g + 1) * LANES:(2 * g + 2) * LANES]
    lo = _lane_lo(qa.shape)
    z = jnp.zeros_like(qa)
    return jnp.concatenate([jnp.where(lo, qa, z), jnp.where(lo, z, qa),
                            jnp.where(lo, qb, z), jnp.where(lo, z, qb)], axis=0).astype(BF16)


def _hgrn_diag(q, k, v, b, n):
    row = lax.broadcasted_iota(jnp.int32, (n, HG_DIM), 0)
    o = jnp.zeros((n, HG_DIM), F32)
    for s in range(n):
        t0 = (s // 8) * 8
        rr = row[t0:]
        e = jnp.exp(jnp.where(rr >= s, b[t0:] - b[s:s + 1], -jnp.inf))
        col = jnp.sum(q[t0:] * e * k[s:s + 1], axis=-1, keepdims=True)
        upd = col * v[s:s + 1]
        o = o + (upd if t0 == 0 else jnp.concatenate([jnp.zeros((t0, HG_DIM), F32), upd], axis=0))
    return o


def _col_bcast(rowvec):
    return jnp.transpose(jnp.broadcast_to(rowvec, (HG_DIM, HG_DIM)))


def _hgrn_chunk_exact(q, k, v, b, st):
    c = HG_CHUNK
    o = _dot((q * jnp.exp(b)).astype(BF16), st.astype(BF16))
    b_last = b[c - 1:c]
    kd = (k * jnp.exp(b_last - b)).astype(BF16)
    vb = v.astype(BF16)
    s_new = _col_bcast(jnp.exp(b_last)) * st + _dot_tn(kd, vb)
    parts = []
    for i in range(c // HG_SUB):
        r0 = i * HG_SUB
        od = _hgrn_diag(q[r0:r0 + HG_SUB], k[r0:r0 + HG_SUB], v[r0:r0 + HG_SUB], b[r0:r0 + HG_SUB], HG_SUB)
        if i > 0:
            g_i = b[r0 - 1:r0]
            qi = (q[r0:r0 + HG_SUB] * jnp.exp(b[r0:r0 + HG_SUB] - g_i)).astype(BF16)
            ki = (k[0:r0] * jnp.exp(g_i - b[0:r0])).astype(BF16)
            a = _dot_nt(qi, ki)
            od = od + _dot(a.astype(BF16), vb[0:r0])
        parts.append(od)
    return o + jnp.concatenate(parts, axis=0), s_new


def _rms_gate(o, gate, nw):
    outs = []
    for h in range(HG_HEADS):
        oh = o[:, h * HG_DIM:(h + 1) * HG_DIM]
        ms = jnp.mean(oh * oh, axis=-1, keepdims=True)
        outs.append(oh * lax.rsqrt(ms + RMS_EPS) * nw * gate[:, h * HG_DIM:(h + 1) * HG_DIM])
    return jnp.concatenate(outs, axis=-1)


def _branch_gate(xb, j, w_in_ref):
    c0 = OFF_GATE + j * D_MODEL
    return _sigmoid(_dot(xb, w_in_ref[:, c0:c0 + D_MODEL]))


def _out_ffn(xn, mix, w_o_ref, w_f1_ref, w_f2_ref, lnp_ref, after_w_o=None):
    u = ALPHA * xn + _dot(mix.astype(BF16), w_o_ref[...])
    if after_w_o is not None:
        after_w_o()
    h = _layer_norm(u, lnp_ref[2:3], lnp_ref[3:4])
    hb = h.astype(BF16)
    acc = None
    for c0 in range(0, D_FF, FFN_CHUNK):
        c1 = min(c0 + FFN_CHUNK, D_FF)
        gt = _dot(hb, w_f1_ref[:, c0:c1])
        up = _dot(hb, w_f1_ref[:, D_FF + c0:D_FF + c1])
        part = _dot((_silu(gt) * up).astype(BF16), w_f2_ref[c0:c1, :])
        acc = part if acc is None else acc + part
    return _layer_norm(ALPHA * h + acc, lnp_ref[4:5], lnp_ref[5:6])


def _hgrn_inputs(hp, lb):
    f = lb + (1.0 - lb) * _sigmoid(hp[:, HG_W:2 * HG_W])
    return (_silu(hp[:, 0:HG_W]), 1.0 - f, hp[:, 2 * HG_W:3 * HG_W], jnp.log(f), _silu(hp[:, 3 * HG_W:4 * HG_W]))


def _prompt_bias():
    i = np.arange(WINDOW)[:, None]
    j = np.arange(2 * WINDOW)[None, :]
    dist = i + WINDOW - j
    valid = (dist >= 0) & (dist < WINDOW)
    out = np.zeros((SW_KV_HEADS, SW_GROUP, WINDOW, 2 * WINDOW), np.float32)
    for h in range(SW_HEADS):
        slope = np.float32(2.0) ** np.float32(-8.0 * (h + 1) / SW_HEADS)
        out[h // SW_GROUP, h % SW_GROUP] = np.where(valid, -slope * dist.astype(np.float32), np.float32(NEG))
    return out.reshape(SW_KV_HEADS, SW_GROUP * WINDOW, 2 * WINDOW)


def _prompt_kernel(sinks_ref, x0_ref, xnext_ref, mk_ref, mv_ref, bias_ref, tri_ref, lnp_ref, hlb_ref, hnw_ref,
                   w_in_ref, w_up_ref, w_o_ref, w_f1_ref, w_f2_ref,
                   y_ref, nk_ref, nv_ref, s_ref,
                   kd_ref, vd_ref, st_ref, sto_ref, hq_ref, hk_ref, hv_ref, hb_ref, oh_ref, xn_sc, qkv_sc,
                   *, tm, nt, fallback):
    s_idx = pl.program_id(0)
    t = lax.rem(s_idx, nt)

    def ln0_q(x):
        x_norm = _layer_norm(x, lnp_ref[0:1], lnp_ref[1:2])
        xn_sc[...] = x_norm
        qkv_sc[:, 0:SW_Q] = _dot(x_norm.astype(BF16), w_in_ref[:, OFF_SW:OFF_SW + SW_Q])

    def kv_proj():
        qkv_sc[:, SW_Q:OFF_HG] = _dot(xn_sc[...].astype(BF16), w_in_ref[:, OFF_SW + SW_Q:OFF_HG])

    @pl.when(s_idx == 0)
    def _():
        ln0_q(x0_ref[0])
        kv_proj()

    @pl.when(t == 0)
    def _():
        kd_ref[:, 0:WINDOW, :] = jnp.zeros((SW_KV_HEADS, WINDOW, LANES), BF16)
        vd_ref[:, 0:WINDOW, :] = jnp.zeros((SW_KV_HEADS, WINDOW, LANES), BF16)
        st_ref[...] = jnp.zeros_like(st_ref)

    xn = xn_sc[...]
    xb = xn.astype(BF16)
    qkv = qkv_sc[...]
    q = qkv[:, 0:SW_Q] * (SW_HEAD_DIM ** -0.5)
    k = qkv[:, SW_Q:SW_Q + SW_KV]
    v = qkv[:, SW_Q + SW_KV:SW_Q + 2 * SW_KV]

    nk_ref[0] = k[tm - WINDOW:tm]
    nv_ref[0] = v[tm - WINDOW:tm]

    for g, dup in enumerate(_dup_heads(k)):
        kd_ref[g, WINDOW:WINDOW + tm, :] = dup.astype(BF16)
    for g, dup in enumerate(_dup_heads(v)):
        vd_ref[g, WINDOW:WINDOW + tm, :] = dup.astype(BF16)

    col = lax.broadcasted_iota(jnp.int32, (1, 2 * WINDOW), 1)
    first = jnp.where(jnp.logical_and(t == 0, col < WINDOW), NEG, 0.0).astype(F32)
    def proj(c0, width):
        return _dot(xb, w_in_ref[:, c0:c0 + width])

    hp_parts = []
    o_rows = []
    for j in range(tm // WINDOW):
        qj = q[j * WINDOW:(j + 1) * WINDOW]
        blocks = []
        for g in range(SW_KV_HEADS):
            kk = kd_ref[g, j * WINDOW:(j + 2) * WINDOW, :]
            vv = vd_ref[g, j * WINDOW:(j + 2) * WINDOW, :]
            s = _dot_nt(_stack_group_queries(qj, g), kk) + bias_ref[g]
            if j == 0:
                s = s + first
            if len(hp_parts) < 4:
                hp_parts.append(proj(OFF_HG + len(hp_parts) * HG_W, HG_W))
            blocks.extend(_sink_softmax_pv(s, vv, sinks_ref, g, WINDOW))
        o_rows.append(jnp.concatenate(blocks, axis=-1))
    o_sw = jnp.concatenate(o_rows, axis=0)
    while len(hp_parts) < 4:
        hp_parts.append(proj(OFF_HG + len(hp_parts) * HG_W, HG_W))
    kd_ref[:, 0:WINDOW, :] = kd_ref[:, tm:tm + WINDOW, :]
    vd_ref[:, 0:WINDOW, :] = vd_ref[:, tm:tm + WINDOW, :]

    hq, hk, hv, hl, hgate = _hgrn_inputs(jnp.concatenate(hp_parts, axis=-1), _lower_bound(hlb_ref[...]))
    hb = _exact_dot(tri_ref[...], hl)
    mq = proj(OFF_MX, MX_W)
    gate_sw = _branch_gate(xb, 0, w_in_ref)
    if fallback:
        sto_ref[...] = st_ref[...]
    qe = (hq * jnp.exp(hb)).astype(BF16)
    ke = (hk * jnp.exp(-hb)).astype(BF16)
    vb = hv.astype(BF16)
    ri = lax.broadcasted_iota(jnp.int32, (HG_CHUNK, HG_CHUNK), 0)
    ci = lax.broadcasted_iota(jnp.int32, (HG_CHUNK, HG_CHUNK), 1)
    causal = ri >= ci
    n_chunks = tm // HG_CHUNK
    heads = [slice(h * HG_DIM, (h + 1) * HG_DIM) for h in range(HG_HEADS)]
    chunks = [slice(c * HG_CHUNK, (c + 1) * HG_CHUNK) for c in range(n_chunks)]
    a_mat, upd, dec = {}, {}, []
    for c, rows in enumerate(chunks):
        b_last = hb[(c + 1) * HG_CHUNK - 1:(c + 1) * HG_CHUNK]
        kd = (hk[rows] * jnp.exp(b_last - hb[rows])).astype(BF16)
        dec.append(jnp.exp(b_last))
        for h, sl in enumerate(heads):
            a_mat[c, h] = jnp.where(causal, _dot_nt(qe[rows, sl], ke[rows, sl]), 0.0).astype(BF16)
            upd[c, h] = _dot_tn(kd[:, sl], vb[rows, sl])
    gate_mx = _branch_gate(xb, 2, w_in_ref)
    st = {(0, h): st_ref[h] for h in range(HG_HEADS)}
    for c in range(n_chunks):
        for h, sl in enumerate(heads):
            st[c + 1, h] = _col_bcast(dec[c][:, sl]) * st[c, h] + upd[c, h]
    for h in range(HG_HEADS):
        st_ref[h] = st[n_chunks, h]
    for c, rows in enumerate(chunks):
        for h, sl in enumerate(heads):
            oh_ref[rows, sl] = _dot(jnp.concatenate([qe[rows, sl], a_mat[c, h]], axis=1),
                                    jnp.concatenate([st[c, h].astype(BF16), vb[rows, sl]], axis=0))
    up_sw = _dot(o_sw.astype(BF16), w_up_ref[0])

    fillers = [lambda: proj(OFF_GATE + D_MODEL, D_MODEL // 2), lambda: proj(OFF_GATE + D_MODEL + D_MODEL // 2, D_MODEL // 2)]
    filled = []
    mk = mk_ref[0]
    mv = mv_ref[0]
    outs = []
    for h in range(MX_HEADS):
        sl = slice(h * MX_HEAD_DIM, (h + 1) * MX_HEAD_DIM)
        s = _dot_nt(mq[:, sl].astype(BF16), mk[:, sl]) * (MX_HEAD_DIM ** -0.5)
        if h % 2 == 0:
            filled.append(fillers[h // 2]())
        p = jnp.exp(s - jnp.max(s, axis=-1, keepdims=True))
        den = jnp.sum(p, axis=-1, keepdims=True)
        outs.append(_dot(p.astype(BF16), mv[:, sl]) / den)
    o_m = jnp.concatenate(outs, axis=-1)
    gate_h = _sigmoid(jnp.concatenate(filled, axis=-1))
    mix = gate_sw * up_sw + gate_mx * _dot(o_m.astype(BF16), w_up_ref[2])

    def redo_tile_exact():
        hq_ref[...] = hq
        hk_ref[...] = hk
        hv_ref[...] = hv
        hb_ref[...] = hb

        def chunk_body(c, carry):
            rows = pl.ds(pl.multiple_of(c * HG_CHUNK, HG_CHUNK), HG_CHUNK)
            for h in range(HG_HEADS):
                sl = slice(h * HG_DIM, (h + 1) * HG_DIM)
                o, st_new = _hgrn_chunk_exact(hq_ref[rows, sl], hk_ref[rows, sl], hv_ref[rows, sl], hb_ref[rows, sl],
                                              sto_ref[h])
                sto_ref[h] = st_new
                oh_ref[rows, sl] = o
            return carry

        lax.fori_loop(0, tm // HG_CHUNK, chunk_body, 0)
        st_ref[...] = sto_ref[...]

    if fallback:
        pl.when(jnp.min(hb) < -HG_SAFE_RANGE)(redo_tile_exact)

    mix = mix + gate_h * _dot(_rms_gate(oh_ref[...], hgate, hnw_ref[...]).astype(BF16), w_up_ref[1])
    y_ref[0] = _out_ffn(xn, mix, w_o_ref, w_f1_ref, w_f2_ref, lnp_ref, after_w_o=lambda: ln0_q(xnext_ref[0]))
    kv_proj()

    @pl.when(t == nt - 1)
    def _():
        for h in range(HG_HEADS):
            s_ref[0, h] = st_ref[h]


def _full(shape):
    nd = len(shape)
    return pl.BlockSpec(shape, lambda *_: (0,) * nd, pipeline_mode=pl.Buffered(1))


def _prompt_layer(x, mk_b, mv_b, sinks, lnp, hlb, hnw, w_in, w_up, w_o, w_f1, w_f2, *, tm, fallback):
    bsz, seq, _ = x.shape
    nt = seq // tm
    assert seq % tm == 0 and tm % WINDOW == 0 and tm % HG_CHUNK == 0
    bias = jnp.asarray(_prompt_bias())
    r = np.arange(tm)
    tri = jnp.asarray((r[:, None] // HG_CHUNK == r[None, :] // HG_CHUNK) & (r[:, None] >= r[None, :]), BF16)
    kern = functools.partial(_prompt_kernel, tm=tm, nt=nt, fallback=fallback)
    n = bsz * nt

    def tile_block(tile):
        return (tile // nt, tile % nt, 0)

    return pl.pallas_call(
        kern,
        grid=(n,),
        in_specs=[
            pl.BlockSpec(memory_space=pltpu.SMEM),
            pl.BlockSpec((1, tm, D_MODEL), lambda s: (0, 0, 0), pipeline_mode=pl.Buffered(1)),
            pl.BlockSpec((1, tm, D_MODEL), lambda s: tile_block(jnp.minimum(s + 1, n - 1))),
            pl.BlockSpec((1, MEM_LEN, MX_W), lambda s: (s // nt, 0, 0)),
            pl.BlockSpec((1, MEM_LEN, MX_W), lambda s: (s // nt, 0, 0)),
            _full(bias.shape), _full(tri.shape), _full(lnp.shape), _full(hlb.shape), _full(hnw.shape),
            _full(w_in.shape), _full(w_up.shape), _full(w_o.shape), _full(w_f1.shape), _full(w_f2.shape),
        ],
        out_specs=[
            pl.BlockSpec((1, tm, D_MODEL), lambda s: tile_block(s)),
            pl.BlockSpec((1, WINDOW, SW_KV), lambda s: (s // nt, 0, 0)),
            pl.BlockSpec((1, WINDOW, SW_KV), lambda s: (s // nt, 0, 0)),
            pl.BlockSpec((1, HG_HEADS, HG_DIM, HG_DIM), lambda s: (s // nt, 0, 0, 0)),
        ],
        out_shape=[
            jax.ShapeDtypeStruct((bsz, seq, D_MODEL), F32),
            jax.ShapeDtypeStruct((bsz, WINDOW, SW_KV), F32),
            jax.ShapeDtypeStruct((bsz, WINDOW, SW_KV), F32),
            jax.ShapeDtypeStruct((bsz, HG_HEADS, HG_DIM, HG_DIM), F32),
        ],
        scratch_shapes=[
            pltpu.VMEM((SW_KV_HEADS, WINDOW + tm, LANES), BF16),
            pltpu.VMEM((SW_KV_HEADS, WINDOW + tm, LANES), BF16),
            pltpu.VMEM((HG_HEADS, HG_DIM, HG_DIM), F32),
            pltpu.VMEM((HG_HEADS, HG_DIM, HG_DIM), F32),
            pltpu.VMEM((tm, HG_W), F32),
            pltpu.VMEM((tm, HG_W), F32),
            pltpu.VMEM((tm, HG_W), F32),
            pltpu.VMEM((tm, HG_W), F32),
            pltpu.VMEM((tm, HG_W), F32),
            pltpu.VMEM((tm, D_MODEL), F32),
            pltpu.VMEM((tm, OFF_HG), F32),
        ],
        compiler_params=pltpu.CompilerParams(
            dimension_semantics=("arbitrary",), vmem_limit_bytes=VMEM_LIMIT_BYTES),
        name="prompt_layer",
    )(sinks, x, x, mk_b, mv_b, bias, tri, lnp, hlb, hnw, w_in, w_up, w_o, w_f1, w_f2)


def _mem_kv_kernel(m_ref, w_ref, mk_ref, mv_ref, mkb_ref, mvb_ref):
    kv = _dot(m_ref[0].astype(BF16), w_ref[...])
    for h in range(MX_HEADS):
        head_rows = pl.ds(h, MEM_LEN, stride=MX_HEADS)
        mk_ref[0, head_rows, :] = kv[:, h * MX_HEAD_DIM:(h + 1) * MX_HEAD_DIM]
        mv_ref[0, head_rows, :] = kv[:, MX_W + h * MX_HEAD_DIM:MX_W + (h + 1) * MX_HEAD_DIM]
    mkb_ref[0] = kv[:, 0:MX_W].astype(BF16)
    mvb_ref[0] = kv[:, MX_W:2 * MX_W].astype(BF16)


def _mem_kv(mem, w_kv):
    bsz = mem.shape[0]
    blk = pl.BlockSpec((1, MEM_LEN, MX_W), lambda b: (b, 0, 0))
    row_blk = pl.BlockSpec((1, MEM_LEN * MX_HEADS, MX_HEAD_DIM), lambda b: (b, 0, 0))
    return pl.pallas_call(
        _mem_kv_kernel,
        grid=(bsz,),
        in_specs=[pl.BlockSpec((1, MEM_LEN, D_MODEL), lambda b: (b, 0, 0)), _full(w_kv.shape)],
        out_specs=[row_blk, row_blk, blk, blk],
        out_shape=[jax.ShapeDtypeStruct((bsz, MEM_LEN * MX_HEADS, MX_HEAD_DIM), F32)] * 2
        + [jax.ShapeDtypeStruct((bsz, MEM_LEN, MX_W), BF16)] * 2,
        compiler_params=pltpu.CompilerParams(dimension_semantics=("arbitrary",)),
        name="mem_kv",
    )(mem, w_kv)


def _prep_params(p):
    assert p['w_in'].shape[0] == DEPTH == 1
    zeros = jnp.zeros((D_MODEL,), F32)
    return dict(
        lnp=jnp.stack([p['ln0_w'], p['ln0_b'], p['ln1_w'][0], p['ln1_b'][0], p['ln2_w'][0], p['ln2_b'][0],
                       zeros, zeros]).astype(F32),
        sinks=p['sw_sinks'][0].astype(F32),
        hlb=p['hg_lower_bound'].astype(F32),
        hnw=p['hg_norm_w'].astype(F32),
        w_in=p['w_in'][0].astype(BF16),
        w_up=jnp.stack([p['w_up_sw'][0], p['w_up_hg'][0], p['w_up_mx'][0]]).astype(BF16),
        w_o=p['w_o'][0].astype(BF16),
        w_f1=p['w_ffn_in'][0].astype(BF16),
        w_f2=p['w_ffn_out'][0].astype(BF16),
        w_kv=p['w_mem_kv'][0].astype(BF16),
    )


def _prompt_group(x, mem, p, *, tm, pp=None):
    pp = _prep_params(p) if pp is None else pp
    mk, mv, mk_b, mv_b = _mem_kv(mem, pp['w_kv'])
    args = (x, mk_b, mv_b, pp['sinks'], pp['lnp'], pp['hlb'], pp['hnw'],
            pp['w_in'], pp['w_up'], pp['w_o'], pp['w_f1'], pp['w_f2'])
    lb_min = jnp.min(jax.nn.softmax(pp['hlb'], axis=0)[0])
    never_redo = HG_CHUNK * jnp.log(lb_min) >= -(HG_SAFE_RANGE - HG_BOUND_MARGIN)
    y, nk, nv, s = lax.cond(never_redo,
                            functools.partial(_prompt_layer, tm=tm, fallback=False),
                            functools.partial(_prompt_layer, tm=tm, fallback=True), *args)
    return y, nk, nv, s, mk, mv


def _sample_proj_kernel(x_ref, lnp_ref, w_in_ref, xn_ref, proj_ref):
    xn = _layer_norm(x_ref[...], lnp_ref[0:1], lnp_ref[1:2])
    xn_ref[...] = xn
    proj_ref[...] = _dot(xn.astype(BF16), w_in_ref[:, 0:OFF_GATE])


def _sample_proj(x2, lnp, w_in, *, tr):
    rows = x2.shape[0]
    return pl.pallas_call(
        _sample_proj_kernel,
        grid=(rows // tr,),
        in_specs=[pl.BlockSpec((tr, D_MODEL), lambda i: (i, 0)), _full(lnp.shape), _full(w_in.shape)],
        out_specs=[pl.BlockSpec((tr, D_MODEL), lambda i: (i, 0)), pl.BlockSpec((tr, OFF_GATE), lambda i: (i, 0))],
        out_shape=[jax.ShapeDtypeStruct((rows, D_MODEL), F32), jax.ShapeDtypeStruct((rows, OFF_GATE), F32)],
        compiler_params=pltpu.CompilerParams(
            dimension_semantics=("arbitrary",), vmem_limit_bytes=VMEM_LIMIT_BYTES),
        name="sample_proj",
    )(x2, lnp, w_in)


def _sample_bias(t_new, n_keys):
    i = np.arange(t_new)[:, None]
    j = np.arange(n_keys)[None, :]
    kpos = np.where(j < WINDOW, PAST_LEN - WINDOW + j, PAST_LEN + j - WINDOW)
    dist = PAST_LEN + i - kpos
    valid = (dist >= 0) & (dist < WINDOW) & (j < WINDOW + t_new)
    out = np.zeros((SW_KV_HEADS, SW_GROUP, t_new, n_keys), np.float32)
    for h in range(SW_HEADS):
        slope = np.float32(2.0) ** np.float32(-8.0 * (h + 1) / SW_HEADS)
        out[h // SW_GROUP, h % SW_GROUP] = np.where(valid, -slope * dist.astype(np.float32), np.float32(NEG))
    return out.reshape(SW_KV_HEADS, SW_GROUP * t_new, n_keys)


def _sample_mix_kernel(sinks_ref, proj_ref, kbuf_ref, vbuf_ref, st_ref, mk_ref, mv_ref, bias_ref, hlb_ref, hnw_ref,
                       osw_ref, oh_ref, om_ref, nk_ref, nv_ref, ns_ref,
                       kk_ref, vv_ref, *, bt, t_new):
    nkeys = 2 * WINDOW
    r = bt * t_new
    elems = [slice(i * t_new, (i + 1) * t_new) for i in range(bt)]
    heads = [slice(h * HG_DIM, (h + 1) * HG_DIM) for h in range(HG_HEADS)]
    pad_rows = BF16_SUBLANES - t_new

    def pad(a):
        return jnp.concatenate([a, jnp.zeros((pad_rows, a.shape[1]), F32)], axis=0)

    def mem_head(ref, i, h):
        return ref[i, pl.ds(h, MEM_LEN, stride=MX_HEADS), :].astype(BF16)

    hq, hk, hv, hl, hgate = _hgrn_inputs(proj_ref[:, OFF_HG:OFF_MX], _lower_bound(hlb_ref[...]))
    ri = lax.broadcasted_iota(jnp.int32, (r, r), 0)
    ci = lax.broadcasted_iota(jnp.int32, (r, r), 1)
    same = (ri // t_new) == (ci // t_new)
    tri = jnp.where(jnp.logical_and(same, ri >= ci), 1.0, 0.0).astype(BF16)
    hb = _exact_dot(tri, hl)
    qe = hq * jnp.exp(hb)
    q_all = proj_ref[:, 0:SW_Q] * (SW_HEAD_DIM ** -0.5)
    mq_all = proj_ref[:, OFF_MX:OFF_GATE]

    zeros = jnp.zeros((nkeys - WINDOW - t_new, LANES), F32)
    for i, rows in enumerate(elems):
        for buf_ref, cache_ref, c0, out_ref in ((kk_ref, kbuf_ref, SW_Q, nk_ref),
                                                (vv_ref, vbuf_ref, SW_Q + SW_KV, nv_ref)):
            buf_ref[i, 0:WINDOW, :] = cache_ref[i]
            buf_ref[i, WINDOW:WINDOW + t_new, :] = proj_ref[rows, c0:c0 + SW_KV]
            buf_ref[i, WINDOW + t_new:nkeys, :] = zeros
            out_ref[i] = buf_ref[i, t_new:t_new + WINDOW, :]

    s_att, s_mem, o_state, upd, b_last = {}, {}, {}, {}, {}
    for i, rows in enumerate(elems):
        kds = _dup_heads(kk_ref[i])
        for g in range(SW_KV_HEADS):
            s_att[i, g] = _dot_nt(_stack_group_queries(q_all[rows], g), kds[g].astype(BF16)) + bias_ref[g]
    for i, rows in enumerate(elems):
        mq = pad(mq_all[rows]).astype(BF16)
        for h, sl in enumerate(heads):
            s_mem[i, h] = _dot_nt(mq[:, sl], mem_head(mk_ref, i, h)) * (MX_HEAD_DIM ** -0.5)
    for i, rows in enumerate(elems):
        b_last[i] = hb[(i + 1) * t_new - 1:(i + 1) * t_new]
        kd = pad(hk[rows] * jnp.exp(b_last[i] - hb[rows])).astype(BF16)
        vb = pad(hv[rows]).astype(BF16)
        qb = pad(qe[rows]).astype(BF16)
        for h, sl in enumerate(heads):
            o_state[i, h] = _dot(qb[:, sl], st_ref[i, h].astype(BF16))[0:t_new]
            upd[i, h] = _dot_tn(kd[:, sl], vb[:, sl])

    p_att, den_att, p_mem, den_mem = {}, {}, {}, {}
    for i in range(bt):
        for g in range(SW_KV_HEADS):
            ps, dens = [], []
            for hh in range(SW_GROUP):
                sh = s_att[i, g][hh * t_new:(hh + 1) * t_new]
                sink = sinks_ref[g * SW_GROUP + hh]
                m = jnp.maximum(jnp.max(sh, axis=-1, keepdims=True), sink)
                p = jnp.exp(sh - m)
                dens.append(jnp.sum(p, axis=-1, keepdims=True) + jnp.exp(sink - m))
                ps.append(p)
            p_att[i, g] = jnp.concatenate(ps, axis=0).astype(BF16)
            den_att[i, g] = jnp.concatenate(dens, axis=0)
        for h in range(MX_HEADS):
            p = jnp.exp(s_mem[i, h] - jnp.max(s_mem[i, h], axis=-1, keepdims=True))
            den_mem[i, h] = jnp.sum(p, axis=-1, keepdims=True)
            p_mem[i, h] = p.astype(BF16)
    for i, rows in enumerate(elems):
        dec = jnp.exp(b_last[i])
        for h, sl in enumerate(heads):
            ns_ref[i, h] = _col_bcast(dec[:, sl]) * st_ref[i, h] + upd[i, h]
            oh_ref[rows, sl] = o_state[i, h] + _hgrn_diag(hq[rows, sl], hk[rows, sl], hv[rows, sl], hb[rows, sl],
                                                          t_new)

    lo = _lane_lo((t_new, LANES))
    for i, rows in enumerate(elems):
        vds = _dup_heads(vv_ref[i])
        blocks = []
        for g in range(SW_KV_HEADS):
            o = _dot(p_att[i, g], vds[g].astype(BF16)) / den_att[i, g]
            blocks.append(jnp.where(lo, o[0:t_new], o[t_new:2 * t_new]))
            blocks.append(jnp.where(lo, o[2 * t_new:3 * t_new], o[3 * t_new:4 * t_new]))
        osw_ref[rows, :] = jnp.concatenate(blocks, axis=-1)
    for i, rows in enumerate(elems):
        outs = [(_dot(p_mem[i, h], mem_head(mv_ref, i, h)) / den_mem[i, h])[0:t_new] for h in range(MX_HEADS)]
        om_ref[rows, :] = jnp.concatenate(outs, axis=-1)
    oh_ref[...] = _rms_gate(oh_ref[...], hgate, hnw_ref[...])


def _sample_mix(proj, kbuf, vbuf, st, mk, mv, sinks, hlb, hnw, *, bt, t_new):
    nb = kbuf.shape[0]
    r = bt * t_new
    assert nb % bt == 0 and t_new == 8 and r % BF16_SUBLANES == 0
    bias = jnp.asarray(_sample_bias(t_new, 2 * WINDOW))
    kern = functools.partial(_sample_mix_kernel, bt=bt, t_new=t_new)
    row_blk = pl.BlockSpec((r, HG_W), lambda i: (i, 0))
    kv_blk = pl.BlockSpec((bt, WINDOW, SW_KV), lambda i: (i, 0, 0))
    st_blk = pl.BlockSpec((bt, HG_HEADS, HG_DIM, HG_DIM), lambda i: (i, 0, 0, 0))
    mem_blk = pl.BlockSpec((bt, MEM_LEN * MX_HEADS, MX_HEAD_DIM), lambda i: (i, 0, 0))
    return pl.pallas_call(
        kern,
        grid=(nb // bt,),
        in_specs=[pl.BlockSpec(memory_space=pltpu.SMEM),
                  pl.BlockSpec((r, OFF_GATE), lambda i: (i, 0)),
                  kv_blk, kv_blk, st_blk, mem_blk, mem_blk,
                  _full(bias.shape), _full(hlb.shape), _full(hnw.shape)],
        out_specs=[row_blk, row_blk, row_blk, kv_blk, kv_blk, st_blk],
        out_shape=[jax.ShapeDtypeStruct((nb * t_new, HG_W), F32)] * 3
        + [jax.ShapeDtypeStruct((nb, WINDOW, SW_KV), F32)] * 2
        + [jax.ShapeDtypeStruct((nb, HG_HEADS, HG_DIM, HG_DIM), F32)],
        scratch_shapes=[pltpu.VMEM((bt, 2 * WINDOW, LANES), F32)] * 2,
        compiler_params=pltpu.CompilerParams(
            dimension_semantics=("arbitrary",), vmem_limit_bytes=VMEM_LIMIT_BYTES),
        name="sample_mix",
    )(sinks, proj, kbuf, vbuf, st, mk, mv, bias, hlb, hnw)


def _sample_out_kernel(xn_ref, osw_ref, oh_ref, om_ref, lnp_ref, w_in_ref, w_up_ref, w_o_ref, w_f1_ref, w_f2_ref,
                       y_ref):
    xn = xn_ref[...]
    xb = xn.astype(BF16)
    mix = None
    for j, br_ref in enumerate((osw_ref, oh_ref, om_ref)):
        term = _branch_gate(xb, j, w_in_ref) * _dot(br_ref[...].astype(BF16), w_up_ref[j])
        mix = term if mix is None else mix + term
    y_ref[...] = _out_ffn(xn, mix, w_o_ref, w_f1_ref, w_f2_ref, lnp_ref)


def _sample_out(xn, osw, oh, om, lnp, w_in, w_up, w_o, w_f1, w_f2, *, tr):
    rows = xn.shape[0]
    wide = pl.BlockSpec((tr, D_MODEL), lambda i: (i, 0))
    half = pl.BlockSpec((tr, HG_W), lambda i: (i, 0))
    return pl.pallas_call(
        _sample_out_kernel,
        grid=(rows // tr,),
        in_specs=[wide, half, half, half, _full(lnp.shape), _full(w_in.shape), _full(w_up.shape),
                  _full(w_o.shape), _full(w_f1.shape), _full(w_f2.shape)],
        out_specs=wide,
        out_shape=jax.ShapeDtypeStruct((rows, D_MODEL), F32),
        compiler_params=pltpu.CompilerParams(
            dimension_semantics=("arbitrary",), vmem_limit_bytes=VMEM_LIMIT_BYTES),
        name="sample_out",
    )(xn, osw, oh, om, lnp, w_in, w_up, w_o, w_f1, w_f2)


def _sample_group(x, p, *, bt, pp=None):
    pp = _prep_params(p) if pp is None else pp
    nb, t_new, _ = x.shape
    rows = nb * t_new
    tr = min(rows, 256)
    xn, proj = _sample_proj(x.reshape(rows, D_MODEL), pp['lnp'], pp['w_in'], tr=tr)
    osw, oh, om, nk, nv, ns = _sample_mix(
        proj,
        p['cache_k_win'][0].reshape(nb, WINDOW, SW_KV), p['cache_v_win'][0].reshape(nb, WINDOW, SW_KV),
        p['state_hgrn'][0],
        p['cache_mem_k'][0].reshape(nb, MEM_LEN * MX_HEADS, MX_HEAD_DIM),
        p['cache_mem_v'][0].reshape(nb, MEM_LEN * MX_HEADS, MX_HEAD_DIM),
        pp['sinks'], pp['hlb'], pp['hnw'], bt=bt, t_new=t_new)
    y = _sample_out(xn, osw, oh, om, pp['lnp'], pp['w_in'], pp['w_up'], pp['w_o'], pp['w_f1'], pp['w_f2'], tr=tr)
    return y.reshape(nb, t_new, D_MODEL), nk, nv, ns


def kernel(x_prompt, x_sample, cache_k_win, cache_v_win, state_hgrn, cache_mem_k, cache_mem_v, mem_prompt,
           ln0_w, ln0_b, w_in, w_up_sw, w_up_hg, w_up_mx, sw_sinks, hg_lower_bound, hg_norm_w, w_mem_kv,
           w_o, ln1_w, ln1_b, w_ffn_in, w_ffn_out, ln2_w, ln2_b):
    p = dict(ln0_w=ln0_w, ln0_b=ln0_b, w_in=w_in, w_up_sw=w_up_sw, w_up_hg=w_up_hg, w_up_mx=w_up_mx,
             sw_sinks=sw_sinks, hg_lower_bound=hg_lower_bound, hg_norm_w=hg_norm_w, w_mem_kv=w_mem_kv,
             w_o=w_o, ln1_w=ln1_w, ln1_b=ln1_b, w_ffn_in=w_ffn_in, w_ffn_out=w_ffn_out, ln2_w=ln2_w, ln2_b=ln2_b)
    p.update(cache_k_win=cache_k_win, cache_v_win=cache_v_win, state_hgrn=state_hgrn,
             cache_mem_k=cache_mem_k, cache_mem_v=cache_mem_v)
    pp = _prep_params(p)
    bp, bs = x_prompt.shape[0], x_sample.shape[0]
    y, nk, nv, s, mk, mv = _prompt_group(x_prompt, mem_prompt, p, tm=PROMPT_TILE, pp=pp)
    ys, nks, nvs, ss = _sample_group(x_sample, p, bt=SAMPLE_BATCH_TILE, pp=pp)
    win = (WINDOW, SW_KV_HEADS, SW_HEAD_DIM)
    mem = (MEM_LEN, MX_HEADS, MX_HEAD_DIM)
    return (y, ys,
            nk.reshape(1, bp, *win), nv.reshape(1, bp, *win), s[None],
            mk.reshape(1, bp, *mem), mv.reshape(1, bp, *mem),
            nks.reshape(1, bs, *win), nvs.reshape(1, bs, *win), ss[None])
```

```python
import functools

import numpy as np
import jax
import jax.numpy as jnp
from jax import lax
from jax.experimental import pallas as pl
from jax.experimental.pallas import tpu as pltpu

D_MODEL = 1024
DEPTH = 1
PAST_LEN = 16384
SW_HEADS, SW_KV_HEADS, SW_HEAD_DIM = 8, 2, 64
SW_GROUP = SW_HEADS // SW_KV_HEADS
WINDOW = 128
HG_HEADS, HG_DIM = 4, 128
MEM_LEN, MX_HEADS, MX_HEAD_DIM = 256, 4, 128
SW_Q = SW_HEADS * SW_HEAD_DIM
SW_KV = SW_KV_HEADS * SW_HEAD_DIM
HG_W = HG_HEADS * HG_DIM
MX_W = MX_HEADS * MX_HEAD_DIM
D_FF = 2816
ALPHA = (2.0 * DEPTH) ** 0.25
LN_EPS = 1e-5
RMS_EPS = 1e-6
OFF_SW = 0
OFF_HG = SW_Q + 2 * SW_KV
OFF_MX = OFF_HG + 4 * HG_W
OFF_GATE = OFF_MX + MX_W
IN_WIDTH = OFF_GATE + 3 * D_MODEL

LANES = 128
BF16_SUBLANES = 16
MXU_TILE = 256
FFN_CHUNK = 4 * MXU_TILE
VMEM_LIMIT_BYTES = 60 * 1024 * 1024

NEG = -1e30
HG_CHUNK = 64
HG_SUB = 16
HG_SAFE_RANGE = 80.0
HG_BOUND_MARGIN = 1.0
PROMPT_TILE = 256
SAMPLE_BATCH_TILE = 8
F32 = jnp.float32
BF16 = jnp.bfloat16


def _dot(a, b):
    return jnp.dot(a, b, preferred_element_type=F32)


def _dot_nt(a, b):
    return lax.dot_general(a, b, (((1,), (1,)), ((), ())), preferred_element_type=F32)


def _dot_tn(a, b):
    return lax.dot_general(a, b, (((0,), (0,)), ((), ())), preferred_element_type=F32)


def _exact_dot(m_bf16, x):
    hi = x.astype(BF16)
    r1 = x - hi.astype(F32)
    mid = r1.astype(BF16)
    lo = (r1 - mid.astype(F32)).astype(BF16)
    return _dot(m_bf16, hi) + _dot(m_bf16, mid) + _dot(m_bf16, lo)


def _layer_norm(x, w, b):
    mu = jnp.mean(x, axis=-1, keepdims=True)
    xc = x - mu
    var = jnp.mean(xc * xc, axis=-1, keepdims=True)
    return xc * lax.rsqrt(var + LN_EPS) * w + b


def _sigmoid(x):
    return 1.0 / (1.0 + jnp.exp(-x))


def _silu(x):
    return x * _sigmoid(x)


def _lower_bound(hlb):
    m = jnp.max(hlb, axis=0, keepdims=True)
    e = jnp.exp(hlb - m)
    return e[0:1] / jnp.sum(e, axis=0, keepdims=True)


def _lane_lo(shape):
    return lax.broadcasted_iota(jnp.int32, shape, len(shape) - 1) < SW_HEAD_DIM


def _dup_heads(kv):
    rolled = pltpu.roll(kv, SW_HEAD_DIM, axis=1)
    lo = _lane_lo(kv.shape)
    return jnp.where(lo, kv, rolled), jnp.where(lo, rolled, kv)


def _sink_softmax_pv(s, vv, sinks_ref, g, rows):
    ps, dens = [], []
    for hh in range(SW_GROUP):
        sh = s[hh * rows:(hh + 1) * rows]
        sink = sinks_ref[g * SW_GROUP + hh]
        m = jnp.maximum(jnp.max(sh, axis=-1, keepdims=True), sink)
        p = jnp.exp(sh - m)
        dens.append(jnp.sum(p, axis=-1, keepdims=True) + jnp.exp(sink - m))
        ps.append(p)
    o = _dot(jnp.concatenate(ps, axis=0).astype(BF16), vv)
    o = o / jnp.concatenate(dens, axis=0)
    lo = _lane_lo((rows, LANES))
    return (jnp.where(lo, o[0:rows], o[rows:2 * rows]),
            jnp.where(lo, o[2 * rows:3 * rows], o[3 * rows:4 * rows]))


def _stack_group_queries(q, g):
    qa = q[:, 2 * g * LANES:(2 * g + 1) * LANES]
    qb = q[:, (2 * g + 1) * LANES:(2 * g + 2) * LANES]
    lo = _lane_lo(qa.shape)
    z = jnp.zeros_like(qa)
    return jnp.concatenate([jnp.where(lo, qa, z), jnp.where(lo, z, qa),
                            jnp.where(lo, qb, z), jnp.where(lo, z, qb)], axis=0).astype(BF16)


def _hgrn_diag(q, k, v, b, n):
    row = lax.broadcasted_iota(jnp.int32, (n, HG_DIM), 0)
    o = jnp.zeros((n, HG_DIM), F32)
    for s in range(n):
        t0 = (s // 8) * 8
        rr = row[t0:]
        e = jnp.exp(jnp.where(rr >= s, b[t0:] - b[s:s + 1], -jnp.inf))
        col = jnp.sum(q[t0:] * e * k[s:s + 1], axis=-1, keepdims=True)
        upd = col * v[s:s + 1]
        o = o + (upd if t0 == 0 else jnp.concatenate([jnp.zeros((t0, HG_DIM), F32), upd], axis=0))
    return o


def _col_bcast(rowvec):
    return jnp.transpose(jnp.broadcast_to(rowvec, (HG_DIM, HG_DIM)))


def _hgrn_chunk_exact(q, k, v, b, st):
    c = HG_CHUNK
    o = _dot((q * jnp.exp(b)).astype(BF16), st.astype(BF16))
    b_last = b[c - 1:c]
    kd = (k * jnp.exp(b_last - b)).astype(BF16)
    vb = v.astype(BF16)
    s_new = _col_bcast(jnp.exp(b_last)) * st + _dot_tn(kd, vb)
    parts = []
    for i in range(c // HG_SUB):
        r0 = i * HG_SUB
        od = _hgrn_diag(q[r0:r0 + HG_SUB], k[r0:r0 + HG_SUB], v[r0:r0 + HG_SUB], b[r0:r0 + HG_SUB], HG_SUB)
        if i > 0:
            g_i = b[r0 - 1:r0]
            qi = (q[r0:r0 + HG_SUB] * jnp.exp(b[r0:r0 + HG_SUB] - g_i)).astype(BF16)
            ki = (k[0:r0] * jnp.exp(g_i - b[0:r0])).astype(BF16)
            a = _dot_nt(qi, ki)
            od = od + _dot(a.astype(BF16), vb[0:r0])
        parts.append(od)
    return o + jnp.concatenate(parts, axis=0), s_new


def _rms_gate(o, gate, nw):
    outs = []
    for h in range(HG_HEADS):
        oh = o[:, h * HG_DIM:(h + 1) * HG_DIM]
        ms = jnp.mean(oh * oh, axis=-1, keepdims=True)
        outs.append(oh * lax.rsqrt(ms + RMS_EPS) * nw * gate[:, h * HG_DIM:(h + 1) * HG_DIM])
    return jnp.concatenate(outs, axis=-1)


def _branch_gate(xb, j, w_in_ref):
    c0 = OFF_GATE + j * D_MODEL
    return _sigmoid(_dot(xb, w_in_ref[:, c0:c0 + D_MODEL]))


def _out_ffn(xn, mix, w_o_ref, w_f1_ref, w_f2_ref, lnp_ref, after_w_o=None):
    u = ALPHA * xn + _dot(mix.astype(BF16), w_o_ref[...])
    if after_w_o is not None:
        after_w_o()
    h = _layer_norm(u, lnp_ref[2:3], lnp_ref[3:4])
    hb = h.astype(BF16)
    acc = None
    for c0 in range(0, D_FF, FFN_CHUNK):
        c1 = min(c0 + FFN_CHUNK, D_FF)
        gt = _dot(hb, w_f1_ref[:, c0:c1])
        up = _dot(hb, w_f1_ref[:, D_FF + c0:D_FF + c1])
        part = _dot((_silu(gt) * up).astype(BF16), w_f2_ref[c0:c1, :])
        acc = part if acc is None else acc + part
    return _layer_norm(ALPHA * h + acc, lnp_ref[4:5], lnp_ref[5:6])


def _hgrn_inputs(hp, lb):
    f = lb + (1.0 - lb) * _sigmoid(hp[:, HG_W:2 * HG_W])
    return (_silu(hp[:, 0:HG_W]), 1.0 - f, hp[:, 2 * HG_W:3 * HG_W], jnp.log(f), _silu(hp[:, 3 * HG_W:4 * HG_W]))


def _prompt_bias():
    i = np.arange(WINDOW)[:, None]
    j = np.arange(2 * WINDOW)[None, :]
    dist = i + WINDOW - j
    valid = (dist >= 0) & (dist < WINDOW)
    out = np.zeros((SW_KV_HEADS, SW_GROUP, WINDOW, 2 * WINDOW), np.float32)
    for h in range(SW_HEADS):
        slope = np.float32(2.0) ** np.float32(-8.0 * (h + 1) / SW_HEADS)
        out[h // SW_GROUP, h % SW_GROUP] = np.where(valid, -slope * dist.astype(np.float32), np.float32(NEG))
    return out.reshape(SW_KV_HEADS, SW_GROUP * WINDOW, 2 * WINDOW)


def _prompt_kernel(sinks_ref, x0_ref, xnext_ref, mk_ref, mv_ref, bias_ref, tri_ref, lnp_ref, hlb_ref, hnw_ref,
                   w_in_ref, w_up_ref, w_o_ref, w_f1_ref, w_f2_ref,
                   y_ref, nk_ref, nv_ref, s_ref,
                   kd_ref, vd_ref, st_ref, sto_ref, hq_ref, hk_ref, hv_ref, hb_ref, oh_ref, xn_sc, qkv_sc,
                   *, tm, nt, fallback):
    s_idx = pl.program_id(0)
    t = lax.rem(s_idx, nt)

    def ln0_q(x):
        x_norm = _layer_norm(x, lnp_ref[0:1], lnp_ref[1:2])
        xn_sc[...] = x_norm
        qkv_sc[:, 0:SW_Q] = _dot(x_norm.astype(BF16), w_in_ref[:, OFF_SW:OFF_SW + SW_Q])

    def kv_proj():
        qkv_sc[:, SW_Q:OFF_HG] = _dot(xn_sc[...].astype(BF16), w_in_ref[:, OFF_SW + SW_Q:OFF_HG])

    @pl.when(s_idx == 0)
    def _():
        ln0_q(x0_ref[0])
        kv_proj()

    @pl.when(t == 0)
    def _():
        kd_ref[:, 0:WINDOW, :] = jnp.zeros((SW_KV_HEADS, WINDOW, LANES), BF16)
        vd_ref[:, 0:WINDOW, :] = jnp.zeros((SW_KV_HEADS, WINDOW, LANES), BF16)
        st_ref[...] = jnp.zeros_like(st_ref)

    xn = xn_sc[...]
    xb = xn.astype(BF16)
    qkv = qkv_sc[...]
    q = qkv[:, 0:SW_Q] * (SW_HEAD_DIM ** -0.5)
    k = qkv[:, SW_Q:SW_Q + SW_KV]
    v = qkv[:, SW_Q + SW_KV:SW_Q + 2 * SW_KV]

    nk_ref[0] = k[tm - WINDOW:tm]
    nv_ref[0] = v[tm - WINDOW:tm]

    for g, dup in enumerate(_dup_heads(k)):
        kd_ref[g, WINDOW:WINDOW + tm, :] = dup.astype(BF16)
    for g, dup in enumerate(_dup_heads(v)):
        vd_ref[g, WINDOW:WINDOW + tm, :] = dup.astype(BF16)

    col = lax.broadcasted_iota(jnp.int32, (1, 2 * WINDOW), 1)
    first = jnp.where(jnp.logical_and(t == 0, col < WINDOW), NEG, 0.0).astype(F32)
    def proj(c0, width):
        return _dot(xb, w_in_ref[:, c0:c0 + width])

    hp_parts = []
    o_rows = []
    for j in range(tm // WINDOW):
        qj = q[j * WINDOW:(j + 1) * WINDOW]
        blocks = []
        for g in range(SW_KV_HEADS):
            kk = kd_ref[g, j * WINDOW:(j + 2) * WINDOW, :]
            vv = vd_ref[g, j * WINDOW:(j + 2) * WINDOW, :]
            s = _dot_nt(_stack_group_queries(qj, g), kk) + bias_ref[g]
            if j == 0:
                s = s + first
            if len(hp_parts) < 4:
                hp_parts.append(proj(OFF_HG + len(hp_parts) * HG_W, HG_W))
            blocks.extend(_sink_softmax_pv(s, vv, sinks_ref, g, WINDOW))
        o_rows.append(jnp.concatenate(blocks, axis=-1))
    o_sw = jnp.concatenate(o_rows, axis=0)
    while len(hp_parts) < 4:
        hp_parts.append(proj(OFF_HG + len(hp_parts) * HG_W, HG_W))
    kd_ref[:, 0:WINDOW, :] = kd_ref[:, tm:tm + WINDOW, :]
    vd_ref[:, 0:WINDOW, :] = vd_ref[:, tm:tm + WINDOW, :]

    hq, hk, hv, hl, hgate = _hgrn_inputs(jnp.concatenate(hp_parts, axis=-1), _lower_bound(hlb_ref[...]))
    hb = _exact_dot(tri_ref[...], hl)
    mq = proj(OFF_MX, MX_W)
    gate_sw = _branch_gate(xb, 0, w_in_ref)
    if fallback:
        sto_ref[...] = st_ref[...]
    qe = (hq * jnp.exp(hb)).astype(BF16)
    ke = (hk * jnp.exp(-hb)).astype(BF16)
    vb = hv.astype(BF16)
    ri = lax.broadcasted_iota(jnp.int32, (HG_CHUNK, HG_CHUNK), 0)
    ci = lax.broadcasted_iota(jnp.int32, (HG_CHUNK, HG_CHUNK), 1)
    causal = ri >= ci
    n_chunks = tm // HG_CHUNK
    heads = [slice(h * HG_DIM, (h + 1) * HG_DIM) for h in range(HG_HEADS)]
    chunks = [slice(c * HG_CHUNK, (c + 1) * HG_CHUNK) for c in range(n_chunks)]
    a_mat, upd, dec = {}, {}, []
    for c, rows in enumerate(chunks):
        b_last = hb[(c + 1) * HG_CHUNK - 1:(c + 1) * HG_CHUNK]
        kd = (hk[rows] * jnp.exp(b_last - hb[rows])).astype(BF16)
        dec.append(jnp.exp(b_last))
        for h, sl in enumerate(heads):
            a_mat[c, h] = jnp.where(causal, _dot_nt(qe[rows, sl], ke[rows, sl]), 0.0).astype(BF16)
            upd[c, h] = _dot_tn(kd[:, sl], vb[rows, sl])
    gate_mx = _branch_gate(xb, 2, w_in_ref)
    st = {(0, h): st_ref[h] for h in range(HG_HEADS)}
    for c in range(n_chunks):
        for h, sl in enumerate(heads):
            st[c + 1, h] = _col_bcast(dec[c][:, sl]) * st[c, h] + upd[c, h]
    for h in range(HG_HEADS):
        st_ref[h] = st[n_chunks, h]
    for c, rows in enumerate(chunks):
        for h, sl in enumerate(heads):
            oh_ref[rows, sl] = _dot(jnp.concatenate([qe[rows, sl], a_mat[c, h]], axis=1),
                                    jnp.concatenate([st[c, h].astype(BF16), vb[rows, sl]], axis=0))
    up_sw = _dot(o_sw.astype(BF16), w_up_ref[0])

    fillers = [lambda: proj(OFF_GATE + D_MODEL, D_MODEL // 2), lambda: proj(OFF_GATE + D_MODEL + D_MODEL // 2, D_MODEL // 2)]
    filled = []
    mk = mk_ref[0]
    mv = mv_ref[0]
    outs = []
    for h in range(MX_HEADS):
        sl = slice(h * MX_HEAD_DIM, (h + 1) * MX_HEAD_DIM)
        s = _dot_nt(mq[:, sl].astype(BF16), mk[:, sl]) * (MX_HEAD_DIM ** -0.5)
        if h % 2 == 0:
            filled.append(fillers[h // 2]())
        p = jnp.exp(s - jnp.max(s, axis=-1, keepdims=True))
        den = jnp.sum(p, axis=-1, keepdims=True)
        outs.append(_dot(p.astype(BF16), mv[:, sl]) / den)
    o_m = jnp.concatenate(outs, axis=-1)
    gate_h = _sigmoid(jnp.concatenate(filled, axis=-1))
    mix = gate_sw * up_sw + gate_mx * _dot(o_m.astype(BF16), w_up_ref[2])

    def redo_tile_exact():
        hq_ref[...] = hq
        hk_ref[...] = hk
        hv_ref[...] = hv
        hb_ref[...] = hb

        def chunk_body(c, carry):
            rows = pl.ds(pl.multiple_of(c * HG_CHUNK, HG_CHUNK), HG_CHUNK)
            for h in range(HG_HEADS):
                sl = slice(h * HG_DIM, (h + 1) * HG_DIM)
                o, st_new = _hgrn_chunk_exact(hq_ref[rows, sl], hk_ref[rows, sl], hv_ref[rows, sl], hb_ref[rows, sl],
                                              sto_ref[h])
                sto_ref[h] = st_new
                oh_ref[rows, sl] = o
            return carry

        lax.fori_loop(0, tm // HG_CHUNK, chunk_body, 0)
        st_ref[...] = sto_ref[...]

    if fallback:
        pl.when(jnp.min(hb) < -HG_SAFE_RANGE)(redo_tile_exact)

    mix = mix + gate_h * _dot(_rms_gate(oh_ref[...], hgate, hnw_ref[...]).astype(BF16), w_up_ref[1])
    y_ref[0] = _out_ffn(xn, mix, w_o_ref, w_f1_ref, w_f2_ref, lnp_ref, after_w_o=lambda: ln0_q(xnext_ref[0]))
    kv_proj()

    @pl.when(t == nt - 1)
    def _():
        for h in range(HG_HEADS):
            s_ref[0, h] = st_ref[h]


def _full(shape):
    nd = len(shape)
    return pl.BlockSpec(shape, lambda *_: (0,) * nd, pipeline_mode=pl.Buffered(1))


def _prompt_layer(x, mk_b, mv_b, sinks, lnp, hlb, hnw, w_in, w_up, w_o, w_f1, w_f2, *, tm, fallback):
    bsz, seq, _ = x.shape
    nt = seq // tm
    assert seq % tm == 0 and tm % WINDOW == 0 and tm % HG_CHUNK == 0
    bias = jnp.asarray(_prompt_bias())
    r = np.arange(tm)
    tri = jnp.asarray((r[:, None] // HG_CHUNK == r[None, :] // HG_CHUNK) & (r[:, None] >= r[None, :]), BF16)
    kern = functools.partial(_prompt_kernel, tm=tm, nt=nt, fallback=fallback)
    n = bsz * nt

    def tile_block(tile):
        return (tile // nt, tile % nt, 0)

    return pl.pallas_call(
        kern,
        grid=(n,),
        in_specs=[
            pl.BlockSpec(memory_space=pltpu.SMEM),
            pl.BlockSpec((1, tm, D_MODEL), lambda s: (0, 0, 0), pipeline_mode=pl.Buffered(1)),
            pl.BlockSpec((1, tm, D_MODEL), lambda s: tile_block(jnp.minimum(s + 1, n - 1))),
            pl.BlockSpec((1, MEM_LEN, MX_W), lambda s: (s // nt, 0, 0)),
            pl.BlockSpec((1, MEM_LEN, MX_W), lambda s: (s // nt, 0, 0)),
            _full(bias.shape), _full(tri.shape), _full(lnp.shape), _full(hlb.shape), _full(hnw.shape),
            _full(w_in.shape), _full(w_up.shape), _full(w_o.shape), _full(w_f1.shape), _full(w_f2.shape),
        ],
        out_specs=[
            pl.BlockSpec((1, tm, D_MODEL), lambda s: tile_block(s)),
            pl.BlockSpec((1, WINDOW, SW_KV), lambda s: (s // nt, 0, 0)),
            pl.BlockSpec((1, WINDOW, SW_KV), lambda s: (s // nt, 0, 0)),
            pl.BlockSpec((1, HG_HEADS, HG_DIM, HG_DIM), lambda s: (s // nt, 0, 0, 0)),
        ],
        out_shape=[
            jax.ShapeDtypeStruct((bsz, seq, D_MODEL), F32),
            jax.ShapeDtypeStruct((bsz, WINDOW, SW_KV), F32),
            jax.ShapeDtypeStruct((bsz, WINDOW, SW_KV), F32),
            jax.ShapeDtypeStruct((bsz, HG_HEADS, HG_DIM, HG_DIM), F32),
        ],
        scratch_shapes=[
            pltpu.VMEM((SW_KV_HEADS, WINDOW + tm, LANES), BF16),
            pltpu.VMEM((SW_KV_HEADS, WINDOW + tm, LANES), BF16),
            pltpu.VMEM((HG_HEADS, HG_DIM, HG_DIM), F32),
            pltpu.VMEM((HG_HEADS, HG_DIM, HG_DIM), F32),
            pltpu.VMEM((tm, HG_W), F32),
            pltpu.VMEM((tm, HG_W), F32),
            pltpu.VMEM((tm, HG_W), F32),
            pltpu.VMEM((tm, HG_W), F32),
            pltpu.VMEM((tm, HG_W), F32),
            pltpu.VMEM((tm, D_MODEL), F32),
            pltpu.VMEM((tm, OFF_HG), F32),
        ],
        compiler_params=pltpu.CompilerParams(
            dimension_semantics=("arbitrary",), vmem_limit_bytes=VMEM_LIMIT_BYTES),
        name="prompt_layer",
    )(sinks, x, x, mk_b, mv_b, bias, tri, lnp, hlb, hnw, w_in, w_up, w_o, w_f1, w_f2)


def _mem_kv_kernel(m_ref, w_ref, mk_ref, mv_ref, mkb_ref, mvb_ref):
    kv = _dot(m_ref[0].astype(BF16), w_ref[...])
    for h in range(MX_HEADS):
        head_rows = pl.ds(h, MEM_LEN, stride=MX_HEADS)
        mk_ref[0, head_rows, :] = kv[:, h * MX_HEAD_DIM:(h + 1) * MX_HEAD_DIM]
        mv_ref[0, head_rows, :] = kv[:, MX_W + h * MX_HEAD_DIM:MX_W + (h + 1) * MX_HEAD_DIM]
    mkb_ref[0] = kv[:, 0:MX_W].astype(BF16)
    mvb_ref[0] = kv[:, MX_W:2 * MX_W].astype(BF16)


def _mem_kv(mem, w_kv):
    bsz = mem.shape[0]
    blk = pl.BlockSpec((1, MEM_LEN, MX_W), lambda b: (b, 0, 0))
    row_blk = pl.BlockSpec((1, MEM_LEN * MX_HEADS, MX_HEAD_DIM), lambda b: (b, 0, 0))
    return pl.pallas_call(
        _mem_kv_kernel,
        grid=(bsz,),
        in_specs=[pl.BlockSpec((1, MEM_LEN, D_MODEL), lambda b: (b, 0, 0)), _full(w_kv.shape)],
        out_specs=[row_blk, row_blk, blk, blk],
        out_shape=[jax.ShapeDtypeStruct((bsz, MEM_LEN * MX_HEADS, MX_HEAD_DIM), F32)] * 2
        + [jax.ShapeDtypeStruct((bsz, MEM_LEN, MX_W), BF16)] * 2,
        compiler_params=pltpu.CompilerParams(dimension_semantics=("arbitrary",)),
        name="mem_kv",
    )(mem, w_kv)


def _prep_params(p):
    assert p['w_in'].shape[0] == DEPTH == 1
    zeros = jnp.zeros((D_MODEL,), F32)
    return dict(
        lnp=jnp.stack([p['ln0_w'], p['ln0_b'], p['ln1_w'][0], p['ln1_b'][0], p['ln2_w'][0], p['ln2_b'][0],
                       zeros, zeros]).astype(F32),
        sinks=p['sw_sinks'][0].astype(F32),
        hlb=p['hg_lower_bound'].astype(F32),
        hnw=p['hg_norm_w'].astype(F32),
        w_in=p['w_in'][0].astype(BF16),
        w_up=jnp.stack([p['w_up_sw'][0], p['w_up_hg'][0], p['w_up_mx'][0]]).astype(BF16),
        w_o=p['w_o'][0].astype(BF16),
        w_f1=p['w_ffn_in'][0].astype(BF16),
        w_f2=p['w_ffn_out'][0].astype(BF16),
        w_kv=p['w_mem_kv'][0].astype(BF16),
    )


def _prompt_group(x, mem, p, *, tm, pp=None):
    pp = _prep_params(p) if pp is None else pp
    mk, mv, mk_b, mv_b = _mem_kv(mem, pp['w_kv'])
    args = (x, mk_b, mv_b, pp['sinks'], pp['lnp'], pp['hlb'], pp['hnw'],
            pp['w_in'], pp['w_up'], pp['w_o'], pp['w_f1'], pp['w_f2'])
    lb_min = jnp.min(jax.nn.softmax(pp['hlb'], axis=0)[0])
    never_redo = HG_CHUNK * jnp.log(lb_min) >= -(HG_SAFE_RANGE - HG_BOUND_MARGIN)
    y, nk, nv, s = lax.cond(never_redo,
                            functools.partial(_prompt_layer, tm=tm, fallback=False),
                            functools.partial(_prompt_layer, tm=tm, fallback=True), *args)
    return y, nk, nv, s, mk, mv


def _sample_proj_kernel(x_ref, lnp_ref, w_in_ref, xn_ref, proj_ref):
    xn = _layer_norm(x_ref[...], lnp_ref[0:1], lnp_ref[1:2])
    xn_ref[...] = xn
    proj_ref[...] = _dot(xn.astype(BF16), w_in_ref[:, 0:OFF_GATE])


def _sample_proj(x2, lnp, w_in, *, tr):
    rows = x2.shape[0]
    return pl.pallas_call(
        _sample_proj_kernel,
        grid=(rows // tr,),
        in_specs=[pl.BlockSpec((tr, D_MODEL), lambda i: (i, 0)), _full(lnp.shape),
                  pl.BlockSpec((D_MODEL, OFF_GATE), lambda i: (0, 0), pipeline_mode=pl.Buffered(1))],
        out_specs=[pl.BlockSpec((tr, D_MODEL), lambda i: (i, 0)), pl.BlockSpec((tr, OFF_GATE), lambda i: (i, 0))],
        out_shape=[jax.ShapeDtypeStruct((rows, D_MODEL), F32), jax.ShapeDtypeStruct((rows, OFF_GATE), F32)],
        compiler_params=pltpu.CompilerParams(
            dimension_semantics=("arbitrary",), vmem_limit_bytes=VMEM_LIMIT_BYTES),
        name="sample_proj",
    )(x2, lnp, w_in)


def _sample_bias(t_new, n_keys):
    i = np.arange(t_new)[:, None]
    j = np.arange(n_keys)[None, :]
    kpos = np.where(j < WINDOW, PAST_LEN - WINDOW + j, PAST_LEN + j - WINDOW)
    dist = PAST_LEN + i - kpos
    valid = (dist >= 0) & (dist < WINDOW) & (j < WINDOW + t_new)
    out = np.zeros((SW_KV_HEADS, SW_GROUP, t_new, n_keys), np.float32)
    for h in range(SW_HEADS):
        slope = np.float32(2.0) ** np.float32(-8.0 * (h + 1) / SW_HEADS)
        out[h // SW_GROUP, h % SW_GROUP] = np.where(valid, -slope * dist.astype(np.float32), np.float32(NEG))
    return out.reshape(SW_KV_HEADS, SW_GROUP * t_new, n_keys)


def _sample_mix_kernel(sinks_ref, proj_ref, kbuf_ref, vbuf_ref, st_ref, mk_ref, mv_ref, bias_ref, hlb_ref, hnw_ref,
                       osw_ref, oh_ref, om_ref, nk_ref, nv_ref, ns_ref,
                       kk_ref, vv_ref, *, bt, t_new):
    nkeys = 2 * WINDOW
    r = bt * t_new
    elems = [slice(i * t_new, (i + 1) * t_new) for i in range(bt)]
    heads = [slice(h * HG_DIM, (h + 1) * HG_DIM) for h in range(HG_HEADS)]
    pad_rows = BF16_SUBLANES - t_new

    def pad(a):
        return jnp.concatenate([a, jnp.zeros((pad_rows, a.shape[1]), F32)], axis=0)

    def mem_head(ref, i, h):
        return ref[i, pl.ds(h, MEM_LEN, stride=MX_HEADS), :].astype(BF16)

    hq, hk, hv, hl, hgate = _hgrn_inputs(proj_ref[:, OFF_HG:OFF_MX], _lower_bound(hlb_ref[...]))
    ri = lax.broadcasted_iota(jnp.int32, (r, r), 0)
    ci = lax.broadcasted_iota(jnp.int32, (r, r), 1)
    same = (ri // t_new) == (ci // t_new)
    tri = jnp.where(jnp.logical_and(same, ri >= ci), 1.0, 0.0).astype(BF16)
    hb = _exact_dot(tri, hl)
    qe = hq * jnp.exp(hb)
    q_all = proj_ref[:, 0:SW_Q] * (SW_HEAD_DIM ** -0.5)
    mq_all = proj_ref[:, OFF_MX:OFF_GATE]

    zeros = jnp.zeros((nkeys - WINDOW - t_new, LANES), F32)
    for i, rows in enumerate(elems):
        for buf_ref, cache_ref, c0, out_ref in ((kk_ref, kbuf_ref, SW_Q, nk_ref),
                                                (vv_ref, vbuf_ref, SW_Q + SW_KV, nv_ref)):
            buf_ref[i, 0:WINDOW, :] = cache_ref[i]
            buf_ref[i, WINDOW:WINDOW + t_new, :] = proj_ref[rows, c0:c0 + SW_KV]
            buf_ref[i, WINDOW + t_new:nkeys, :] = zeros
            out_ref[i] = buf_ref[i, t_new:t_new + WINDOW, :]

    s_att, s_mem, o_state, upd, b_last = {}, {}, {}, {}, {}
    for i, rows in enumerate(elems):
        kds = _dup_heads(kk_ref[i])
        for g in range(SW_KV_HEADS):
            s_att[i, g] = _dot_nt(_stack_group_queries(q_all[rows], g), kds[g].astype(BF16)) + bias_ref[g]
    for i, rows in enumerate(elems):
        mq = pad(mq_all[rows]).astype(BF16)
        for h, sl in enumerate(heads):
            s_mem[i, h] = _dot_nt(mq[:, sl], mem_head(mk_ref, i, h)) * (MX_HEAD_DIM ** -0.5)
    for i, rows in enumerate(elems):
        b_last[i] = hb[(i + 1) * t_new - 1:(i + 1) * t_new]
        kd = pad(hk[rows] * jnp.exp(b_last[i] - hb[rows])).astype(BF16)
        vb = pad(hv[rows]).astype(BF16)
        qb = pad(qe[rows]).astype(BF16)
        for h, sl in enumerate(heads):
            o_state[i, h] = _dot(qb[:, sl], st_ref[i, h].astype(BF16))[0:t_new]
            upd[i, h] = _dot_tn(kd[:, sl], vb[:, sl])

    p_att, den_att, p_mem, den_mem = {}, {}, {}, {}
    for i in range(bt):
        for g in range(SW_KV_HEADS):
            ps, dens = [], []
            for hh in range(SW_GROUP):
                sh = s_att[i, g][hh * t_new:(hh + 1) * t_new]
                sink = sinks_ref[g * SW_GROUP + hh]
                m = jnp.maximum(jnp.max(sh, axis=-1, keepdims=True), sink)
                p = jnp.exp(sh - m)
                dens.append(jnp.sum(p, axis=-1, keepdims=True) + jnp.exp(sink - m))
                ps.append(p)
            p_att[i, g] = jnp.concatenate(ps, axis=0).astype(BF16)
            den_att[i, g] = jnp.concatenate(dens, axis=0)
        for h in range(MX_HEADS):
            p = jnp.exp(s_mem[i, h] - jnp.max(s_mem[i, h], axis=-1, keepdims=True))
            den_mem[i, h] = jnp.sum(p, axis=-1, keepdims=True)
            p_mem[i, h] = p.astype(BF16)
    for i, rows in enumerate(elems):
        dec = jnp.exp(b_last[i])
        for h, sl in enumerate(heads):
            ns_ref[i, h] = _col_bcast(dec[:, sl]) * st_ref[i, h] + upd[i, h]
            oh_ref[rows, sl] = o_state[i, h] + _hgrn_diag(hq[rows, sl], hk[rows, sl], hv[rows, sl], hb[rows, sl],
                                                          t_new)

    lo = _lane_lo((t_new, LANES))
    for i, rows in enumerate(elems):
        vds = _dup_heads(vv_ref[i])
        blocks = []
        for g in range(SW_KV_HEADS):
            o = _dot(p_att[i, g], vds[g].astype(BF16)) / den_att[i, g]
            blocks.append(jnp.where(lo, o[0:t_new], o[t_new:2 * t_new]))
            blocks.append(jnp.where(lo, o[2 * t_new:3 * t_new], o[3 * t_new:4 * t_new]))
        osw_ref[rows, :] = jnp.concatenate(blocks, axis=-1)
    for i, rows in enumerate(elems):
        outs = [(_dot(p_mem[i, h], mem_head(mv_ref, i, h)) / den_mem[i, h])[0:t_new] for h in range(MX_HEADS)]
        om_ref[rows, :] = jnp.concatenate(outs, axis=-1)
    oh_ref[...] = _rms_gate(oh_ref[...], hgate, hnw_ref[...])


def _sample_mix(proj, kbuf, vbuf, st, mk, mv, sinks, hlb, hnw, *, bt, t_new):
    nb = kbuf.shape[0]
    r = bt * t_new
    assert nb % bt == 0 and t_new == 8 and r % BF16_SUBLANES == 0
    bias = jnp.asarray(_sample_bias(t_new, 2 * WINDOW))
    kern = functools.partial(_sample_mix_kernel, bt=bt, t_new=t_new)
    row_blk = pl.BlockSpec((r, HG_W), lambda i: (i, 0))
    kv_blk = pl.BlockSpec((bt, WINDOW, SW_KV), lambda i: (i, 0, 0))
    st_blk = pl.BlockSpec((bt, HG_HEADS, HG_DIM, HG_DIM), lambda i: (i, 0, 0, 0))
    mem_blk = pl.BlockSpec((bt, MEM_LEN * MX_HEADS, MX_HEAD_DIM), lambda i: (i, 0, 0))
    return pl.pallas_call(
        kern,
        grid=(nb // bt,),
        in_specs=[pl.BlockSpec(memory_space=pltpu.SMEM),
                  pl.BlockSpec((r, OFF_GATE), lambda i: (i, 0)),
                  kv_blk, kv_blk, st_blk, mem_blk, mem_blk,
                  _full(bias.shape), _full(hlb.shape), _full(hnw.shape)],
        out_specs=[row_blk, row_blk, row_blk, kv_blk, kv_blk, st_blk],
        out_shape=[jax.ShapeDtypeStruct((nb * t_new, HG_W), F32)] * 3
        + [jax.ShapeDtypeStruct((nb, WINDOW, SW_KV), F32)] * 2
        + [jax.ShapeDtypeStruct((nb, HG_HEADS, HG_DIM, HG_DIM), F32)],
        scratch_shapes=[pltpu.VMEM((bt, 2 * WINDOW, LANES), F32)] * 2,
        compiler_params=pltpu.CompilerParams(
            dimension_semantics=("arbitrary",), vmem_limit_bytes=VMEM_LIMIT_BYTES),
        name="sample_mix",
    )(sinks, proj, kbuf, vbuf, st, mk, mv, bias, hlb, hnw)


def _sample_out_kernel(xn_ref, osw_ref, oh_ref, om_ref, lnp_ref, w_in_ref, w_up_ref, w_o_ref, w_f1_ref, w_f2_ref,
                       y_ref):
    xn = xn_ref[...]
    xb = xn.astype(BF16)
    mix = None
    for j, br_ref in enumerate((osw_ref, oh_ref, om_ref)):
        term = _branch_gate(xb, j, w_in_ref) * _dot(br_ref[...].astype(BF16), w_up_ref[j])
        mix = term if mix is None else mix + term
    y_ref[...] = _out_ffn(xn, mix, w_o_ref, w_f1_ref, w_f2_ref, lnp_ref)


def _sample_out(xn, osw, oh, om, lnp, w_in, w_up, w_o, w_f1, w_f2, *, tr):
    rows = xn.shape[0]
    wide = pl.BlockSpec((tr, D_MODEL), lambda i: (i, 0))
    half = pl.BlockSpec((tr, HG_W), lambda i: (i, 0))
    return pl.pallas_call(
        _sample_out_kernel,
        grid=(rows // tr,),
        in_specs=[wide, half, half, half, _full(lnp.shape), _full(w_in.shape), _full(w_up.shape),
                  _full(w_o.shape), _full(w_f1.shape), _full(w_f2.shape)],
        out_specs=wide,
        out_shape=jax.ShapeDtypeStruct((rows, D_MODEL), F32),
        compiler_params=pltpu.CompilerParams(
            dimension_semantics=("arbitrary",), vmem_limit_bytes=VMEM_LIMIT_BYTES),
        name="sample_out",
    )(xn, osw, oh, om, lnp, w_in, w_up, w_o, w_f1, w_f2)


def _sample_group(x, p, *, bt, pp=None):
    pp = _prep_params(p) if pp is None else pp
    nb, t_new, _ = x.shape
    rows = nb * t_new
    tr = min(rows, 256)
    xn, proj = _sample_proj(x.reshape(rows, D_MODEL), pp['lnp'], pp['w_in'], tr=tr)
    osw, oh, om, nk, nv, ns = _sample_mix(
        proj,
        p['cache_k_win'][0].reshape(nb, WINDOW, SW_KV), p['cache_v_win'][0].reshape(nb, WINDOW, SW_KV),
        p['state_hgrn'][0],
        p['cache_mem_k'][0].reshape(nb, MEM_LEN * MX_HEADS, MX_HEAD_DIM),
        p['cache_mem_v'][0].reshape(nb, MEM_LEN * MX_HEADS, MX_HEAD_DIM),
        pp['sinks'], pp['hlb'], pp['hnw'], bt=bt, t_new=t_new)
    y = _sample_out(xn, osw, oh, om, pp['lnp'], pp['w_in'], pp['w_up'], pp['w_o'], pp['w_f1'], pp['w_f2'], tr=tr)
    return y.reshape(nb, t_new, D_MODEL), nk, nv, ns


def kernel(x_prompt, x_sample, cache_k_win, cache_v_win, state_hgrn, cache_mem_k, cache_mem_v, mem_prompt,
           ln0_w, ln0_b, w_in, w_up_sw, w_up_hg, w_up_mx, sw_sinks, hg_lower_bound, hg_norm_w, w_mem_kv,
           w_o, ln1_w, ln1_b, w_ffn_in, w_ffn_out, ln2_w, ln2_b):
    p = dict(ln0_w=ln0_w, ln0_b=ln0_b, w_in=w_in, w_up_sw=w_up_sw, w_up_hg=w_up_hg, w_up_mx=w_up_mx,
             sw_sinks=sw_sinks, hg_lower_bound=hg_lower_bound, hg_norm_w=hg_norm_w, w_mem_kv=w_mem_kv,
             w_o=w_o, ln1_w=ln1_w, ln1_b=ln1_b, w_ffn_in=w_ffn_in, w_ffn_out=w_ffn_out, ln2_w=ln2_w, ln2_b=ln2_b)
    p.update(cache_k_win=cache_k_win, cache_v_win=cache_v_win, state_hgrn=state_hgrn,
             cache_mem_k=cache_mem_k, cache_mem_v=cache_mem_v)
    pp = _prep_params(p)
    bp, bs = x_prompt.shape[0], x_sample.shape[0]
    y, nk, nv, s, mk, mv = _prompt_group(x_prompt, mem_prompt, p, tm=PROMPT_TILE, pp=pp)
    ys, nks, nvs, ss = _sample_group(x_sample, p, bt=SAMPLE_BATCH_TILE, pp=pp)
    win = (WINDOW, SW_KV_HEADS, SW_HEAD_DIM)
    mem = (MEM_LEN, MX_HEADS, MX_HEAD_DIM)
    return (y, ys,
            nk.reshape(1, bp, *win), nv.reshape(1, bp, *win), s[None],
            mk.reshape(1, bp, *mem), mv.reshape(1, bp, *mem),
            nks.reshape(1, bs, *win), nvs.reshape(1, bs, *win), ss[None])
```

```python
import functools

import numpy as np
import jax
import jax.numpy as jnp
from jax import lax
from jax.experimental import pallas as pl
from jax.experimental.pallas import tpu as pltpu

D_MODEL = 1024
DEPTH = 1
PAST_LEN = 16384
SW_HEADS, SW_KV_HEADS, SW_HEAD_DIM = 8, 2, 64
SW_GROUP = SW_HEADS // SW_KV_HEADS
WINDOW = 128
HG_HEADS, HG_DIM = 4, 128
MEM_LEN, MX_HEADS, MX_HEAD_DIM = 256, 4, 128
SW_Q = SW_HEADS * SW_HEAD_DIM
SW_KV = SW_KV_HEADS * SW_HEAD_DIM
HG_W = HG_HEADS * HG_DIM
MX_W = MX_HEADS * MX_HEAD_DIM
D_FF = 2816
ALPHA = (2.0 * DEPTH) ** 0.25
LN_EPS = 1e-5
RMS_EPS = 1e-6
OFF_SW = 0
OFF_HG = SW_Q + 2 * SW_KV
OFF_MX = OFF_HG + 4 * HG_W
OFF_GATE = OFF_MX + MX_W
IN_WIDTH = OFF_GATE + 3 * D_MODEL

LANES = 128
BF16_SUBLANES = 16
MXU_TILE = 256
FFN_CHUNK = 4 * MXU_TILE
VMEM_LIMIT_BYTES = 60 * 1024 * 1024

NEG = -1e30
HG_CHUNK = 64
HG_SUB = 16
HG_SAFE_RANGE = 80.0
HG_BOUND_MARGIN = 1.0
PROMPT_TILE = 256
SAMPLE_BATCH_TILE = 8
MEM_BATCH_TILE = 2
F32 = jnp.float32
BF16 = jnp.bfloat16


def _dot(a, b):
    return jnp.dot(a, b, preferred_element_type=F32)


def _dot_nt(a, b):
    return lax.dot_general(a, b, (((1,), (1,)), ((), ())), preferred_element_type=F32)


def _dot_tn(a, b):
    return lax.dot_general(a, b, (((0,), (0,)), ((), ())), preferred_element_type=F32)


def _exact_dot(m_bf16, x):
    hi = x.astype(BF16)
    r1 = x - hi.astype(F32)
    mid = r1.astype(BF16)
    lo = (r1 - mid.astype(F32)).astype(BF16)
    return _dot(m_bf16, hi) + _dot(m_bf16, mid) + _dot(m_bf16, lo)


def _layer_norm(x, w, b):
    mu = jnp.mean(x, axis=-1, keepdims=True)
    xc = x - mu
    var = jnp.mean(xc * xc, axis=-1, keepdims=True)
    return xc * lax.rsqrt(var + LN_EPS) * w + b


def _sigmoid(x):
    return 1.0 / (1.0 + jnp.exp(-x))


def _silu(x):
    return x * _sigmoid(x)


def _lower_bound(hlb):
    m = jnp.max(hlb, axis=0, keepdims=True)
    e = jnp.exp(hlb - m)
    return e[0:1] / jnp.sum(e, axis=0, keepdims=True)


def _lane_lo(shape):
    return lax.broadcasted_iota(jnp.int32, shape, len(shape) - 1) < SW_HEAD_DIM


def _dup_heads(kv):
    rolled = pltpu.roll(kv, SW_HEAD_DIM, axis=1)
    lo = _lane_lo(kv.shape)
    return jnp.where(lo, kv, rolled), jnp.where(lo, rolled, kv)


def _sink_softmax_pv(s, vv, sinks_ref, g, rows):
    ps, dens = [], []
    for hh in range(SW_GROUP):
        sh = s[hh * rows:(hh + 1) * rows]
        sink = sinks_ref[g * SW_GROUP + hh]
        m = jnp.maximum(jnp.max(sh, axis=-1, keepdims=True), sink)
        p = jnp.exp(sh - m)
        dens.append(jnp.sum(p, axis=-1, keepdims=True) + jnp.exp(sink - m))
        ps.append(p)
    o = _dot(jnp.concatenate(ps, axis=0).astype(BF16), vv)
    o = o / jnp.concatenate(dens, axis=0)
    lo = _lane_lo((rows, LANES))
    return (jnp.where(lo, o[0:rows], o[rows:2 * rows]),
            jnp.where(lo, o[2 * rows:3 * rows], o[3 * rows:4 * rows]))


def _stack_group_queries(q, g):
    qa = q[:, 2 * g * LANES:(2 * g + 1) * LANES]
    qb = q[:, (2 * g + 1) * LANES:(2 * g + 2) * LANES]
    lo = _lane_lo(qa.shape)
    z = jnp.zeros_like(qa)
    return jnp.concatenate([jnp.where(lo, qa, z), jnp.where(lo, z, qa),
                            jnp.where(lo, qb, z), jnp.where(lo, z, qb)], axis=0).astype(BF16)


def _hgrn_diag(q, k, v, b, n):
    row = lax.broadcasted_iota(jnp.int32, (n, HG_DIM), 0)
    o = jnp.zeros((n, HG_DIM), F32)
    for s in range(n):
        t0 = (s // 8) * 8
        rr = row[t0:]
        e = jnp.exp(jnp.where(rr >= s, b[t0:] - b[s:s + 1], -jnp.inf))
        col = jnp.sum(q[t0:] * e * k[s:s + 1], axis=-1, keepdims=True)
        upd = col * v[s:s + 1]
        o = o + (upd if t0 == 0 else jnp.concatenate([jnp.zeros((t0, HG_DIM), F32), upd], axis=0))
    return o


def _col_bcast(rowvec):
    return jnp.transpose(jnp.broadcast_to(rowvec, (HG_DIM, HG_DIM)))


def _hgrn_chunk_exact(q, k, v, b, st):
    c = HG_CHUNK
    o = _dot((q * jnp.exp(b)).astype(BF16), st.astype(BF16))
    b_last = b[c - 1:c]
    kd = (k * jnp.exp(b_last - b)).astype(BF16)
    vb = v.astype(BF16)
    s_new = _col_bcast(jnp.exp(b_last)) * st + _dot_tn(kd, vb)
    parts = []
    for i in range(c // HG_SUB):
        r0 = i * HG_SUB
        od = _hgrn_diag(q[r0:r0 + HG_SUB], k[r0:r0 + HG_SUB], v[r0:r0 + HG_SUB], b[r0:r0 + HG_SUB], HG_SUB)
        if i > 0:
            g_i = b[r0 - 1:r0]
            qi = (q[r0:r0 + HG_SUB] * jnp.exp(b[r0:r0 + HG_SUB] - g_i)).astype(BF16)
            ki = (k[0:r0] * jnp.exp(g_i - b[0:r0])).astype(BF16)
            a = _dot_nt(qi, ki)
            od = od + _dot(a.astype(BF16), vb[0:r0])
        parts.append(od)
    return o + jnp.concatenate(parts, axis=0), s_new


def _rms_gate(o, gate, nw):
    outs = []
    for h in range(HG_HEADS):
        oh = o[:, h * HG_DIM:(h + 1) * HG_DIM]
        ms = jnp.mean(oh * oh, axis=-1, keepdims=True)
        outs.append(oh * lax.rsqrt(ms + RMS_EPS) * nw * gate[:, h * HG_DIM:(h + 1) * HG_DIM])
    return jnp.concatenate(outs, axis=-1)


def _branch_gate(xb, j, w_in_ref):
    c0 = OFF_GATE + j * D_MODEL
    return _sigmoid(_dot(xb, w_in_ref[:, c0:c0 + D_MODEL]))


def _out_ffn(xn, mix, w_o_ref, w_f1_ref, w_f2_ref, lnp_ref, after_w_o=None):
    u = ALPHA * xn + _dot(mix.astype(BF16), w_o_ref[...])
    if after_w_o is not None:
        after_w_o()
    h = _layer_norm(u, lnp_ref[2:3], lnp_ref[3:4])
    hb = h.astype(BF16)
    acc = None
    for c0 in range(0, D_FF, FFN_CHUNK):
        c1 = min(c0 + FFN_CHUNK, D_FF)
        gt = _dot(hb, w_f1_ref[:, c0:c1])
        up = _dot(hb, w_f1_ref[:, D_FF + c0:D_FF + c1])
        part = _dot((_silu(gt) * up).astype(BF16), w_f2_ref[c0:c1, :])
        acc = part if acc is None else acc + part
    return _layer_norm(ALPHA * h + acc, lnp_ref[4:5], lnp_ref[5:6])


def _hgrn_inputs(hp, lb):
    f = lb + (1.0 - lb) * _sigmoid(hp[:, HG_W:2 * HG_W])
    return (_silu(hp[:, 0:HG_W]), 1.0 - f, hp[:, 2 * HG_W:3 * HG_W], jnp.log(f), _silu(hp[:, 3 * HG_W:4 * HG_W]))


def _prompt_bias():
    i = np.arange(WINDOW)[:, None]
    j = np.arange(2 * WINDOW)[None, :]
    dist = i + WINDOW - j
    valid = (dist >= 0) & (dist < WINDOW)
    out = np.zeros((SW_KV_HEADS, SW_GROUP, WINDOW, 2 * WINDOW), np.float32)
    for h in range(SW_HEADS):
        slope = np.float32(2.0) ** np.float32(-8.0 * (h + 1) / SW_HEADS)
        out[h // SW_GROUP, h % SW_GROUP] = np.where(valid, -slope * dist.astype(np.float32), np.float32(NEG))
    return out.reshape(SW_KV_HEADS, SW_GROUP * WINDOW, 2 * WINDOW)


def _prompt_kernel(sinks_ref, x0_ref, xnext_ref, mk_ref, mv_ref, bias_ref, tri_ref, lnp_ref, hlb_ref, hnw_ref,
                   w_in_ref, w_up_ref, w_o_ref, w_f1_ref, w_f2_ref,
                   y_ref, nk_ref, nv_ref, s_ref,
                   kd_ref, vd_ref, st_ref, sto_ref, hq_ref, hk_ref, hv_ref, hb_ref, oh_ref, xn_sc, qkv_sc,
                   *, tm, nt, fallback):
    s_idx = pl.program_id(0)
    t = lax.rem(s_idx, nt)

    def ln0_q(x):
        x_norm = _layer_norm(x, lnp_ref[0:1], lnp_ref[1:2])
        xn_sc[...] = x_norm
        qkv_sc[:, 0:SW_Q] = _dot(x_norm.astype(BF16), w_in_ref[:, OFF_SW:OFF_SW + SW_Q])

    def kv_proj():
        qkv_sc[:, SW_Q:OFF_HG] = _dot(xn_sc[...].astype(BF16), w_in_ref[:, OFF_SW + SW_Q:OFF_HG])

    @pl.when(s_idx == 0)
    def _():
        ln0_q(x0_ref[0])
        kv_proj()

    @pl.when(t == 0)
    def _():
        kd_ref[:, 0:WINDOW, :] = jnp.zeros((SW_KV_HEADS, WINDOW, LANES), BF16)
        vd_ref[:, 0:WINDOW, :] = jnp.zeros((SW_KV_HEADS, WINDOW, LANES), BF16)
        st_ref[...] = jnp.zeros_like(st_ref)

    xn = xn_sc[...]
    xb = xn.astype(BF16)
    qkv = qkv_sc[...]
    q = qkv[:, 0:SW_Q] * (SW_HEAD_DIM ** -0.5)
    k = qkv[:, SW_Q:SW_Q + SW_KV]
    v = qkv[:, SW_Q + SW_KV:SW_Q + 2 * SW_KV]

    nk_ref[0] = k[tm - WINDOW:tm]
    nv_ref[0] = v[tm - WINDOW:tm]

    for g, dup in enumerate(_dup_heads(k)):
        kd_ref[g, WINDOW:WINDOW + tm, :] = dup.astype(BF16)
    for g, dup in enumerate(_dup_heads(v)):
        vd_ref[g, WINDOW:WINDOW + tm, :] = dup.astype(BF16)

    col = lax.broadcasted_iota(jnp.int32, (1, 2 * WINDOW), 1)
    first = jnp.where(jnp.logical_and(t == 0, col < WINDOW), NEG, 0.0).astype(F32)
    def proj(c0, width):
        return _dot(xb, w_in_ref[:, c0:c0 + width])

    hp_parts = []
    o_rows = []
    for j in range(tm // WINDOW):
        qj = q[j * WINDOW:(j + 1) * WINDOW]
        blocks = []
        for g in range(SW_KV_HEADS):
            kk = kd_ref[g, j * WINDOW:(j + 2) * WINDOW, :]
            vv = vd_ref[g, j * WINDOW:(j + 2) * WINDOW, :]
            s = _dot_nt(_stack_group_queries(qj, g), kk) + bias_ref[g]
            if j == 0:
                s = s + first
            if len(hp_parts) < 4:
                hp_parts.append(proj(OFF_HG + len(hp_parts) * HG_W, HG_W))
            blocks.extend(_sink_softmax_pv(s, vv, sinks_ref, g, WINDOW))
        o_rows.append(jnp.concatenate(blocks, axis=-1))
    o_sw = jnp.concatenate(o_rows, axis=0)
    while len(hp_parts) < 4:
        hp_parts.append(proj(OFF_HG + len(hp_parts) * HG_W, HG_W))
    kd_ref[:, 0:WINDOW, :] = kd_ref[:, tm:tm + WINDOW, :]
    vd_ref[:, 0:WINDOW, :] = vd_ref[:, tm:tm + WINDOW, :]

    hq, hk, hv, hl, hgate = _hgrn_inputs(jnp.concatenate(hp_parts, axis=-1), _lower_bound(hlb_ref[...]))
    hb = _exact_dot(tri_ref[...], hl)
    mq = proj(OFF_MX, MX_W)
    gate_sw = _branch_gate(xb, 0, w_in_ref)
    if fallback:
        sto_ref[...] = st_ref[...]
    qe = (hq * jnp.exp(hb)).astype(BF16)
    ke = (hk * jnp.exp(-hb)).astype(BF16)
    vb = hv.astype(BF16)
    ri = lax.broadcasted_iota(jnp.int32, (HG_CHUNK, HG_CHUNK), 0)
    ci = lax.broadcasted_iota(jnp.int32, (HG_CHUNK, HG_CHUNK), 1)
    causal = ri >= ci
    n_chunks = tm // HG_CHUNK
    heads = [slice(h * HG_DIM, (h + 1) * HG_DIM) for h in range(HG_HEADS)]
    chunks = [slice(c * HG_CHUNK, (c + 1) * HG_CHUNK) for c in range(n_chunks)]
    a_mat, upd, dec = {}, {}, []
    for c, rows in enumerate(chunks):
        b_last = hb[(c + 1) * HG_CHUNK - 1:(c + 1) * HG_CHUNK]
        kd = (hk[rows] * jnp.exp(b_last - hb[rows])).astype(BF16)
        dec.append(jnp.exp(b_last))
        for h, sl in enumerate(heads):
            a_mat[c, h] = jnp.where(causal, _dot_nt(qe[rows, sl], ke[rows, sl]), 0.0).astype(BF16)
            upd[c, h] = _dot_tn(kd[:, sl], vb[rows, sl])
    gate_mx = _branch_gate(xb, 2, w_in_ref)
    st = {(0, h): st_ref[h] for h in range(HG_HEADS)}
    for c in range(n_chunks):
        for h, sl in enumerate(heads):
            st[c + 1, h] = _col_bcast(dec[c][:, sl]) * st[c, h] + upd[c, h]
    for h in range(HG_HEADS):
        st_ref[h] = st[n_chunks, h]
    for c, rows in enumerate(chunks):
        for h, sl in enumerate(heads):
            oh_ref[rows, sl] = _dot(jnp.concatenate([qe[rows, sl], a_mat[c, h]], axis=1),
                                    jnp.concatenate([st[c, h].astype(BF16), vb[rows, sl]], axis=0))
    up_sw = _dot(o_sw.astype(BF16), w_up_ref[0])

    fillers = [lambda: proj(OFF_GATE + D_MODEL, D_MODEL // 2), lambda: proj(OFF_GATE + D_MODEL + D_MODEL // 2, D_MODEL // 2)]
    filled = []
    mk = mk_ref[0]
    mv = mv_ref[0]
    outs = []
    for h in range(MX_HEADS):
        sl = slice(h * MX_HEAD_DIM, (h + 1) * MX_HEAD_DIM)
        s = _dot_nt(mq[:, sl].astype(BF16), mk[:, sl]) * (MX_HEAD_DIM ** -0.5)
        if h % 2 == 0:
            filled.append(fillers[h // 2]())
        p = jnp.exp(s - jnp.max(s, axis=-1, keepdims=True))
        den = jnp.sum(p, axis=-1, keepdims=True)
        outs.append(_dot(p.astype(BF16), mv[:, sl]) / den)
    o_m = jnp.concatenate(outs, axis=-1)
    gate_h = _sigmoid(jnp.concatenate(filled, axis=-1))
    mix = gate_sw * up_sw + gate_mx * _dot(o_m.astype(BF16), w_up_ref[2])

    def redo_tile_exact():
        hq_ref[...] = hq
        hk_ref[...] = hk
        hv_ref[...] = hv
        hb_ref[...] = hb

        def chunk_body(c, carry):
            rows = pl.ds(pl.multiple_of(c * HG_CHUNK, HG_CHUNK), HG_CHUNK)
            for h in range(HG_HEADS):
                sl = slice(h * HG_DIM, (h + 1) * HG_DIM)
                o, st_new = _hgrn_chunk_exact(hq_ref[rows, sl], hk_ref[rows, sl], hv_ref[rows, sl], hb_ref[rows, sl],
                                              sto_ref[h])
                sto_ref[h] = st_new
                oh_ref[rows, sl] = o
            return carry

        lax.fori_loop(0, tm // HG_CHUNK, chunk_body, 0)
        st_ref[...] = sto_ref[...]

    if fallback:
        pl.when(jnp.min(hb) < -HG_SAFE_RANGE)(redo_tile_exact)

    mix = mix + gate_h * _dot(_rms_gate(oh_ref[...], hgate, hnw_ref[...]).astype(BF16), w_up_ref[1])
    y_ref[0] = _out_ffn(xn, mix, w_o_ref, w_f1_ref, w_f2_ref, lnp_ref, after_w_o=lambda: ln0_q(xnext_ref[0]))
    kv_proj()

    @pl.when(t == nt - 1)
    def _():
        for h in range(HG_HEADS):
            s_ref[0, h] = st_ref[h]


def _full(shape):
    nd = len(shape)
    return pl.BlockSpec(shape, lambda *_: (0,) * nd, pipeline_mode=pl.Buffered(1))


def _prompt_layer(x, mk_b, mv_b, sinks, lnp, hlb, hnw, w_in, w_up, w_o, w_f1, w_f2, *, tm, fallback):
    bsz, seq, _ = x.shape
    nt = seq // tm
    assert seq % tm == 0 and tm % WINDOW == 0 and tm % HG_CHUNK == 0
    bias = jnp.asarray(_prompt_bias())
    r = np.arange(tm)
    tri = jnp.asarray((r[:, None] // HG_CHUNK == r[None, :] // HG_CHUNK) & (r[:, None] >= r[None, :]), BF16)
    kern = functools.partial(_prompt_kernel, tm=tm, nt=nt, fallback=fallback)
    n = bsz * nt

    def tile_block(tile):
        return (tile // nt, tile % nt, 0)

    return pl.pallas_call(
        kern,
        grid=(n,),
        in_specs=[
            pl.BlockSpec(memory_space=pltpu.SMEM),
            pl.BlockSpec((1, tm, D_MODEL), lambda s: (0, 0, 0), pipeline_mode=pl.Buffered(1)),
            pl.BlockSpec((1, tm, D_MODEL), lambda s: tile_block(jnp.minimum(s + 1, n - 1))),
            pl.BlockSpec((1, MEM_LEN, MX_W), lambda s: (s // nt, 0, 0)),
            pl.BlockSpec((1, MEM_LEN, MX_W), lambda s: (s // nt, 0, 0)),
            _full(bias.shape), _full(tri.shape), _full(lnp.shape), _full(hlb.shape), _full(hnw.shape),
            _full(w_in.shape), _full(w_up.shape), _full(w_o.shape), _full(w_f1.shape), _full(w_f2.shape),
        ],
        out_specs=[
            pl.BlockSpec((1, tm, D_MODEL), lambda s: tile_block(s)),
            pl.BlockSpec((1, WINDOW, SW_KV), lambda s: (s // nt, 0, 0)),
            pl.BlockSpec((1, WINDOW, SW_KV), lambda s: (s // nt, 0, 0)),
            pl.BlockSpec((1, HG_HEADS, HG_DIM, HG_DIM), lambda s: (s // nt, 0, 0, 0)),
        ],
        out_shape=[
            jax.ShapeDtypeStruct((bsz, seq, D_MODEL), F32),
            jax.ShapeDtypeStruct((bsz, WINDOW, SW_KV), F32),
            jax.ShapeDtypeStruct((bsz, WINDOW, SW_KV), F32),
            jax.ShapeDtypeStruct((bsz, HG_HEADS, HG_DIM, HG_DIM), F32),
        ],
        scratch_shapes=[
            pltpu.VMEM((SW_KV_HEADS, WINDOW + tm, LANES), BF16),
            pltpu.VMEM((SW_KV_HEADS, WINDOW + tm, LANES), BF16),
            pltpu.VMEM((HG_HEADS, HG_DIM, HG_DIM), F32),
            pltpu.VMEM((HG_HEADS, HG_DIM, HG_DIM), F32),
            pltpu.VMEM((tm, HG_W), F32),
            pltpu.VMEM((tm, HG_W), F32),
            pltpu.VMEM((tm, HG_W), F32),
            pltpu.VMEM((tm, HG_W), F32),
            pltpu.VMEM((tm, HG_W), F32),
            pltpu.VMEM((tm, D_MODEL), F32),
            pltpu.VMEM((tm, OFF_HG), F32),
        ],
        compiler_params=pltpu.CompilerParams(
            dimension_semantics=("arbitrary",), vmem_limit_bytes=VMEM_LIMIT_BYTES),
        name="prompt_layer",
    )(sinks, x, x, mk_b, mv_b, bias, tri, lnp, hlb, hnw, w_in, w_up, w_o, w_f1, w_f2)


def _mem_kv_kernel(m_ref, w_ref, mk_ref, mv_ref, mkb_ref, mvb_ref, *, bt):
    kv_all = _dot(jnp.concatenate([m_ref[i] for i in range(bt)], axis=0).astype(BF16), w_ref[...])
    for i in range(bt):
        kv = kv_all[i * MEM_LEN:(i + 1) * MEM_LEN]
        for h in range(MX_HEADS):
            head_rows = pl.ds(h, MEM_LEN, stride=MX_HEADS)
            mk_ref[i, head_rows, :] = kv[:, h * MX_HEAD_DIM:(h + 1) * MX_HEAD_DIM]
            mv_ref[i, head_rows, :] = kv[:, MX_W + h * MX_HEAD_DIM:MX_W + (h + 1) * MX_HEAD_DIM]
        mkb_ref[i] = kv[:, 0:MX_W].astype(BF16)
        mvb_ref[i] = kv[:, MX_W:2 * MX_W].astype(BF16)


def _mem_kv(mem, w_kv):
    bsz = mem.shape[0]
    bt = MEM_BATCH_TILE if bsz % MEM_BATCH_TILE == 0 else 1
    blk = pl.BlockSpec((bt, MEM_LEN, MX_W), lambda b: (b, 0, 0))
    row_blk = pl.BlockSpec((bt, MEM_LEN * MX_HEADS, MX_HEAD_DIM), lambda b: (b, 0, 0))
    return pl.pallas_call(
        functools.partial(_mem_kv_kernel, bt=bt),
        grid=(bsz // bt,),
        in_specs=[pl.BlockSpec((bt, MEM_LEN, D_MODEL), lambda b: (b, 0, 0)), _full(w_kv.shape)],
        out_specs=[row_blk, row_blk, blk, blk],
        out_shape=[jax.ShapeDtypeStruct((bsz, MEM_LEN * MX_HEADS, MX_HEAD_DIM), F32)] * 2
        + [jax.ShapeDtypeStruct((bsz, MEM_LEN, MX_W), BF16)] * 2,
        compiler_params=pltpu.CompilerParams(dimension_semantics=("arbitrary",)),
        name="mem_kv",
    )(mem, w_kv)


def _prep_params(p):
    assert p['w_in'].shape[0] == DEPTH == 1
    zeros = jnp.zeros((D_MODEL,), F32)
    return dict(
        lnp=jnp.stack([p['ln0_w'], p['ln0_b'], p['ln1_w'][0], p['ln1_b'][0], p['ln2_w'][0], p['ln2_b'][0],
                       zeros, zeros]).astype(F32),
        sinks=p['sw_sinks'][0].astype(F32),
        hlb=p['hg_lower_bound'].astype(F32),
        hnw=p['hg_norm_w'].astype(F32),
        w_in=p['w_in'][0].astype(BF16),
        w_up=jnp.stack([p['w_up_sw'][0], p['w_up_hg'][0], p['w_up_mx'][0]]).astype(BF16),
        w_o=p['w_o'][0].astype(BF16),
        w_f1=p['w_ffn_in'][0].astype(BF16),
        w_f2=p['w_ffn_out'][0].astype(BF16),
        w_kv=p['w_mem_kv'][0].astype(BF16),
    )


def _prompt_group(x, mem, p, *, tm, pp=None):
    pp = _prep_params(p) if pp is None else pp
    mk, mv, mk_b, mv_b = _mem_kv(mem, pp['w_kv'])
    args = (x, mk_b, mv_b, pp['sinks'], pp['lnp'], pp['hlb'], pp['hnw'],
            pp['w_in'], pp['w_up'], pp['w_o'], pp['w_f1'], pp['w_f2'])
    lb_min = jnp.min(jax.nn.softmax(pp['hlb'], axis=0)[0])
    never_redo = HG_CHUNK * jnp.log(lb_min) >= -(HG_SAFE_RANGE - HG_BOUND_MARGIN)
    y, nk, nv, s = lax.cond(never_redo,
                            functools.partial(_prompt_layer, tm=tm, fallback=False),
                            functools.partial(_prompt_layer, tm=tm, fallback=True), *args)
    return y, nk, nv, s, mk, mv


def _sample_proj_kernel(x_ref, lnp_ref, w_in_ref, xn_ref, proj_ref):
    xn = _layer_norm(x_ref[...], lnp_ref[0:1], lnp_ref[1:2])
    xn_ref[...] = xn
    proj_ref[...] = _dot(xn.astype(BF16), w_in_ref[:, 0:OFF_GATE])


def _sample_proj(x2, lnp, w_in, *, tr):
    rows = x2.shape[0]
    return pl.pallas_call(
        _sample_proj_kernel,
        grid=(rows // tr,),
        in_specs=[pl.BlockSpec((tr, D_MODEL), lambda i: (i, 0)), _full(lnp.shape),
                  pl.BlockSpec((D_MODEL, OFF_GATE), lambda i: (0, 0), pipeline_mode=pl.Buffered(1))],
        out_specs=[pl.BlockSpec((tr, D_MODEL), lambda i: (i, 0)), pl.BlockSpec((tr, OFF_GATE), lambda i: (i, 0))],
        out_shape=[jax.ShapeDtypeStruct((rows, D_MODEL), F32), jax.ShapeDtypeStruct((rows, OFF_GATE), F32)],
        compiler_params=pltpu.CompilerParams(
            dimension_semantics=("arbitrary",), vmem_limit_bytes=VMEM_LIMIT_BYTES),
        name="sample_proj",
    )(x2, lnp, w_in)


def _sample_bias(t_new, n_keys):
    i = np.arange(t_new)[:, None]
    j = np.arange(n_keys)[None, :]
    kpos = np.where(j < WINDOW, PAST_LEN - WINDOW + j, PAST_LEN + j - WINDOW)
    dist = PAST_LEN + i - kpos
    valid = (dist >= 0) & (dist < WINDOW) & (j < WINDOW + t_new)
    out = np.zeros((SW_KV_HEADS, SW_GROUP, t_new, n_keys), np.float32)
    for h in range(SW_HEADS):
        slope = np.float32(2.0) ** np.float32(-8.0 * (h + 1) / SW_HEADS)
        out[h // SW_GROUP, h % SW_GROUP] = np.where(valid, -slope * dist.astype(np.float32), np.float32(NEG))
    return out.reshape(SW_KV_HEADS, SW_GROUP * t_new, n_keys)


def _sample_mix_kernel(sinks_ref, proj_ref, kbuf_ref, vbuf_ref, st_ref, mk_ref, mv_ref, bias_ref, hlb_ref, hnw_ref,
                       osw_ref, oh_ref, om_ref, nk_ref, nv_ref, ns_ref,
                       kk_ref, vv_ref, *, bt, t_new):
    nkeys = 2 * WINDOW
    r = bt * t_new
    elems = [slice(i * t_new, (i + 1) * t_new) for i in range(bt)]
    heads = [slice(h * HG_DIM, (h + 1) * HG_DIM) for h in range(HG_HEADS)]
    pad_rows = BF16_SUBLANES - t_new

    def pad(a):
        return jnp.concatenate([a, jnp.zeros((pad_rows, a.shape[1]), F32)], axis=0)

    def mem_head(ref, i, h):
        return ref[i, pl.ds(h, MEM_LEN, stride=MX_HEADS), :].astype(BF16)

    hq, hk, hv, hl, hgate = _hgrn_inputs(proj_ref[:, OFF_HG:OFF_MX], _lower_bound(hlb_ref[...]))
    ri = lax.broadcasted_iota(jnp.int32, (r, r), 0)
    ci = lax.broadcasted_iota(jnp.int32, (r, r), 1)
    same = (ri // t_new) == (ci // t_new)
    tri = jnp.where(jnp.logical_and(same, ri >= ci), 1.0, 0.0).astype(BF16)
    hb = _exact_dot(tri, hl)
    qe = hq * jnp.exp(hb)
    q_all = proj_ref[:, 0:SW_Q] * (SW_HEAD_DIM ** -0.5)
    mq_all = proj_ref[:, OFF_MX:OFF_GATE]

    zeros = jnp.zeros((nkeys - WINDOW - t_new, LANES), F32)
    for i, rows in enumerate(elems):
        for buf_ref, cache_ref, c0, out_ref in ((kk_ref, kbuf_ref, SW_Q, nk_ref),
                                                (vv_ref, vbuf_ref, SW_Q + SW_KV, nv_ref)):
            buf_ref[i, 0:WINDOW, :] = cache_ref[i]
            buf_ref[i, WINDOW:WINDOW + t_new, :] = proj_ref[rows, c0:c0 + SW_KV]
            buf_ref[i, WINDOW + t_new:nkeys, :] = zeros
            out_ref[i] = buf_ref[i, t_new:t_new + WINDOW, :]

    s_att, s_mem, o_state, upd, b_last = {}, {}, {}, {}, {}
    for i, rows in enumerate(elems):
        kds = _dup_heads(kk_ref[i])
        for g in range(SW_KV_HEADS):
            s_att[i, g] = _dot_nt(_stack_group_queries(q_all[rows], g), kds[g].astype(BF16)) + bias_ref[g]
    for i, rows in enumerate(elems):
        mq = pad(mq_all[rows]).astype(BF16)
        for h, sl in enumerate(heads):
            s_mem[i, h] = _dot_nt(mq[:, sl], mem_head(mk_ref, i, h)) * (MX_HEAD_DIM ** -0.5)
    for i, rows in enumerate(elems):
        b_last[i] = hb[(i + 1) * t_new - 1:(i + 1) * t_new]
        kd = pad(hk[rows] * jnp.exp(b_last[i] - hb[rows])).astype(BF16)
        vb = pad(hv[rows]).astype(BF16)
        qb = pad(qe[rows]).astype(BF16)
        for h, sl in enumerate(heads):
            o_state[i, h] = _dot(qb[:, sl], st_ref[i, h].astype(BF16))[0:t_new]
            upd[i, h] = _dot_tn(kd[:, sl], vb[:, sl])

    p_att, den_att, p_mem, den_mem = {}, {}, {}, {}
    for i in range(bt):
        for g in range(SW_KV_HEADS):
            ps, dens = [], []
            for hh in range(SW_GROUP):
                sh = s_att[i, g][hh * t_new:(hh + 1) * t_new]
                sink = sinks_ref[g * SW_GROUP + hh]
                m = jnp.maximum(jnp.max(sh, axis=-1, keepdims=True), sink)
                p = jnp.exp(sh - m)
                dens.append(jnp.sum(p, axis=-1, keepdims=True) + jnp.exp(sink - m))
                ps.append(p)
            p_att[i, g] = jnp.concatenate(ps, axis=0).astype(BF16)
            den_att[i, g] = jnp.concatenate(dens, axis=0)
        for h in range(MX_HEADS):
            p = jnp.exp(s_mem[i, h] - jnp.max(s_mem[i, h], axis=-1, keepdims=True))
            den_mem[i, h] = jnp.sum(p, axis=-1, keepdims=True)
            p_mem[i, h] = p.astype(BF16)
    for i, rows in enumerate(elems):
        dec = jnp.exp(b_last[i])
        for h, sl in enumerate(heads):
            ns_ref[i, h] = _col_bcast(dec[:, sl]) * st_ref[i, h] + upd[i, h]
            oh_ref[rows, sl] = o_state[i, h] + _hgrn_diag(hq[rows, sl], hk[rows, sl], hv[rows, sl], hb[rows, sl],
                                                          t_new)

    lo = _lane_lo((t_new, LANES))
    for i, rows in enumerate(elems):
        vds = _dup_heads(vv_ref[i])
        blocks = []
        for g in range(SW_KV_HEADS):
            o = _dot(p_att[i, g], vds[g].astype(BF16)) / den_att[i, g]
            blocks.append(jnp.where(lo, o[0:t_new], o[t_new:2 * t_new]))
            blocks.append(jnp.where(lo, o[2 * t_new:3 * t_new], o[3 * t_new:4 * t_new]))
        osw_ref[rows, :] = jnp.concatenate(blocks, axis=-1)
    for i, rows in enumerate(elems):
        outs = [(_dot(p_mem[i, h], mem_head(mv_ref, i, h)) / den_mem[i, h])[0:t_new] for h in range(MX_HEADS)]
        om_ref[rows, :] = jnp.concatenate(outs, axis=-1)
    oh_ref[...] = _rms_gate(oh_ref[...], hgate, hnw_ref[...])


def _sample_mix(proj, kbuf, vbuf, st, mk, mv, sinks, hlb, hnw, *, bt, t_new):
    nb = kbuf.shape[0]
    r = bt * t_new
    assert nb % bt == 0 and t_new == 8 and r % BF16_SUBLANES == 0
    bias = jnp.asarray(_sample_bias(t_new, 2 * WINDOW))
    kern = functools.partial(_sample_mix_kernel, bt=bt, t_new=t_new)
    row_blk = pl.BlockSpec((r, HG_W), lambda i: (i, 0))
    kv_blk = pl.BlockSpec((bt, WINDOW, SW_KV), lambda i: (i, 0, 0))
    st_blk = pl.BlockSpec((bt, HG_HEADS, HG_DIM, HG_DIM), lambda i: (i, 0, 0, 0))
    mem_blk = pl.BlockSpec((bt, MEM_LEN * MX_HEADS, MX_HEAD_DIM), lambda i: (i, 0, 0))
    return pl.pallas_call(
        kern,
        grid=(nb // bt,),
        in_specs=[pl.BlockSpec(memory_space=pltpu.SMEM),
                  pl.BlockSpec((r, OFF_GATE), lambda i: (i, 0)),
                  kv_blk, kv_blk, st_blk, mem_blk, mem_blk,
                  _full(bias.shape), _full(hlb.shape), _full(hnw.shape)],
        out_specs=[row_blk, row_blk, row_blk, kv_blk, kv_blk, st_blk],
        out_shape=[jax.ShapeDtypeStruct((nb * t_new, HG_W), F32)] * 3
        + [jax.ShapeDtypeStruct((nb, WINDOW, SW_KV), F32)] * 2
        + [jax.ShapeDtypeStruct((nb, HG_HEADS, HG_DIM, HG_DIM), F32)],
        scratch_shapes=[pltpu.VMEM((bt, 2 * WINDOW, LANES), F32)] * 2,
        compiler_params=pltpu.CompilerParams(
            dimension_semantics=("arbitrary",), vmem_limit_bytes=VMEM_LIMIT_BYTES),
        name="sample_mix",
    )(sinks, proj, kbuf, vbuf, st, mk, mv, bias, hlb, hnw)


def _sample_out_kernel(xn_ref, osw_ref, oh_ref, om_ref, lnp_ref, w_in_ref, w_up_ref, w_o_ref, w_f1_ref, w_f2_ref,
                       y_ref):
    xn = xn_ref[...]
    xb = xn.astype(BF16)
    mix = None
    for j, br_ref in enumerate((osw_ref, oh_ref, om_ref)):
        term = _branch_gate(xb, j, w_in_ref) * _dot(br_ref[...].astype(BF16), w_up_ref[j])
        mix = term if mix is None else mix + term
    y_ref[...] = _out_ffn(xn, mix, w_o_ref, w_f1_ref, w_f2_ref, lnp_ref)


def _sample_out(xn, osw, oh, om, lnp, w_in, w_up, w_o, w_f1, w_f2, *, tr):
    rows = xn.shape[0]
    wide = pl.BlockSpec((tr, D_MODEL), lambda i: (i, 0))
    half = pl.BlockSpec((tr, HG_W), lambda i: (i, 0))
    return pl.pallas_call(
        _sample_out_kernel,
        grid=(rows // tr,),
        in_specs=[wide, half, half, half, _full(lnp.shape), _full(w_in.shape), _full(w_up.shape),
                  _full(w_o.shape), _full(w_f1.shape), _full(w_f2.shape)],
        out_specs=wide,
        out_shape=jax.ShapeDtypeStruct((rows, D_MODEL), F32),
        compiler_params=pltpu.CompilerParams(
            dimension_semantics=("arbitrary",), vmem_limit_bytes=VMEM_LIMIT_BYTES),
        name="sample_out",
    )(xn, osw, oh, om, lnp, w_in, w_up, w_o, w_f1, w_f2)


def _sample_group(x, p, *, bt, pp=None):
    pp = _prep_params(p) if pp is None else pp
    nb, t_new, _ = x.shape
    rows = nb * t_new
    tr = min(rows, 256)
    xn, proj = _sample_proj(x.reshape(rows, D_MODEL), pp['lnp'], pp['w_in'], tr=tr)
    osw, oh, om, nk, nv, ns = _sample_mix(
        proj,
        p['cache_k_win'][0].reshape(nb, WINDOW, SW_KV), p['cache_v_win'][0].reshape(nb, WINDOW, SW_KV),
        p['state_hgrn'][0],
        p['cache_mem_k'][0].reshape(nb, MEM_LEN * MX_HEADS, MX_HEAD_DIM),
        p['cache_mem_v'][0].reshape(nb, MEM_LEN * MX_HEADS, MX_HEAD_DIM),
        pp['sinks'], pp['hlb'], pp['hnw'], bt=bt, t_new=t_new)
    y = _sample_out(xn, osw, oh, om, pp['lnp'], pp['w_in'], pp['w_up'], pp['w_o'], pp['w_f1'], pp['w_f2'], tr=tr)
    return y.reshape(nb, t_new, D_MODEL), nk, nv, ns


def kernel(x_prompt, x_sample, cache_k_win, cache_v_win, state_hgrn, cache_mem_k, cache_mem_v, mem_prompt,
           ln0_w, ln0_b, w_in, w_up_sw, w_up_hg, w_up_mx, sw_sinks, hg_lower_bound, hg_norm_w, w_mem_kv,
           w_o, ln1_w, ln1_b, w_ffn_in, w_ffn_out, ln2_w, ln2_b):
    p = dict(ln0_w=ln0_w, ln0_b=ln0_b, w_in=w_in, w_up_sw=w_up_sw, w_up_hg=w_up_hg, w_up_mx=w_up_mx,
             sw_sinks=sw_sinks, hg_lower_bound=hg_lower_bound, hg_norm_w=hg_norm_w, w_mem_kv=w_mem_kv,
             w_o=w_o, ln1_w=ln1_w, ln1_b=ln1_b, w_ffn_in=w_ffn_in, w_ffn_out=w_ffn_out, ln2_w=ln2_w, ln2_b=ln2_b)
    p.update(cache_k_win=cache_k_win, cache_v_win=cache_v_win, state_hgrn=state_hgrn,
             cache_mem_k=cache_mem_k, cache_mem_v=cache_mem_v)
    pp = _prep_params(p)
    bp, bs = x_prompt.shape[0], x_sample.shape[0]
    y, nk, nv, s, mk, mv = _prompt_group(x_prompt, mem_prompt, p, tm=PROMPT_TILE, pp=pp)
    ys, nks, nvs, ss = _sample_group(x_sample, p, bt=SAMPLE_BATCH_TILE, pp=pp)
    win = (WINDOW, SW_KV_HEADS, SW_HEAD_DIM)
    mem = (MEM_LEN, MX_HEADS, MX_HEAD_DIM)
    return (y, ys,
            nk.reshape(1, bp, *win), nv.reshape(1, bp, *win), s[None],
            mk.reshape(1, bp, *mem), mv.reshape(1, bp, *mem),
            nks.reshape(1, bs, *win), nvs.reshape(1, bs, *win), ss[None])
```
